```python
import math, functools
import jax, jax.numpy as jnp
from jax import lax
import numpy as np

D_MODEL = 4096
BATCH = 4
SEQ = 2048
DEPTH = 2
DEC_BATCH = 128
DEC_SEQ = 8
PAST_LEN = 16384
PAGE_SIZE = 128

BRANCH_W = D_MODEL // 2
N_BRANCH = 3
H_A = BRANCH_W // 128
DK_A = 128
DV_A = BRANCH_W // H_A
CONV_W = 4
CONV_DIM = 2 * H_A * DK_A + H_A * DV_A
GDN_CHUNK = 64
H_B = 8
DV_B = BRANCH_W // H_B
DK_B = DV_B // 2
RET_CHUNK = 64
ROPE_BASE = 10000.0
G_C = 8
DG_C = BRANCH_W // G_C
CHUNK_C = 128
D_FF = -(-8 * D_MODEL // (3 * 256)) * 256
EPS = 1e-6
IN_WIDTHS = (H_A * DK_A, H_A * DK_A, H_A * DV_A, H_A * DV_A, H_A, H_A,
             H_B * DK_B, H_B * DK_B, H_B * DV_B, H_B * DV_B,
             BRANCH_W, BRANCH_W, D_MODEL, D_MODEL, D_MODEL)
D_IN = sum(IN_WIDTHS)

kernel_name = 'hybrid_gdn_retention_gmlp_decode_step'

F32 = jnp.float32


def rms_norm(x, gain):
    x32 = x.astype(F32)
    y = x32 * lax.rsqrt(jnp.mean(x32 * x32, axis=-1, keepdims=True) + EPS)
    return (y * gain.astype(F32)).astype(x.dtype)


def l2_normalize(x):
    return x * lax.rsqrt(jnp.sum(x * x, axis=-1, keepdims=True) + EPS)


def to_chunks(t, n_chunks, c):
    n, _, h = t.shape[:3]
    t = t.reshape((n, n_chunks, c, h) + t.shape[3:])
    return jnp.moveaxis(t, (1, 3), (0, 2))


def from_chunks(o):
    nc, n, h, c, d = o.shape
    return jnp.moveaxis(o, (0, 2), (1, 3)).reshape(n, nc * c, h, d)


def causal_conv(x, buf, w):
    L = x.shape[1]
    xpad = jnp.concatenate([buf.astype(x.dtype), x], axis=1)
    xp32 = xpad.astype(F32)
    w32 = w.astype(F32)
    out = sum(xp32[:, j:j + L] * w32[j] for j in range(CONV_W))
    return out, xpad[:, L:]


def gated_delta_rule(q, k, v, g, beta, s0):
    N, L, H, _ = q.shape
    C = math.gcd(L, GDN_CHUNK)
    NC = L // C
    q, k, v, g, beta = (to_chunks(t, NC, C) for t in (q, k, v, g, beta))
    gc = jnp.cumsum(g, axis=-1)
    tri = jnp.tril(jnp.ones((C, C), bool))
    strict = jnp.tril(jnp.ones((C, C), bool), -1)
    decay = jnp.exp(jnp.where(tri, gc[..., :, None] - gc[..., None, :], -jnp.inf))
    kb = k * beta[..., None]
    a_strict = jnp.where(strict, jnp.einsum('cnhid,cnhjd->cnhij', kb, k) * decay, 0.0)
    t_mat = a_strict + jnp.eye(C, dtype=a_strict.dtype)
    solve = functools.partial(lax.linalg.triangular_solve, left_side=True, lower=True,
                              unit_diagonal=True)
    u = solve(t_mat, v * beta[..., None])
    w = solve(t_mat, kb * jnp.exp(gc)[..., None])
    attn = jnp.einsum('cnhid,cnhjd->cnhij', q, k) * decay
    q_dec = q * jnp.exp(gc)[..., None]
    k_dec = k * jnp.exp(gc[..., -1:] - gc)[..., None]
    chunk_decay = jnp.exp(gc[..., -1])

    def step(s, xs):
        u_c, w_c, attn_c, qd_c, kd_c, cd_c = xs
        v_new = u_c - jnp.einsum('nhik,nhkv->nhiv', w_c, s)
        o_c = (jnp.einsum('nhik,nhkv->nhiv', qd_c, s)
               + jnp.einsum('nhij,nhjv->nhiv', attn_c, v_new))
        s = s * cd_c[..., None, None] + jnp.einsum('nhjk,nhjv->nhkv', kd_c, v_new)
        return s, o_c

    s, o = lax.scan(step, s0, (u, w, attn, q_dec, k_dec, chunk_decay))
    return from_chunks(o), s


def retention_chunkwise(q, k, v, log_gamma, r0):
    N, L, H, _ = q.shape
    C = math.gcd(L, RET_CHUNK)
    NC = L // C
    q, k, v = (to_chunks(t, NC, C) for t in (q, k, v))
    idx = jnp.arange(C, dtype=F32)
    rel = idx[:, None] - idx[None, :]
    dmat = jnp.where(rel >= 0, jnp.exp(log_gamma[:, None, None] * jnp.maximum(rel, 0.0)), 0.0)
    inner = jnp.einsum('cnhij,cnhjv->cnhiv', jnp.einsum('cnhid,cnhjd->cnhij', q, k) * dmat, v)
    q_dec = q * jnp.exp(log_gamma[:, None] * (idx + 1.0))[:, :, None]
    k_dec = k * jnp.exp(log_gamma[:, None] * (C - 1.0 - idx))[:, :, None]
    chunk_decay = jnp.exp(log_gamma * C)

    def step(r, xs):
        qd_c, kd_c, v_c, in_c = xs
        o_c = in_c + jnp.einsum('nhik,nhkv->nhiv', qd_c, r)
        r = r * chunk_decay[:, None, None] + jnp.einsum('nhjk,nhjv->nhkv', kd_c, v_c)
        return r, o_c

    r, o = lax.scan(step, r0, (q_dec, k_dec, v, inner))
    return from_chunks(o), r


def rotary(x, pos):
    half = x.shape[-1] // 2
    inv_freq = 1.0 / (ROPE_BASE ** jnp.linspace(0.0, 1.0, half, dtype=F32))
    ang = pos.astype(F32)[:, None] * inv_freq[None, :]
    sin = jnp.sin(ang)[None, :, None, :]
    cos = jnp.cos(ang)[None, :, None, :]
    x1, x2 = x[..., 0::2], x[..., 1::2]
    return jnp.stack([x1 * cos - x2 * sin, x2 * cos + x1 * sin], axis=-1).reshape(x.shape)


def gdn_branch(q, k, v, z, a, b, conv_buf, conv_w, s0, a_log, dt_bias, norm_g):
    N, L, _ = q.shape
    qkv, new_buf = causal_conv(jnp.concatenate([q, k, v], axis=-1), conv_buf, conv_w)
    qkv = jax.nn.silu(qkv)
    qc, kc, vc = jnp.split(qkv, [H_A * DK_A, 2 * H_A * DK_A], axis=-1)
    qc = l2_normalize(qc.reshape(N, L, H_A, DK_A)) * (DK_A ** -0.5)
    kc = l2_normalize(kc.reshape(N, L, H_A, DK_A))
    vc = vc.reshape(N, L, H_A, DV_A)
    beta = jax.nn.sigmoid(b.astype(F32))
    g = -jnp.exp(a_log.astype(F32)) * jax.nn.softplus(a.astype(F32) + dt_bias.astype(F32))
    o, s = gated_delta_rule(qc, kc, vc, g, beta, s0.astype(F32))
    o = o * lax.rsqrt(jnp.mean(o * o, axis=-1, keepdims=True) + EPS) * norm_g.astype(F32)
    o = o * jax.nn.silu(z.astype(F32).reshape(N, L, H_A, DV_A))
    return o.reshape(N, L, H_A * DV_A).astype(q.dtype), new_buf, s.astype(s0.dtype)


def retention_branch(q, k, v, gate, pos, r0):
    N, L, _ = q.shape
    qr = rotary(q.reshape(N, L, H_B, DK_B).astype(F32), pos)
    kr = rotary(k.reshape(N, L, H_B, DK_B).astype(F32), pos) * (DK_B ** -0.5)
    vr = v.reshape(N, L, H_B, DV_B).astype(F32)
    log_gamma = jnp.log1p(-jnp.exp2(-5.0 - jnp.arange(H_B, dtype=F32)))
    o, r = retention_chunkwise(qr, kr, vr, log_gamma, r0.astype(F32))
    mu = jnp.mean(o, axis=-1, keepdims=True)
    var = jnp.mean(jnp.square(o - mu), axis=-1, keepdims=True)
    o = ((o - mu) * lax.rsqrt(var + EPS)).reshape(N, L, H_B * DV_B)
    o = jax.nn.silu(gate.astype(F32)) * o
    return o.astype(q.dtype), r.astype(r0.dtype)


def spatial_gating_branch(u, v, w_s, b_s, ln_g, ln_b):
    N, L, W = u.shape
    C = min(L, CHUNK_C)
    NC = L // C
    u32 = jax.nn.gelu(u.astype(F32), approximate=False)
    v32 = jax.nn.gelu(v.astype(F32), approximate=False)
    mu = jnp.mean(v32, axis=-1, keepdims=True)
    var = jnp.mean(jnp.square(v32 - mu), axis=-1, keepdims=True)
    v32 = (v32 - mu) * lax.rsqrt(var + EPS) * ln_g.astype(F32) + ln_b.astype(F32)
    w = jnp.where(jnp.tril(jnp.ones((C, C), bool)), w_s[:, :C, :C].astype(F32), 0.0)
    vb = v32.reshape(N, NC, C, G_C, DG_C)
    mixed = (jnp.einsum('gij,ncjgd->ncigd', w, vb)
             + b_s[:, :C].astype(F32).T[None, None, :, :, None])
    out = u32 * mixed.reshape(N, L, W)
    return out.astype(u.dtype), v32.astype(u.dtype)


def layer_forward(x, pos, gdn_s0, conv_buf, ret_r0, w_in, conv_w, a_log, dt_bias, gdn_norm,
                  w_s, b_s, ln_c_g, ln_c_b, w_br, w_o, g_pre_mix, g_post_mix, g_pre_ffn,
                  g_post_ffn, w_ffn_gate, w_ffn_up, w_ffn_down):
    h = rms_norm(x, g_pre_mix)
    proj = jnp.einsum('nld,de->nle', h, w_in)
    offsets = np.cumsum(IN_WIDTHS)[:-1].tolist()
    (q_a, k_a, v_a, z_a, a_a, b_a, q_b, k_b, v_b, g_b, u_c, v_c,
     gate_a, gate_b, gate_c) = jnp.split(proj, offsets, axis=-1)
    o_a, conv_new, gdn_new = gdn_branch(q_a, k_a, v_a, z_a, a_a, b_a, conv_buf, conv_w, gdn_s0,
                                        a_log, dt_bias, gdn_norm)
    o_b, ret_new = retention_branch(q_b, k_b, v_b, g_b, pos, ret_r0)
    o_c, v_rows = spatial_gating_branch(u_c, v_c, w_s, b_s, ln_c_g, ln_c_b)
    merged = (jax.nn.sigmoid(gate_a.astype(F32)) * (o_a @ w_br[0]).astype(F32)
              + jax.nn.sigmoid(gate_b.astype(F32)) * (o_b @ w_br[1]).astype(F32)
              + jax.nn.sigmoid(gate_c.astype(F32)) * (o_c @ w_br[2]).astype(F32))
    mix = merged.astype(x.dtype) @ w_o
    x = x + rms_norm(mix, g_post_mix)
    h = rms_norm(x, g_pre_ffn)
    f = (jax.nn.silu(h @ w_ffn_gate) * (h @ w_ffn_up)) @ w_ffn_down
    x = x + rms_norm(f, g_post_ffn)
    return x, gdn_new, conv_new, ret_new, v_rows


def setup_inputs(seed: int = 0) -> dict:
    key = jax.random.key(seed)
    ks = jax.random.split(key, 24)

    def nrm(k, shape, scale):
        return jax.random.normal(k, shape, F32) * scale

    def gain(k, shape):
        return 1.0 + 0.02 * jax.random.normal(k, shape, F32)

    return {
        'x_prompt': nrm(ks[0], (BATCH, SEQ, D_MODEL), 1.0),
        'x_sample': nrm(ks[1], (DEC_BATCH, DEC_SEQ, D_MODEL), 1.0),
        'state_gdn': nrm(ks[2], (DEPTH, DEC_BATCH, H_A, DK_A, DV_A), 0.1),
        'state_conv': nrm(ks[3], (DEPTH, DEC_BATCH, CONV_W - 1, CONV_DIM), 1.0),
        'state_ret': nrm(ks[4], (DEPTH, DEC_BATCH, H_B, DK_B, DV_B), 0.5),
        'w_in': nrm(ks[5], (DEPTH, D_MODEL, D_IN), D_MODEL ** -0.5),
        'conv_w': nrm(ks[6], (DEPTH, CONV_W, CONV_DIM), CONV_W ** -0.5),
        'a_log': jnp.log(jax.random.uniform(ks[7], (DEPTH, H_A), F32, 0.5, 8.0)),
        'dt_bias': nrm(ks[8], (DEPTH, H_A), 0.1),
        'gdn_norm': gain(ks[9], (DEPTH, DV_A)),
        'w_s': nrm(ks[10], (DEPTH, G_C, CHUNK_C, CHUNK_C), CHUNK_C ** -0.5),
        'b_s': gain(ks[11], (DEPTH, G_C, CHUNK_C)),
        'ln_c_g': gain(ks[12], (DEPTH, BRANCH_W)),
        'ln_c_b': nrm(ks[13], (DEPTH, BRANCH_W), 0.02),
        'w_br': nrm(ks[14], (DEPTH, N_BRANCH, BRANCH_W, D_MODEL), BRANCH_W ** -0.5),
        'w_o': nrm(ks[15], (DEPTH, D_MODEL, D_MODEL), D_MODEL ** -0.5),
        'g_pre_mix': gain(ks[16], (DEPTH, D_MODEL)),
        'g_post_mix': gain(ks[17], (DEPTH, D_MODEL)),
        'g_pre_ffn': gain(ks[18], (DEPTH, D_MODEL)),
        'g_post_ffn': gain(ks[19], (DEPTH, D_MODEL)),
        'w_ffn_gate': nrm(ks[20], (DEPTH, D_MODEL, D_FF), D_MODEL ** -0.5),
        'w_ffn_up': nrm(ks[21], (DEPTH, D_MODEL, D_FF), D_MODEL ** -0.5),
        'w_ffn_down': nrm(ks[22], (DEPTH, D_FF, D_MODEL), D_FF ** -0.5),
    }


def reference(x_prompt, x_sample, state_gdn, state_conv, state_ret, w_in, conv_w, a_log,
              dt_bias, gdn_norm, w_s, b_s, ln_c_g, ln_c_b, w_br, w_o, g_pre_mix, g_post_mix,
              g_pre_ffn, g_post_ffn, w_ffn_gate, w_ffn_up, w_ffn_down):
    n_p, l_p = x_prompt.shape[:2]
    pos_p = jnp.arange(l_p)
    pos_s = PAST_LEN + jnp.arange(x_sample.shape[1])
    zero_gdn = jnp.zeros((n_p, H_A, DK_A, DV_A), F32)
    zero_conv = jnp.zeros((n_p, CONV_W - 1, CONV_DIM), x_prompt.dtype)
    zero_ret = jnp.zeros((n_p, H_B, DK_B, DV_B), F32)
    yp, ys = x_prompt, x_sample
    gdn_p, gdn_s, conv_p, conv_s, ret_p, ret_s, vrows_s = [], [], [], [], [], [], []
    for l in range(DEPTH):
        weights = (w_in[l], conv_w[l], a_log[l], dt_bias[l], gdn_norm[l], w_s[l], b_s[l],
                   ln_c_g[l], ln_c_b[l], w_br[l], w_o[l], g_pre_mix[l], g_post_mix[l],
                   g_pre_ffn[l], g_post_ffn[l], w_ffn_gate[l], w_ffn_up[l], w_ffn_down[l])
        yp, sg, sc, sr, _ = layer_forward(yp, pos_p, zero_gdn, zero_conv, zero_ret, *weights)
        gdn_p.append(sg); conv_p.append(sc); ret_p.append(sr)
        ys, sg, sc, sr, vr = layer_forward(ys, pos_s, state_gdn[l], state_conv[l], state_ret[l],
                                           *weights)
        gdn_s.append(sg); conv_s.append(sc); ret_s.append(sr); vrows_s.append(vr)
    return (yp, ys, jnp.stack(gdn_p), jnp.stack(gdn_s), jnp.stack(conv_p), jnp.stack(conv_s),
            jnp.stack(ret_p), jnp.stack(ret_s), jnp.stack(vrows_s))
```

```python
import functools
import math

import jax
import jax.numpy as jnp
from jax import lax
from jax.experimental import pallas as pl
from jax.experimental.pallas import tpu as pltpu

F32 = jnp.float32
BF16 = jnp.bfloat16

D_MODEL = 4096
BRANCH_W = 2048
H_A, DK_A, DV_A = 16, 128, 128
CONV_W = 4
CONV_DIM = 6144
H_B, DK_B, DV_B = 8, 128, 256
G_C, DG_C, CHUNK_C = 8, 256, 128
D_FF = 11008
D_FF_PAD = 11264
EPS = 1e-6
ROPE_BASE = 10000.0
PAST_LEN = 16384
MIX_CHUNK = 64

COL_QKV, COL_Z, COL_QKB, COL_VB, COL_GB, COL_U, COL_V, COL_GATE = (
    0, 6144, 8192, 10240, 12288, 14336, 16384, 18432)
N_PROJ = 30720
VMEM_LIMIT = 56 * 1024 * 1024


def _cparams(sem):
    return pltpu.CompilerParams(dimension_semantics=sem, vmem_limit_bytes=VMEM_LIMIT)


def _bdot(a, b):
    return jnp.dot(a.astype(BF16), b.astype(BF16), preferred_element_type=F32)


def _bdot_nt(a, b):
    return lax.dot_general(a.astype(BF16), b.astype(BF16), (((1,), (1,)), ((), ())),
                           preferred_element_type=F32)


def _bdot_tn(a, b):
    return lax.dot_general(a.astype(BF16), b.astype(BF16), (((0,), (0,)), ((), ())),
                           preferred_element_type=F32)


def _mm_kernel(a_ref, b_ref, o_ref):
    o_ref[...] = jnp.dot(a_ref[...], b_ref[...], preferred_element_type=F32).astype(o_ref.dtype)


def _matmul(a, b, out_dtype, tm, tn):
    m, k = a.shape
    n = b.shape[1]
    return pl.pallas_call(
        _mm_kernel, grid=(m // tm, n // tn),
        in_specs=[pl.BlockSpec((tm, k), lambda i, j: (i, 0)),
                  pl.BlockSpec((k, tn), lambda i, j: (0, j))],
        out_specs=pl.BlockSpec((tm, tn), lambda i, j: (i, j)),
        out_shape=jax.ShapeDtypeStruct((m, n), out_dtype),
        compiler_params=_cparams(("parallel", "parallel")))(a, b)


def _mm_acc_kernel(a_ref, b_ref, o_ref, acc_ref):
    k = pl.program_id(2)

    @pl.when(k == 0)
    def _():
        acc_ref[...] = jnp.zeros_like(acc_ref)

    acc_ref[...] += jnp.dot(a_ref[...], b_ref[...], preferred_element_type=F32)

    @pl.when(k == pl.num_programs(2) - 1)
    def _():
        o_ref[...] = acc_ref[...].astype(o_ref.dtype)


def _matmul_kgrid(a, b, out_dtype, tm, tn, tk):
    m, k = a.shape
    n = b.shape[1]
    return pl.pallas_call(
        _mm_acc_kernel, grid=(m // tm, n // tn, k // tk),
        in_specs=[pl.BlockSpec((tm, tk), lambda i, j, kk: (i, kk)),
                  pl.BlockSpec((tk, tn), lambda i, j, kk: (kk, j))],
        out_specs=pl.BlockSpec((tm, tn), lambda i, j, kk: (i, j)),
        out_shape=jax.ShapeDtypeStruct((m, n), out_dtype),
        scratch_shapes=[pltpu.VMEM((tm, tn), F32)],
        compiler_params=_cparams(("parallel", "parallel", "arbitrary")))(a, b)


def _ffn_up_kernel(h_ref, wg_ref, wu_ref, o_ref):
    h = h_ref[...]
    g = jnp.dot(h, wg_ref[...], preferred_element_type=F32)
    u = jnp.dot(h, wu_ref[...], preferred_element_type=F32)
    o_ref[...] = (jax.nn.silu(g) * u).astype(o_ref.dtype)


def _ffn_up(h, wg, wu, tm, tn):
    m, k = h.shape
    n = wg.shape[1]
    return pl.pallas_call(
        _ffn_up_kernel, grid=(m // tm, n // tn),
        in_specs=[pl.BlockSpec((tm, k), lambda i, j: (i, 0)),
                  pl.BlockSpec((k, tn), lambda i, j: (0, j)),
                  pl.BlockSpec((k, tn), lambda i, j: (0, j))],
        out_specs=pl.BlockSpec((tm, tn), lambda i, j: (i, j)),
        out_shape=jax.ShapeDtypeStruct((m, n), BF16),
        compiler_params=_cparams(("parallel", "parallel")))(h, wg, wu)


def _merge_kernel(oa_ref, ob_ref, oc_ref, w_ref, gate_ref, o_ref, acc_ref):
    b = pl.program_id(2)
    sig = jax.nn.sigmoid(gate_ref[...])

    @pl.when(b == 0)
    def _():
        acc_ref[...] = sig * jnp.dot(oa_ref[...], w_ref[0], preferred_element_type=F32)

    @pl.when(b == 1)
    def _():
        acc_ref[...] += sig * jnp.dot(ob_ref[...], w_ref[0], preferred_element_type=F32)

    @pl.when(b == 2)
    def _():
        o_ref[...] = (acc_ref[...] + sig * jnp.dot(oc_ref[...], w_ref[0],
                                                   preferred_element_type=F32)).astype(o_ref.dtype)


def _merge(o_a, o_b, o_c, w_br, proj, tm, tn):
    m = o_a.shape[0]
    gate_blk0 = COL_GATE // tn
    per_branch = D_MODEL // tn
    o_spec = pl.BlockSpec((tm, BRANCH_W), lambda i, j, b: (i, 0))
    return pl.pallas_call(
        _merge_kernel, grid=(m // tm, D_MODEL // tn, 3),
        in_specs=[o_spec, o_spec, o_spec,
                  pl.BlockSpec((1, BRANCH_W, tn), lambda i, j, b: (b, 0, j)),
                  pl.BlockSpec((tm, tn), lambda i, j, b: (i, gate_blk0 + b * per_branch + j))],
        out_specs=pl.BlockSpec((tm, tn), lambda i, j, b: (i, j)),
        out_shape=jax.ShapeDtypeStruct((m, D_MODEL), BF16),
        scratch_shapes=[pltpu.VMEM((tm, tn), F32)],
        compiler_params=_cparams(("parallel", "parallel", "arbitrary")))(o_a, o_b, o_c, w_br, proj)


def _rms(x, g):
    return x * lax.rsqrt(jnp.mean(x * x, axis=-1, keepdims=True) + EPS) * g


def _prenorm_kernel(x_ref, g_ref, h_ref):
    h_ref[...] = _rms(x_ref[...], g_ref[...]).astype(h_ref.dtype)


def _prenorm(x, g, tr):
    m = x.shape[0]
    return pl.pallas_call(
        _prenorm_kernel, grid=(m // tr,),
        in_specs=[pl.BlockSpec((tr, D_MODEL), lambda i: (i, 0)),
                  pl.BlockSpec((1, D_MODEL), lambda i: (0, 0))],
        out_specs=pl.BlockSpec((tr, D_MODEL), lambda i: (i, 0)),
        out_shape=jax.ShapeDtypeStruct((m, D_MODEL), BF16),
        compiler_params=_cparams(("parallel",)))(x, g.reshape(1, D_MODEL))


def _post_kernel(x_ref, y_ref, gp_ref, gn_ref, xo_ref, *h_ref):
    xn = x_ref[...] + _rms(y_ref[...], gp_ref[...])
    xo_ref[...] = xn
    if h_ref:
        h_ref[0][...] = _rms(xn, gn_ref[...]).astype(BF16)


def _post(x, y, g_post, g_next, tr):
    m = x.shape[0]
    want_h = g_next is not None
    row = pl.BlockSpec((tr, D_MODEL), lambda i: (i, 0))
    vec = pl.BlockSpec((1, D_MODEL), lambda i: (0, 0))
    out_shape = [jax.ShapeDtypeStruct((m, D_MODEL), F32)]
    out_specs = [row]
    if want_h:
        out_shape.append(jax.ShapeDtypeStruct((m, D_MODEL), BF16))
        out_specs.append(row)
    gn = (g_next if want_h else g_post).reshape(1, D_MODEL)
    res = pl.pallas_call(
        _post_kernel, grid=(m // tr,),
        in_specs=[row, row, vec, vec], out_specs=out_specs, out_shape=out_shape,
        compiler_params=_cparams(("parallel",)))(x, y, g_post.reshape(1, D_MODEL), gn)
    return (res[0], res[1]) if want_h else (res[0], None)


def _gdn_prep_kernel(a_ref, b_ref, alog_ref, dt_ref, gc_ref, beta_ref, *, prompt_blocks):
    i = pl.program_id(0)
    x = a_ref[...] + dt_ref[...]
    softplus = jnp.maximum(x, 0.0) + jnp.log1p(jnp.exp(-jnp.abs(x)))
    g = -jnp.exp(alog_ref[...]) * softplus
    chunk = jnp.where(i < prompt_blocks, MIX_CHUNK, 8)
    rmod = lax.broadcasted_iota(jnp.int32, g.shape, 0) & (chunk - 1)
    s = 1
    while s < MIX_CHUNK:
        g = g + jnp.where(rmod >= s, pltpu.roll(g, s, axis=0), 0.0)
        s *= 2
    gc_ref[...] = g
    beta_ref[...] = jax.nn.sigmoid(b_ref[...])


def _gdn_prep(ab, a_log, dt_bias, tr, prompt_rows):
    m = ab.shape[0]
    pad = lambda v: jnp.pad(v.astype(F32), (0, 128 - H_A)).reshape(1, 128)
    blk = lambda c: pl.BlockSpec((tr, 128), lambda i: (i, c))
    vec = pl.BlockSpec((1, 128), lambda i: (0, 0))
    return pl.pallas_call(
        functools.partial(_gdn_prep_kernel, prompt_blocks=prompt_rows // tr), grid=(m // tr,),
        in_specs=[blk(0), blk(1), vec, vec], out_specs=[blk(0), blk(0)],
        out_shape=[jax.ShapeDtypeStruct((m, 128), F32)] * 2,
        compiler_params=_cparams(("parallel",)))(ab, ab, pad(a_log), pad(dt_bias))


def _conv_kernel(x_ref, prev_ref, buf_ref, w_ref, o_ref, xc_ref, *, tl):
    l = pl.program_id(1)
    sec = pl.program_id(2)
    xc_ref[8:, :] = x_ref[...]

    @pl.when(l == 0)
    def _():
        xc_ref[:8, :] = buf_ref[0]

    @pl.when(l > 0)
    def _():
        xc_ref[:8, :] = prev_ref[...]

    acc = xc_ref[pl.ds(8 - (CONV_W - 1), tl), :] * w_ref[0:1, :]
    for j in range(1, CONV_W):
        acc = acc + xc_ref[pl.ds(8 - (CONV_W - 1) + j, tl), :] * w_ref[j:j + 1, :]
    y = jax.nn.silu(acc)

    @pl.when(sec == 2)
    def _():
        o_ref[...] = y

    @pl.when(sec < 2)
    def _():
        scale = jnp.where(sec == 0, DK_A ** -0.5, 1.0).astype(F32)
        for h in range(H_A):
            seg = y[:, h * DK_A:(h + 1) * DK_A]
            inv = lax.rsqrt(jnp.sum(seg * seg, axis=-1, keepdims=True) + EPS)
            o_ref[:, h * DK_A:(h + 1) * DK_A] = seg * inv * scale


def _gdn_conv(proj, buf8, conv_w, row_off, n, l, tl):
    nl = l // tl
    base = row_off // tl
    base8 = row_off // 8
    sec_w = H_A * DK_A
    return pl.pallas_call(
        functools.partial(_conv_kernel, tl=tl), grid=(n, nl, 3),
        in_specs=[pl.BlockSpec((tl, sec_w), lambda i, j, s: (base + i * nl + j, s)),
                  pl.BlockSpec((8, sec_w),
                               lambda i, j, s: (jnp.maximum(base8 + (i * l + j * tl) // 8 - 1, 0), s)),
                  pl.BlockSpec((1, 8, sec_w), lambda i, j, s: (i, 0, s)),
                  pl.BlockSpec((CONV_W, sec_w), lambda i, j, s: (0, s))],
        out_specs=pl.BlockSpec((tl, sec_w), lambda i, j, s: (i * nl + j, s)),
        out_shape=jax.ShapeDtypeStruct((n * l, CONV_DIM), F32),
        scratch_shapes=[pltpu.VMEM((tl + 8, sec_w), F32)],
        compiler_params=_cparams(("parallel", "parallel", "parallel")))(proj, proj, buf8, conv_w)


def _gdn_kernel(qkv_ref, z_ref, gcol_ref, bcol_ref, grow_ref, s0_ref, ng_ref, o_ref, s_ref, *, c):
    @pl.when(pl.program_id(1) == 0)
    def _():
        s_ref[...] = s0_ref[...]

    ii = lax.broadcasted_iota(jnp.int32, (c, c), 0)
    jj = lax.broadcasted_iota(jnp.int32, (c, c), 1)
    tri = ii >= jj
    strict = ii > jj
    eye = (ii == jj).astype(F32)
    ng = ng_ref[...]
    for h in range(H_A):
        lo = h * DK_A
        q = qkv_ref[:, lo:lo + DK_A]
        k = qkv_ref[:, H_A * DK_A + lo:H_A * DK_A + lo + DK_A]
        v = qkv_ref[:, 2 * H_A * DK_A + lo:2 * H_A * DK_A + lo + DV_A]
        gc = gcol_ref[:, h:h + 1]
        beta = bcol_ref[:, h:h + 1]
        gr = grow_ref[0, 0, h:h + 1, :]
        glast = gr[:, c - 1:c]
        decay = jnp.exp(jnp.where(tri, gc - gr, -jnp.inf))
        egc = jnp.exp(gc)
        kb = k * beta
        m = _bdot_nt(jnp.concatenate([kb, q], axis=0), k)
        p = -jnp.where(strict, m[:c] * decay, 0.0)
        attn = m[c:] * decay
        t = eye + p
        n_sq = 1
        while 2 * n_sq < c:
            p = _bdot(p, p)
            t = t + _bdot(t, p)
            n_sq *= 2
        uw = _bdot(t, jnp.concatenate([v * beta, kb * egc], axis=1))
        u = uw[:, :DV_A]
        w = uw[:, DV_A:]
        s = s_ref[0, h]
        ws_qs = _bdot(jnp.concatenate([w, q * egc], axis=0), s)
        v_new = u - ws_qs[:c]
        o = ws_qs[c:] + _bdot(attn, v_new)
        kd = k * jnp.exp(glast - gc)
        s_ref[0, h] = s * jnp.exp(glast) + _bdot_tn(kd, v_new)
        o = o * lax.rsqrt(jnp.mean(o * o, axis=-1, keepdims=True) + EPS) * ng
        o = o * jax.nn.silu(z_ref[:, lo:lo + DV_A])
        o_ref[:, lo:lo + DV_A] = o.astype(o_ref.dtype)


def _gdn(qkvc, proj, gc, beta, grow, s0, norm_g, row_off, n, l, c, out_dtype):
    nc = l // c
    base = row_off // c
    return pl.pallas_call(
        functools.partial(_gdn_kernel, c=c), grid=(n, nc),
        in_specs=[pl.BlockSpec((c, CONV_DIM), lambda i, j: (i * nc + j, 0)),
                  pl.BlockSpec((c, BRANCH_W), lambda i, j: (base + i * nc + j, COL_Z // BRANCH_W)),
                  pl.BlockSpec((c, 128), lambda i, j: (base + i * nc + j, 0)),
                  pl.BlockSpec((c, 128), lambda i, j: (base + i * nc + j, 0)),
                  pl.BlockSpec((1, 1, H_A, c), lambda i, j: (i, j, 0, 0)),
                  pl.BlockSpec((1, H_A, DK_A, DV_A), lambda i, j: (i, 0, 0, 0)),
                  pl.BlockSpec((1, DV_A), lambda i, j: (0, 0))],
        out_specs=[pl.BlockSpec((c, BRANCH_W), lambda i, j: (i * nc + j, 0)),
                   pl.BlockSpec((1, H_A, DK_A, DV_A), lambda i, j: (i, 0, 0, 0))],
        out_shape=[jax.ShapeDtypeStruct((n * l, BRANCH_W), out_dtype),
                   jax.ShapeDtypeStruct((n, H_A, DK_A, DV_A), F32)],
        compiler_params=_cparams(("parallel", "arbitrary")))(
            qkvc, proj, gc, beta, grow, s0, norm_g.reshape(1, DV_A))


def _ret_kernel(qk_ref, v_ref, gate_ref, cos_ref, sin_ref, r0_ref, o_ref, r_ref, *, c):
    @pl.when(pl.program_id(1) == 0)
    def _():
        r_ref[...] = r0_ref[...]

    ii = lax.broadcasted_iota(jnp.int32, (c, c), 0)
    jj = lax.broadcasted_iota(jnp.int32, (c, c), 1)
    rel = (ii - jj).astype(F32)
    idx = lax.broadcasted_iota(jnp.int32, (c, 1), 0).astype(F32)
    even = (lax.broadcasted_iota(jnp.int32, (c, DK_B), 1) & 1) == 0
    cos = cos_ref[...]
    sin = sin_ref[...]

    def rot(x):
        swapped = jnp.where(even, pltpu.roll(x, DK_B - 1, axis=1), pltpu.roll(x, 1, axis=1))
        return x * cos + swapped * sin

    for h in range(H_B):
        lg = math.log1p(-2.0 ** (-5.0 - h))
        q = rot(qk_ref[:, h * DK_B:(h + 1) * DK_B])
        k = rot(qk_ref[:, H_B * DK_B + h * DK_B:H_B * DK_B + (h + 1) * DK_B]) * (DK_B ** -0.5)
        v = v_ref[:, h * DV_B:(h + 1) * DV_B]
        dmat = jnp.where(rel >= 0, jnp.exp(lg * jnp.maximum(rel, 0.0)), 0.0)
        inner = _bdot(_bdot_nt(q, k) * dmat, v)
        r = r_ref[0, h]
        o = inner + _bdot(q * jnp.exp(lg * (idx + 1.0)), r)
        r_ref[0, h] = r * math.exp(lg * c) + _bdot_tn(k * jnp.exp(lg * (c - 1.0 - idx)), v)
        mu = jnp.mean(o, axis=-1, keepdims=True)
        d = o - mu
        var = jnp.mean(d * d, axis=-1, keepdims=True)
        o = d * lax.rsqrt(var + EPS)
        o = jax.nn.silu(gate_ref[:, h * DV_B:(h + 1) * DV_B]) * o
        o_ref[:, h * DV_B:(h + 1) * DV_B] = o.astype(o_ref.dtype)


def _rope_tables(pos):
    half = DK_B // 2
    inv_freq = 1.0 / (ROPE_BASE ** jnp.linspace(0.0, 1.0, half, dtype=F32))
    ang = pos.astype(F32)[:, None] * inv_freq[None, :]
    sin = jnp.sin(ang)
    cos = jnp.cos(ang)
    cos2 = jnp.stack([cos, cos], axis=-1).reshape(-1, DK_B)
    sin2 = jnp.stack([-sin, sin], axis=-1).reshape(-1, DK_B)
    return cos2, sin2


def _retention(proj, cos2, sin2, r0, row_off, n, l, c, out_dtype):
    nc = l // c
    base = row_off // c
    col = lambda off: pl.BlockSpec((c, BRANCH_W), lambda i, j: (base + i * nc + j, off // BRANCH_W))
    tab = pl.BlockSpec((c, DK_B), lambda i, j: (j, 0))
    return pl.pallas_call(
        functools.partial(_ret_kernel, c=c), grid=(n, nc),
        in_specs=[col(COL_QKB), col(COL_VB), col(COL_GB), tab, tab,
                  pl.BlockSpec((1, H_B, DK_B, DV_B), lambda i, j: (i, 0, 0, 0))],
        out_specs=[pl.BlockSpec((c, BRANCH_W), lambda i, j: (i * nc + j, 0)),
                   pl.BlockSpec((1, H_B, DK_B, DV_B), lambda i, j: (i, 0, 0, 0))],
        out_shape=[jax.ShapeDtypeStruct((n * l, BRANCH_W), out_dtype),
                   jax.ShapeDtypeStruct((n, H_B, DK_B, DV_B), F32)],
        compiler_params=_cparams(("parallel", "arbitrary")))(proj, proj, proj, cos2, sin2, r0)


def _gelu(x):
    return 0.5 * x * (1.0 + lax.erf(x * (2.0 ** -0.5)))


def _gmlp_kernel(u_ref, v_ref, w_ref, bcol_ref, lng_ref, lnb_ref, o_ref, *vrows_ref, seq):
    u32 = _gelu(u_ref[...])
    v32 = _gelu(v_ref[...])
    mu = jnp.mean(v32, axis=-1, keepdims=True)
    d = v32 - mu
    var = jnp.mean(d * d, axis=-1, keepdims=True)
    vn = d * lax.rsqrt(var + EPS) * lng_ref[...] + lnb_ref[...]
    if vrows_ref:
        vrows_ref[0][...] = vn
    ii = lax.broadcasted_iota(jnp.int32, (CHUNK_C, CHUNK_C), 0)
    jj = lax.broadcasted_iota(jnp.int32, (CHUNK_C, CHUNK_C), 1)
    mask = (ii >= jj) & ((ii & -seq) == (jj & -seq))
    for g in range(G_C):
        w = jnp.where(mask, w_ref[g], 0.0)
        mixed = _bdot(w, vn[:, g * DG_C:(g + 1) * DG_C]) + bcol_ref[:, g:g + 1]
        o_ref[:, g * DG_C:(g + 1) * DG_C] = (u32[:, g * DG_C:(g + 1) * DG_C] * mixed).astype(o_ref.dtype)


def _gmlp(proj, w_tile, b_col, ln_g, ln_b, row_off, rows, seq, want_vrows):
    base = row_off // CHUNK_C
    col = lambda off: pl.BlockSpec((CHUNK_C, BRANCH_W), lambda i: (base + i, off // BRANCH_W))
    row = pl.BlockSpec((CHUNK_C, BRANCH_W), lambda i: (i, 0))
    vec = pl.BlockSpec((1, BRANCH_W), lambda i: (0, 0))
    out_shape = [jax.ShapeDtypeStruct((rows, BRANCH_W), BF16)]
    out_specs = [row]
    if want_vrows:
        out_shape.append(jax.ShapeDtypeStruct((rows, BRANCH_W), F32))
        out_specs.append(row)
    res = pl.pallas_call(
        functools.partial(_gmlp_kernel, seq=seq), grid=(rows // CHUNK_C,),
        in_specs=[col(COL_U), col(COL_V),
                  pl.BlockSpec((G_C, CHUNK_C, CHUNK_C), lambda i: (0, 0, 0)),
                  pl.BlockSpec((CHUNK_C, G_C), lambda i: (0, 0)), vec, vec],
        out_specs=out_specs, out_shape=out_shape,
        compiler_params=_cparams(("parallel",)))(
            proj, proj, w_tile, b_col, ln_g.reshape(1, BRANCH_W), ln_b.reshape(1, BRANCH_W))
    return res if want_vrows else (res[0], None)


def _pack_w_in(w):
    ab0 = 4 * BRANCH_W
    main = jnp.concatenate([w[:, :ab0], w[:, ab0 + 2 * H_A:]], axis=1).astype(BF16)
    a = jnp.pad(w[:, ab0:ab0 + H_A], ((0, 0), (0, 128 - H_A)))
    b = jnp.pad(w[:, ab0 + H_A:ab0 + 2 * H_A], ((0, 0), (0, 128 - H_A)))
    return main, jnp.concatenate([a, b], axis=1).astype(BF16)


def kernel(x_prompt, x_sample, state_gdn, state_conv, state_ret, w_in, conv_w, a_log, dt_bias, gdn_norm, w_s, b_s, ln_c_g, ln_c_b, w_br, w_o, g_pre_mix, g_post_mix, g_pre_ffn, g_post_ffn, w_ffn_gate, w_ffn_up, w_ffn_down):
    n_p, l_p, _ = x_prompt.shape
    n_s, l_s, _ = x_sample.shape
    depth = w_in.shape[0]
    rows_p, rows_s = n_p * l_p, n_s * l_s
    m = rows_p + rows_s
    x = jnp.concatenate([x_prompt.reshape(rows_p, D_MODEL), x_sample.reshape(rows_s, D_MODEL)], axis=0)

    cos_p, sin_p = _rope_tables(jnp.arange(l_p))
    cos_s, sin_s = _rope_tables(PAST_LEN + jnp.arange(l_s))
    zero_gdn = jnp.zeros((n_p, H_A, DK_A, DV_A), F32)
    zero_ret = jnp.zeros((n_p, H_B, DK_B, DV_B), F32)
    zero_buf8 = jnp.zeros((n_p, 8, CONV_DIM), F32)
    reps = CHUNK_C // l_s

    outs = {k: [] for k in ("gdn_p", "gdn_s", "conv_p", "conv_s", "ret_p", "ret_s", "vrows")}
    h = _prenorm(x, g_pre_mix[0], 512)
    for l in range(depth):
        w_main, w_ab = _pack_w_in(w_in[l])
        proj = _matmul(h, w_main, F32, 1024, 1024)
        ab = _matmul(h, w_ab, F32, 1024, 256)

        gc, beta = _gdn_prep(ab, a_log[l], dt_bias[l], 1024, rows_p)
        grow_p = gc[:rows_p, :H_A].reshape(n_p, l_p // MIX_CHUNK, MIX_CHUNK, H_A).transpose(0, 1, 3, 2)
        grow_s = gc[rows_p:, :H_A].reshape(n_s, 1, l_s, H_A).transpose(0, 1, 3, 2)
        buf8_s = jnp.pad(state_conv[l], ((0, 0), (8 - (CONV_W - 1), 0), (0, 0)))
        qkvc_p = _gdn_conv(proj, zero_buf8, conv_w[l], 0, n_p, l_p, 256)
        qkvc_s = _gdn_conv(proj, buf8_s, conv_w[l], rows_p, n_s, l_s, l_s)
        oa_p, sg_p = _gdn(qkvc_p, proj, gc, beta, grow_p, zero_gdn, gdn_norm[l], 0, n_p, l_p, MIX_CHUNK, BF16)
        oa_s, sg_s = _gdn(qkvc_s, proj, gc, beta, grow_s, state_gdn[l], gdn_norm[l], rows_p, n_s, l_s, l_s, F32)
        qkv_rows = proj[:, :CONV_DIM]
        outs["conv_p"].append(qkv_rows[:rows_p].reshape(n_p, l_p, CONV_DIM)[:, l_p - (CONV_W - 1):])
        outs["conv_s"].append(qkv_rows[rows_p:].reshape(n_s, l_s, CONV_DIM)[:, l_s - (CONV_W - 1):])
        outs["gdn_p"].append(sg_p)
        outs["gdn_s"].append(sg_s)

        ob_p, sr_p = _retention(proj, cos_p, sin_p, zero_ret, 0, n_p, l_p, MIX_CHUNK, BF16)
        ob_s, sr_s = _retention(proj, cos_s, sin_s, state_ret[l], rows_p, n_s, l_s, l_s, F32)
        outs["ret_p"].append(sr_p)
        outs["ret_s"].append(sr_s)

        oc_p, _ = _gmlp(proj, w_s[l], b_s[l].T, ln_c_g[l], ln_c_b[l], 0, rows_p, CHUNK_C, False)
        w_tile_s = jnp.tile(w_s[l][:, :l_s, :l_s], (1, reps, reps))
        b_col_s = jnp.tile(b_s[l][:, :l_s].T, (reps, 1))
        oc_s, vrows = _gmlp(proj, w_tile_s, b_col_s, ln_c_g[l], ln_c_b[l], rows_p, rows_s, l_s, True)
        outs["vrows"].append(vrows.reshape(n_s, l_s, BRANCH_W))

        o_a = jnp.concatenate([oa_p, oa_s.astype(BF16)], axis=0)
        o_b = jnp.concatenate([ob_p, ob_s.astype(BF16)], axis=0)
        o_c = jnp.concatenate([oc_p, oc_s], axis=0)
        merged = _merge(o_a, o_b, o_c, w_br[l].astype(BF16), proj, 512, 1024)
        mix = _matmul(merged, w_o[l].astype(BF16), F32, 1024, 1024)
        x, h = _post(x, mix, g_post_mix[l], g_pre_ffn[l], 256)

        pad_ff = ((0, 0), (0, D_FF_PAD - D_FF))
        wg = jnp.pad(w_ffn_gate[l].astype(BF16), pad_ff)
        wu = jnp.pad(w_ffn_up[l].astype(BF16), pad_ff)
        wd = jnp.pad(w_ffn_down[l].astype(BF16), ((0, D_FF_PAD - D_FF), (0, 0)))
        f1 = _ffn_up(h, wg, wu, 1024, 512)
        f = _matmul_kgrid(f1, wd, F32, 1024, 1024, D_FF_PAD // 4)
        x, h = _post(x, f, g_post_ffn[l], g_pre_mix[l + 1] if l + 1 < depth else None, 256)

    y_p = x[:rows_p].reshape(n_p, l_p, D_MODEL)
    y_s = x[rows_p:].reshape(n_s, l_s, D_MODEL)
    st = lambda k: jnp.stack(outs[k])
    return (y_p, y_s, st("gdn_p"), st("gdn_s"), st("conv_p"), st("conv_s"), st("ret_p"), st("ret_s"),
            st("vrows"))
```

```python
import functools
import math

import jax
import jax.numpy as jnp
from jax import lax
from jax.experimental import pallas as pl
from jax.experimental.pallas import tpu as pltpu

F32 = jnp.float32
BF16 = jnp.bfloat16

D_MODEL = 4096
BRANCH_W = 2048
H_A, DK_A, DV_A = 16, 128, 128
CONV_W = 4
CONV_DIM = 6144
H_B, DK_B, DV_B = 8, 128, 256
G_C, DG_C, CHUNK_C = 8, 256, 128
D_FF = 11008
D_FF_PAD = 11264
EPS = 1e-6
ROPE_BASE = 10000.0
PAST_LEN = 16384
MIX_CHUNK = 64
GDN_GROUP = 8

COL_QKV, COL_Z, COL_QKB, COL_VB, COL_GB, COL_U, COL_V, COL_GATE = (
    0, 6144, 8192, 10240, 12288, 14336, 16384, 18432)
N_PROJ = 30720
VMEM_LIMIT = 56 * 1024 * 1024


def _cparams(sem):
    return pltpu.CompilerParams(dimension_semantics=sem, vmem_limit_bytes=VMEM_LIMIT)


def _bdot(a, b):
    return jnp.dot(a.astype(BF16), b.astype(BF16), preferred_element_type=F32)


def _bdot_nt(a, b):
    return lax.dot_general(a.astype(BF16), b.astype(BF16), (((1,), (1,)), ((), ())),
                           preferred_element_type=F32)


def _bdot_tn(a, b):
    return lax.dot_general(a.astype(BF16), b.astype(BF16), (((0,), (0,)), ((), ())),
                           preferred_element_type=F32)


def _mm_kernel(a_ref, b_ref, o_ref):
    o_ref[...] = jnp.dot(a_ref[...], b_ref[...], preferred_element_type=F32).astype(o_ref.dtype)


def _matmul(a, b, out_dtype, tm, tn):
    m, k = a.shape
    n = b.shape[1]
    return pl.pallas_call(
        _mm_kernel, grid=(m // tm, n // tn),
        in_specs=[pl.BlockSpec((tm, k), lambda i, j: (i, 0)),
                  pl.BlockSpec((k, tn), lambda i, j: (0, j))],
        out_specs=pl.BlockSpec((tm, tn), lambda i, j: (i, j)),
        out_shape=jax.ShapeDtypeStruct((m, n), out_dtype),
        compiler_params=_cparams(("parallel", "parallel")))(a, b)


def _mm_ws_kernel(a_ref, b_ref, o_ref, bw_ref):
    @pl.when(pl.program_id(1) == 0)
    def _():
        bw_ref[...] = b_ref[...].astype(BF16)

    o_ref[...] = jnp.dot(a_ref[...], bw_ref[...], preferred_element_type=F32).astype(o_ref.dtype)


def _matmul_ws(a, b, layer, out_dtype, tm, tn):
    m, k = a.shape
    n = b.shape[2]
    return pl.pallas_call(
        _mm_ws_kernel, grid=(n // tn, m // tm),
        in_specs=[pl.BlockSpec((tm, k), lambda j, i: (i, 0)),
                  pl.BlockSpec((None, k, tn), lambda j, i: (layer, 0, j))],
        out_specs=pl.BlockSpec((tm, tn), lambda j, i: (i, j)),
        out_shape=jax.ShapeDtypeStruct((m, n), out_dtype),
        scratch_shapes=[pltpu.VMEM((k, tn), BF16)],
        compiler_params=_cparams(("arbitrary", "arbitrary")))(a, b)


def _mm_acc_kernel(a_ref, b_ref, o_ref, acc_ref):
    k = pl.program_id(2)

    @pl.when(k == 0)
    def _():
        acc_ref[...] = jnp.zeros_like(acc_ref)

    acc_ref[...] += jnp.dot(a_ref[...], b_ref[...], preferred_element_type=F32)

    @pl.when(k == pl.num_programs(2) - 1)
    def _():
        o_ref[...] = acc_ref[...].astype(o_ref.dtype)


def _matmul_kgrid(a, b, out_dtype, tm, tn, tk):
    m, k = a.shape
    n = b.shape[1]
    return pl.pallas_call(
        _mm_acc_kernel, grid=(m // tm, n // tn, k // tk),
        in_specs=[pl.BlockSpec((tm, tk), lambda i, j, kk: (i, kk)),
                  pl.BlockSpec((tk, tn), lambda i, j, kk: (kk, j))],
        out_specs=pl.BlockSpec((tm, tn), lambda i, j, kk: (i, j)),
        out_shape=jax.ShapeDtypeStruct((m, n), out_dtype),
        scratch_shapes=[pltpu.VMEM((tm, tn), F32)],
        compiler_params=_cparams(("parallel", "parallel", "arbitrary")))(a, b)


def _ffn_up_kernel(h_ref, wg_ref, wu_ref, o_ref, wgb_ref, wub_ref, *, tn, n_valid):
    j = pl.program_id(0)

    @pl.when(pl.program_id(1) == 0)
    def _():
        wgb_ref[...] = wg_ref[...].astype(BF16)
        wub_ref[...] = wu_ref[...].astype(BF16)

    h = h_ref[...]
    g = jnp.dot(h, wgb_ref[...], preferred_element_type=F32)
    u = jnp.dot(h, wub_ref[...], preferred_element_type=F32)
    col = j * tn + lax.broadcasted_iota(jnp.int32, g.shape, 1)
    o_ref[...] = jnp.where(col < n_valid, jax.nn.silu(g) * u, 0.0).astype(o_ref.dtype)


def _ffn_up(h, wg, wu, layer, tm, tn, n_out):
    m, k = h.shape
    n = wg.shape[2]
    wspec = pl.BlockSpec((None, k, tn), lambda j, i: (layer, 0, j))
    return pl.pallas_call(
        functools.partial(_ffn_up_kernel, tn=tn, n_valid=n), grid=(n_out // tn, m // tm),
        in_specs=[pl.BlockSpec((tm, k), lambda j, i: (i, 0)), wspec, wspec],
        out_specs=pl.BlockSpec((tm, tn), lambda j, i: (i, j)),
        out_shape=jax.ShapeDtypeStruct((m, n_out), BF16),
        scratch_shapes=[pltpu.VMEM((k, tn), BF16)] * 2,
        compiler_params=_cparams(("arbitrary", "arbitrary")))(h, wg, wu)


def _merge_kernel(oa_ref, ob_ref, oc_ref, w_ref, gate_ref, o_ref, acc_ref):
    b = pl.program_id(2)
    sig = jax.nn.sigmoid(gate_ref[...])

    @pl.when(b == 0)
    def _():
        acc_ref[...] = sig * jnp.dot(oa_ref[...], w_ref[0], preferred_element_type=F32)

    @pl.when(b == 1)
    def _():
        acc_ref[...] += sig * jnp.dot(ob_ref[...], w_ref[0], preferred_element_type=F32)

    @pl.when(b == 2)
    def _():
        o_ref[...] = (acc_ref[...] + sig * jnp.dot(oc_ref[...], w_ref[0],
                                                   preferred_element_type=F32)).astype(o_ref.dtype)


def _merge(o_a, o_b, o_c, w_br, proj, tm, tn):
    m = o_a.shape[0]
    gate_blk0 = COL_GATE // tn
    per_branch = D_MODEL // tn
    o_spec = pl.BlockSpec((tm, BRANCH_W), lambda i, j, b: (i, 0))
    return pl.pallas_call(
        _merge_kernel, grid=(m // tm, D_MODEL // tn, 3),
        in_specs=[o_spec, o_spec, o_spec,
                  pl.BlockSpec((1, BRANCH_W, tn), lambda i, j, b: (b, 0, j)),
                  pl.BlockSpec((tm, tn), lambda i, j, b: (i, gate_blk0 + b * per_branch + j))],
        out_specs=pl.BlockSpec((tm, tn), lambda i, j, b: (i, j)),
        out_shape=jax.ShapeDtypeStruct((m, D_MODEL), BF16),
        scratch_shapes=[pltpu.VMEM((tm, tn), F32)],
        compiler_params=_cparams(("parallel", "parallel", "arbitrary")))(o_a, o_b, o_c, w_br, proj)


def _rms(x, g):
    return x * lax.rsqrt(jnp.mean(x * x, axis=-1, keepdims=True) + EPS) * g


def _two_source_specs(tr, nb_first):
    first = pl.BlockSpec((tr, D_MODEL), lambda i: (jnp.minimum(i, nb_first - 1), 0))
    second = pl.BlockSpec((tr, D_MODEL), lambda i: (jnp.maximum(i - nb_first, 0), 0))
    return first, second


def _prenorm_kernel(xa_ref, xb_ref, g_ref, h_ref, *, nb_first):
    i = pl.program_id(0)

    @pl.when(i < nb_first)
    def _():
        h_ref[...] = _rms(xa_ref[...], g_ref[...]).astype(h_ref.dtype)

    @pl.when(i >= nb_first)
    def _():
        h_ref[...] = _rms(xb_ref[...], g_ref[...]).astype(h_ref.dtype)


def _prenorm(xa, xb, g, tr):
    ma, mb = xa.shape[0], xb.shape[0]
    sa, sb = _two_source_specs(tr, ma // tr)
    return pl.pallas_call(
        functools.partial(_prenorm_kernel, nb_first=ma // tr), grid=((ma + mb) // tr,),
        in_specs=[sa, sb, pl.BlockSpec((1, D_MODEL), lambda i: (0, 0))],
        out_specs=pl.BlockSpec((tr, D_MODEL), lambda i: (i, 0)),
        out_shape=jax.ShapeDtypeStruct((ma + mb, D_MODEL), BF16),
        compiler_params=_cparams(("arbitrary",)))(xa, xb, g.reshape(1, D_MODEL))


def _post_kernel(*refs, nb_first, n_src, want_h):
    x_refs = refs[:n_src]
    y_ref, gp_ref, gn_ref, xo_ref = refs[n_src:n_src + 4]
    r = _rms(y_ref[...], gp_ref[...])

    def finish(x_ref):
        xn = x_ref[...] + r
        xo_ref[...] = xn
        if want_h:
            refs[n_src + 4][...] = _rms(xn, gn_ref[...]).astype(BF16)

    if n_src == 1:
        finish(x_refs[0])
    else:
        i = pl.program_id(0)
        pl.when(i < nb_first)(lambda: finish(x_refs[0]))
        pl.when(i >= nb_first)(lambda: finish(x_refs[1]))


def _post(xs, y, g_post, g_next, tr):
    m = y.shape[0]
    want_h = g_next is not None
    row = pl.BlockSpec((tr, D_MODEL), lambda i: (i, 0))
    vec = pl.BlockSpec((1, D_MODEL), lambda i: (0, 0))
    nb_first = xs[0].shape[0] // tr
    x_specs = list(_two_source_specs(tr, nb_first)) if len(xs) == 2 else [row]
    out_shape = [jax.ShapeDtypeStruct((m, D_MODEL), F32)]
    out_specs = [row]
    if want_h:
        out_shape.append(jax.ShapeDtypeStruct((m, D_MODEL), BF16))
        out_specs.append(row)
    gn = (g_next if want_h else g_post).reshape(1, D_MODEL)
    res = pl.pallas_call(
        functools.partial(_post_kernel, nb_first=nb_first, n_src=len(xs), want_h=want_h), grid=(m // tr,),
        in_specs=x_specs + [row, vec, vec], out_specs=out_specs, out_shape=out_shape,
        compiler_params=_cparams(("arbitrary",)))(*xs, y, g_post.reshape(1, D_MODEL), gn)
    return (res[0], res[1]) if want_h else (res[0], None)


def _gdn_prep_kernel(a_ref, b_ref, alog_ref, dt_ref, gc_ref, beta_ref, *, prompt_blocks):
    i = pl.program_id(0)
    x = a_ref[...] + dt_ref[...]
    softplus = jnp.maximum(x, 0.0) + jnp.log1p(jnp.exp(-jnp.abs(x)))
    g = -jnp.exp(alog_ref[...]) * softplus
    chunk = jnp.where(i < prompt_blocks, MIX_CHUNK, 8)
    rmod = lax.broadcasted_iota(jnp.int32, g.shape, 0) & (chunk - 1)
    s = 1
    while s < MIX_CHUNK:
        g = g + jnp.where(rmod >= s, pltpu.roll(g, s, axis=0), 0.0)
        s *= 2
    gc_ref[...] = g
    beta_ref[...] = jax.nn.sigmoid(b_ref[...])


def _gdn_prep(ab, a_log, dt_bias, tr, prompt_rows):
    m = ab.shape[0]
    pad = lambda v: jnp.pad(v.astype(F32), (0, 128 - H_A)).reshape(1, 128)
    blk = lambda c: pl.BlockSpec((tr, 128), lambda i: (i, c))
    vec = pl.BlockSpec((1, 128), lambda i: (0, 0))
    return pl.pallas_call(
        functools.partial(_gdn_prep_kernel, prompt_blocks=prompt_rows // tr), grid=(m // tr,),
        in_specs=[blk(0), blk(1), vec, vec], out_specs=[blk(0), blk(0)],
        out_shape=[jax.ShapeDtypeStruct((m, 128), F32)] * 2,
        compiler_params=_cparams(("parallel",)))(ab, ab, pad(a_log), pad(dt_bias))


def _gdn_kernel(x_ref, z_ref, gcol_ref, bcol_ref, grow_ref, s0_ref, buf_ref, cw_ref, ng_ref,
                o_ref, s_ref, xc_ref, *, c, group):
    @pl.when(pl.program_id(1) == 0)
    def _():
        s_ref[...] = s0_ref[...]
        xc_ref[:8, :] = buf_ref[0]

    xc_ref[8:, :] = x_ref[...]

    def conv(lo):
        acc = xc_ref[pl.ds(8 - (CONV_W - 1), c), lo:lo + 128] * cw_ref[0:1, lo:lo + 128]
        for j in range(1, CONV_W):
            acc = acc + xc_ref[pl.ds(8 - (CONV_W - 1) + j, c), lo:lo + 128] * cw_ref[j:j + 1, lo:lo + 128]
        return jax.nn.silu(acc)

    def l2n(t):
        return t * lax.rsqrt(jnp.sum(t * t, axis=-1, keepdims=True) + EPS)

    ii = lax.broadcasted_iota(jnp.int32, (c, c), 0)
    jj = lax.broadcasted_iota(jnp.int32, (c, c), 1)
    tri = ii >= jj
    strict = ii > jj
    ng = ng_ref[...]
    hk = H_A * DK_A
    for h0 in range(0, H_A, group):
        heads = range(h0, h0 + group)
        st = []
        for h in heads:
            lo = h * DK_A
            q = l2n(conv(lo)) * (DK_A ** -0.5)
            k = l2n(conv(hk + lo))
            v = conv(2 * hk + lo)
            gc = gcol_ref[:, h:h + 1]
            beta = bcol_ref[:, h:h + 1]
            gr = grow_ref[0, 0, h:h + 1, :]
            glast = gr[:, c - 1:c]
            decay = jnp.exp(jnp.where(tri, gc - gr, -jnp.inf))
            egc = jnp.exp(gc)
            kb = k * beta
            kbf = k.astype(BF16)
            m = lax.dot_general(jnp.concatenate([kb, q], axis=0).astype(BF16), kbf,
                                (((1,), (1,)), ((), ())), preferred_element_type=F32)
            s = s_ref[0, h]
            sq = _bdot(jnp.concatenate([kb * egc, q * egc], axis=0), s)
            st.append(dict(lo=lo, m=m, sq=sq, s=s, decay=decay, vb=v * beta,
                           kd=(k * jnp.exp(glast - gc)).astype(BF16), cd=jnp.exp(glast)))
        for d in st:
            d["p"] = (-jnp.where(strict, d["m"][:c] * d["decay"], 0.0)).astype(BF16)
            d["attn"] = (d["m"][c:] * d["decay"]).astype(BF16)
            d["x"] = d["vb"] - d["sq"][:c]
        n_sq = 1
        while n_sq < c:
            last = 2 * n_sq >= c
            for d in st:
                d["x"] = d["x"] + jnp.dot(d["p"], d["x"].astype(BF16), preferred_element_type=F32)
            if not last:
                for d in st:
                    d["p"] = jnp.dot(d["p"], d["p"], preferred_element_type=F32).astype(BF16)
            n_sq *= 2
        for h, d in zip(heads, st):
            xb = d["x"].astype(BF16)
            o = d["sq"][c:] + jnp.dot(d["attn"], xb, preferred_element_type=F32)
            s_ref[0, h] = d["s"] * d["cd"] + lax.dot_general(
                d["kd"], xb, (((0,), (0,)), ((), ())), preferred_element_type=F32)
            lo = d["lo"]
            o = o * lax.rsqrt(jnp.mean(o * o, axis=-1, keepdims=True) + EPS) * ng
            o = o * jax.nn.silu(z_ref[:, lo:lo + DV_A])
            o_ref[:, lo:lo + DV_A] = o.astype(o_ref.dtype)
    xc_ref[:8, :] = xc_ref[c:c + 8, :]


def _gdn(proj, gc, beta, grow, s0, layer, buf8, conv_w, norm_g, row_off, n, l, c, group, out_dtype):
    nc = l // c
    base = row_off // c
    rows = lambda w, col: pl.BlockSpec((c, w), lambda i, j: (base + i * nc + j, col))
    return pl.pallas_call(
        functools.partial(_gdn_kernel, c=c, group=group), grid=(n, nc),
        in_specs=[rows(CONV_DIM, 0), rows(BRANCH_W, COL_Z // BRANCH_W), rows(128, 0), rows(128, 0),
                  pl.BlockSpec((1, 1, H_A, c), lambda i, j: (i, j, 0, 0)),
                  pl.BlockSpec((None, 1, H_A, DK_A, DV_A), lambda i, j: (layer, i, 0, 0, 0)),
                  pl.BlockSpec((1, 8, CONV_DIM), lambda i, j: (i, 0, 0)),
                  pl.BlockSpec((CONV_W, CONV_DIM), lambda i, j: (0, 0)),
                  pl.BlockSpec((1, DV_A), lambda i, j: (0, 0))],
        out_specs=[pl.BlockSpec((c, BRANCH_W), lambda i, j: (i * nc + j, 0)),
                   pl.BlockSpec((1, H_A, DK_A, DV_A), lambda i, j: (i, 0, 0, 0))],
        out_shape=[jax.ShapeDtypeStruct((n * l, BRANCH_W), out_dtype),
                   jax.ShapeDtypeStruct((n, H_A, DK_A, DV_A), F32)],
        scratch_shapes=[pltpu.VMEM((c + 8, CONV_DIM), F32)],
        compiler_params=_cparams(("parallel", "arbitrary")))(
            proj, proj, gc, beta, grow, s0, buf8, conv_w, norm_g.reshape(1, DV_A))


def _ret_kernel(qk_ref, v_ref, gate_ref, cos_ref, sin_ref, r0_ref, o_ref, r_ref, *, c):
    @pl.when(pl.program_id(1) == 0)
    def _():
        r_ref[...] = r0_ref[...]

    ii = lax.broadcasted_iota(jnp.int32, (c, c), 0)
    jj = lax.broadcasted_iota(jnp.int32, (c, c), 1)
    rel = (ii - jj).astype(F32)
    idx = lax.broadcasted_iota(jnp.int32, (c, 1), 0).astype(F32)
    even = (lax.broadcasted_iota(jnp.int32, (c, DK_B), 1) & 1) == 0
    cos = cos_ref[...]
    sin = sin_ref[...]

    def rot(x):
        swapped = jnp.where(even, pltpu.roll(x, DK_B - 1, axis=1), pltpu.roll(x, 1, axis=1))
        return x * cos + swapped * sin

    for h in range(H_B):
        lg = math.log1p(-2.0 ** (-5.0 - h))
        q = rot(qk_ref[:, h * DK_B:(h + 1) * DK_B])
        k = rot(qk_ref[:, H_B * DK_B + h * DK_B:H_B * DK_B + (h + 1) * DK_B]) * (DK_B ** -0.5)
        v = v_ref[:, h * DV_B:(h + 1) * DV_B]
        dmat = jnp.where(rel >= 0, jnp.exp(lg * jnp.maximum(rel, 0.0)), 0.0)
        inner = _bdot(_bdot_nt(q, k) * dmat, v)
        r = r_ref[0, h]
        o = inner + _bdot(q * jnp.exp(lg * (idx + 1.0)), r)
        r_ref[0, h] = r * math.exp(lg * c) + _bdot_tn(k * jnp.exp(lg * (c - 1.0 - idx)), v)
        mu = jnp.mean(o, axis=-1, keepdims=True)
        d = o - mu
        var = jnp.mean(d * d, axis=-1, keepdims=True)
        o = d * lax.rsqrt(var + EPS)
        o = jax.nn.silu(gate_ref[:, h * DV_B:(h + 1) * DV_B]) * o
        o_ref[:, h * DV_B:(h + 1) * DV_B] = o.astype(o_ref.dtype)


def _rope_tables(pos):
    half = DK_B // 2
    inv_freq = 1.0 / (ROPE_BASE ** jnp.linspace(0.0, 1.0, half, dtype=F32))
    ang = pos.astype(F32)[:, None] * inv_freq[None, :]
    sin = jnp.sin(ang)
    cos = jnp.cos(ang)
    cos2 = jnp.stack([cos, cos], axis=-1).reshape(-1, DK_B)
    sin2 = jnp.stack([-sin, sin], axis=-1).reshape(-1, DK_B)
    return cos2, sin2


def _retention(proj, cos2, sin2, r0, layer, row_off, n, l, c, out_dtype):
    nc = l // c
    base = row_off // c
    col = lambda off: pl.BlockSpec((c, BRANCH_W), lambda i, j: (base + i * nc + j, off // BRANCH_W))
    tab = pl.BlockSpec((c, DK_B), lambda i, j: (j, 0))
    return pl.pallas_call(
        functools.partial(_ret_kernel, c=c), grid=(n, nc),
        in_specs=[col(COL_QKB), col(COL_VB), col(COL_GB), tab, tab,
                  pl.BlockSpec((None, 1, H_B, DK_B, DV_B), lambda i, j: (layer, i, 0, 0, 0))],
        out_specs=[pl.BlockSpec((c, BRANCH_W), lambda i, j: (i * nc + j, 0)),
                   pl.BlockSpec((1, H_B, DK_B, DV_B), lambda i, j: (i, 0, 0, 0))],
        out_shape=[jax.ShapeDtypeStruct((n * l, BRANCH_W), out_dtype),
                   jax.ShapeDtypeStruct((n, H_B, DK_B, DV_B), F32)],
        compiler_params=_cparams(("parallel", "arbitrary")))(proj, proj, proj, cos2, sin2, r0)


def _gelu(x):
    return 0.5 * x * (1.0 + lax.erf(x * (2.0 ** -0.5)))


def _gmlp_kernel(u_ref, v_ref, w_ref, bcol_ref, lng_ref, lnb_ref, o_ref, *vrows_ref, seq):
    u32 = _gelu(u_ref[...])
    v32 = _gelu(v_ref[...])
    mu = jnp.mean(v32, axis=-1, keepdims=True)
    d = v32 - mu
    var = jnp.mean(d * d, axis=-1, keepdims=True)
    vn = d * lax.rsqrt(var + EPS) * lng_ref[...] + lnb_ref[...]
    if vrows_ref:
        vrows_ref[0][...] = vn
    ii = lax.broadcasted_iota(jnp.int32, (CHUNK_C, CHUNK_C), 0)
    jj = lax.broadcasted_iota(jnp.int32, (CHUNK_C, CHUNK_C), 1)
    mask = (ii >= jj) & ((ii & -seq) == (jj & -seq))
    for g in range(G_C):
        w = jnp.where(mask, w_ref[g], 0.0)
        mixed = _bdot(w, vn[:, g * DG_C:(g + 1) * DG_C]) + bcol_ref[:, g:g + 1]
        o_ref[:, g * DG_C:(g + 1) * DG_C] = (u32[:, g * DG_C:(g + 1) * DG_C] * mixed).astype(o_ref.dtype)


def _gmlp(proj, w_tile, b_col, ln_g, ln_b, row_off, rows, seq, want_vrows):
    base = row_off // CHUNK_C
    col = lambda off: pl.BlockSpec((CHUNK_C, BRANCH_W), lambda i: (base + i, off // BRANCH_W))
    row = pl.BlockSpec((CHUNK_C, BRANCH_W), lambda i: (i, 0))
    vec = pl.BlockSpec((1, BRANCH_W), lambda i: (0, 0))
    out_shape = [jax.ShapeDtypeStruct((rows, BRANCH_W), BF16)]
    out_specs = [row]
    if want_vrows:
        out_shape.append(jax.ShapeDtypeStruct((rows, BRANCH_W), F32))
        out_specs.append(row)
    res = pl.pallas_call(
        functools.partial(_gmlp_kernel, seq=seq), grid=(rows // CHUNK_C,),
        in_specs=[col(COL_U), col(COL_V),
                  pl.BlockSpec((G_C, CHUNK_C, CHUNK_C), lambda i: (0, 0, 0)),
                  pl.BlockSpec((CHUNK_C, G_C), lambda i: (0, 0)), vec, vec],
        out_specs=out_specs, out_shape=out_shape,
        compiler_params=_cparams(("parallel",)))(
            proj, proj, w_tile, b_col, ln_g.reshape(1, BRANCH_W), ln_b.reshape(1, BRANCH_W))
    return res if want_vrows else (res[0], None)


def _pack_w_in(w):
    ab0 = 4 * BRANCH_W
    main = jnp.concatenate([w[:, :ab0], w[:, ab0 + 2 * H_A:]], axis=1).astype(BF16)
    a = jnp.pad(w[:, ab0:ab0 + H_A], ((0, 0), (0, 128 - H_A)))
    b = jnp.pad(w[:, ab0 + H_A:ab0 + 2 * H_A], ((0, 0), (0, 128 - H_A)))
    return main, jnp.concatenate([a, b], axis=1).astype(BF16)


def kernel(x_prompt, x_sample, state_gdn, state_conv, state_ret, w_in, conv_w, a_log, dt_bias, gdn_norm, w_s, b_s, ln_c_g, ln_c_b, w_br, w_o, g_pre_mix, g_post_mix, g_pre_ffn, g_post_ffn, w_ffn_gate, w_ffn_up, w_ffn_down):
    n_p, l_p, _ = x_prompt.shape
    n_s, l_s, _ = x_sample.shape
    depth = w_in.shape[0]
    rows_p, rows_s = n_p * l_p, n_s * l_s
    xs = (x_prompt.reshape(rows_p, D_MODEL), x_sample.reshape(rows_s, D_MODEL))
    tr = 256

    cos_p, sin_p = _rope_tables(jnp.arange(l_p))
    cos_s, sin_s = _rope_tables(PAST_LEN + jnp.arange(l_s))
    zero_gdn = jnp.zeros((1, n_p, H_A, DK_A, DV_A), F32)
    zero_ret = jnp.zeros((1, n_p, H_B, DK_B, DV_B), F32)
    zero_buf8 = jnp.zeros((n_p, 8, CONV_DIM), F32)
    reps = CHUNK_C // l_s

    outs = {k: [] for k in ("gdn_p", "gdn_s", "conv_p", "conv_s", "ret_p", "ret_s", "vrows")}
    h = _prenorm(xs[0], xs[1], g_pre_mix[0], tr)
    for l in range(depth):
        w_main, w_ab = _pack_w_in(w_in[l])
        proj = _matmul(h, w_main, F32, 1024, 1024)
        ab = _matmul(h, w_ab, F32, 1024, 256)

        gc, beta = _gdn_prep(ab, a_log[l], dt_bias[l], 1024, rows_p)
        grow_p = gc[:rows_p, :H_A].reshape(n_p, l_p // MIX_CHUNK, MIX_CHUNK, H_A).transpose(0, 1, 3, 2)
        grow_s = gc[rows_p:, :H_A].reshape(n_s, 1, l_s, H_A).transpose(0, 1, 3, 2)
        buf8_s = jnp.pad(state_conv[l], ((0, 0), (8 - (CONV_W - 1), 0), (0, 0)))
        oa_p, sg_p = _gdn(proj, gc, beta, grow_p, zero_gdn, 0, zero_buf8, conv_w[l], gdn_norm[l],
                          0, n_p, l_p, MIX_CHUNK, GDN_GROUP, BF16)
        oa_s, sg_s = _gdn(proj, gc, beta, grow_s, state_gdn, l, buf8_s, conv_w[l], gdn_norm[l],
                          rows_p, n_s, l_s, l_s, GDN_GROUP, F32)
        qkv_rows = proj[:, :CONV_DIM]
        outs["conv_p"].append(qkv_rows[:rows_p].reshape(n_p, l_p, CONV_DIM)[:, l_p - (CONV_W - 1):])
        outs["conv_s"].append(qkv_rows[rows_p:].reshape(n_s, l_s, CONV_DIM)[:, l_s - (CONV_W - 1):])
        outs["gdn_p"].append(sg_p)
        outs["gdn_s"].append(sg_s)

        ob_p, sr_p = _retention(proj, cos_p, sin_p, zero_ret, 0, 0, n_p, l_p, MIX_CHUNK, BF16)
        ob_s, sr_s = _retention(proj, cos_s, sin_s, state_ret, l, rows_p, n_s, l_s, l_s, F32)
        outs["ret_p"].append(sr_p)
        outs["ret_s"].append(sr_s)

        oc_p, _ = _gmlp(proj, w_s[l], b_s[l].T, ln_c_g[l], ln_c_b[l], 0, rows_p, CHUNK_C, False)
        w_tile_s = jnp.tile(w_s[l][:, :l_s, :l_s], (1, reps, reps))
        b_col_s = jnp.tile(b_s[l][:, :l_s].T, (reps, 1))
        oc_s, vrows = _gmlp(proj, w_tile_s, b_col_s, ln_c_g[l], ln_c_b[l], rows_p, rows_s, l_s, True)
        outs["vrows"].append(vrows.reshape(n_s, l_s, BRANCH_W))

        o_a = jnp.concatenate([oa_p, oa_s.astype(BF16)], axis=0)
        o_b = jnp.concatenate([ob_p, ob_s.astype(BF16)], axis=0)
        o_c = jnp.concatenate([oc_p, oc_s], axis=0)
        merged = _merge(o_a, o_b, o_c, w_br[l].astype(BF16), proj, 1024, 1024)
        mix = _matmul_ws(merged, w_o, l, F32, 1024, 512)
        x, h = _post(xs, mix, g_post_mix[l], g_pre_ffn[l], tr)

        wd = jnp.pad(w_ffn_down[l].astype(BF16), ((0, D_FF_PAD - D_FF), (0, 0)))
        f1 = _ffn_up(h, w_ffn_gate, w_ffn_up, l, 512, 512, D_FF_PAD)
        f = _matmul_kgrid(f1, wd, F32, 1024, 1024, D_FF_PAD // 4)
        x, h = _post((x,), f, g_post_ffn[l], g_pre_mix[l + 1] if l + 1 < depth else None, tr)
        xs = (x,)

    y_p = x[:rows_p].reshape(n_p, l_p, D_MODEL)
    y_s = x[rows_p:].reshape(n_s, l_s, D_MODEL)
    st = lambda k: jnp.stack(outs[k])
    return (y_p, y_s, st("gdn_p"), st("gdn_s"), st("conv_p"), st("conv_s"), st("ret_p"), st("ret_s"),
            st("vrows"))
```

```python
import functools
import math

import jax
import jax.numpy as jnp
from jax import lax
from jax.experimental import pallas as pl
from jax.experimental.pallas import tpu as pltpu

F32 = jnp.float32
BF16 = jnp.bfloat16

D_MODEL = 4096
BRANCH_W = 2048
H_A, DK_A, DV_A = 16, 128, 128
CONV_W = 4
CONV_DIM = 6144
H_B, DK_B, DV_B = 8, 128, 256
G_C, DG_C, CHUNK_C = 8, 256, 128
D_FF = 11008
D_FF_PAD = 11264
EPS = 1e-6
ROPE_BASE = 10000.0
PAST_LEN = 16384
MIX_CHUNK = 64
RET_CHUNK = 128
GDN_GROUP = 8

COL_QKV, COL_Z, COL_QKB, COL_VB, COL_GB, COL_U, COL_V, COL_GATE = (
    0, 6144, 8192, 10240, 12288, 14336, 16384, 18432)
N_PROJ = 30720
VMEM_LIMIT = 56 * 1024 * 1024


def _cparams(sem):
    return pltpu.CompilerParams(dimension_semantics=sem, vmem_limit_bytes=VMEM_LIMIT)


def _bdot(a, b):
    return jnp.dot(a.astype(BF16), b.astype(BF16), preferred_element_type=F32)


def _bdot_nt(a, b):
    return lax.dot_general(a.astype(BF16), b.astype(BF16), (((1,), (1,)), ((), ())),
                           preferred_element_type=F32)


def _bdot_tn(a, b):
    return lax.dot_general(a.astype(BF16), b.astype(BF16), (((0,), (0,)), ((), ())),
                           preferred_element_type=F32)


def _mm_kernel(a_ref, b_ref, o_ref):
    o_ref[...] = jnp.dot(a_ref[...], b_ref[...], preferred_element_type=F32).astype(o_ref.dtype)


def _matmul(a, b, out_dtype, tm, tn):
    m, k = a.shape
    n = b.shape[1]
    return pl.pallas_call(
        _mm_kernel, grid=(m // tm, n // tn),
        in_specs=[pl.BlockSpec((tm, k), lambda i, j: (i, 0)),
                  pl.BlockSpec((k, tn), lambda i, j: (0, j))],
        out_specs=pl.BlockSpec((tm, tn), lambda i, j: (i, j)),
        out_shape=jax.ShapeDtypeStruct((m, n), out_dtype),
        compiler_params=_cparams(("parallel", "parallel")))(a, b)


def _mm_ws_kernel(a_ref, b_ref, o_ref, bw_ref):
    @pl.when(pl.program_id(1) == 0)
    def _():
        bw_ref[...] = b_ref[...].astype(BF16)

    o_ref[...] = jnp.dot(a_ref[...], bw_ref[...], preferred_element_type=F32).astype(o_ref.dtype)


def _matmul_ws(a, b, layer, out_dtype, tm, tn):
    m, k = a.shape
    n = b.shape[2]
    return pl.pallas_call(
        _mm_ws_kernel, grid=(n // tn, m // tm),
        in_specs=[pl.BlockSpec((tm, k), lambda j, i: (i, 0)),
                  pl.BlockSpec((None, k, tn), lambda j, i: (layer, 0, j))],
        out_specs=pl.BlockSpec((tm, tn), lambda j, i: (i, j)),
        out_shape=jax.ShapeDtypeStruct((m, n), out_dtype),
        scratch_shapes=[pltpu.VMEM((k, tn), BF16)],
        compiler_params=_cparams(("arbitrary", "arbitrary")))(a, b)


def _mm_acc_kernel(a_ref, b_ref, o_ref, acc_ref):
    k = pl.program_id(2)

    @pl.when(k == 0)
    def _():
        acc_ref[...] = jnp.zeros_like(acc_ref)

    acc_ref[...] += jnp.dot(a_ref[...], b_ref[...], preferred_element_type=F32)

    @pl.when(k == pl.num_programs(2) - 1)
    def _():
        o_ref[...] = acc_ref[...].astype(o_ref.dtype)


def _matmul_kgrid(a, b, out_dtype, tm, tn, tk):
    m, k = a.shape
    n = b.shape[1]
    return pl.pallas_call(
        _mm_acc_kernel, grid=(m // tm, n // tn, k // tk),
        in_specs=[pl.BlockSpec((tm, tk), lambda i, j, kk: (i, kk)),
                  pl.BlockSpec((tk, tn), lambda i, j, kk: (kk, j))],
        out_specs=pl.BlockSpec((tm, tn), lambda i, j, kk: (i, j)),
        out_shape=jax.ShapeDtypeStruct((m, n), out_dtype),
        scratch_shapes=[pltpu.VMEM((tm, tn), F32)],
        compiler_params=_cparams(("parallel", "parallel", "arbitrary")))(a, b)


def _ffn_up_kernel(h_ref, wg_ref, wu_ref, o_ref, wgb_ref, wub_ref, *, tn, n_valid):
    j = pl.program_id(0)

    @pl.when(pl.program_id(1) == 0)
    def _():
        wgb_ref[...] = wg_ref[...].astype(BF16)
        wub_ref[...] = wu_ref[...].astype(BF16)

    h = h_ref[...]
    g = jnp.dot(h, wgb_ref[...], preferred_element_type=F32)
    u = jnp.dot(h, wub_ref[...], preferred_element_type=F32)
    col = j * tn + lax.broadcasted_iota(jnp.int32, g.shape, 1)
    o_ref[...] = jnp.where(col < n_valid, jax.nn.silu(g) * u, 0.0).astype(o_ref.dtype)


def _ffn_up(h, wg, wu, layer, tm, tn, n_out):
    m, k = h.shape
    n = wg.shape[2]
    wspec = pl.BlockSpec((None, k, tn), lambda j, i: (layer, 0, j))
    return pl.pallas_call(
        functools.partial(_ffn_up_kernel, tn=tn, n_valid=n), grid=(n_out // tn, m // tm),
        in_specs=[pl.BlockSpec((tm, k), lambda j, i: (i, 0)), wspec, wspec],
        out_specs=pl.BlockSpec((tm, tn), lambda j, i: (i, j)),
        out_shape=jax.ShapeDtypeStruct((m, n_out), BF16),
        scratch_shapes=[pltpu.VMEM((k, tn), BF16)] * 2,
        compiler_params=_cparams(("arbitrary", "arbitrary")))(h, wg, wu)


def _merge_kernel(oa_ref, ob_ref, oc_ref, w_ref, gate_ref, o_ref, acc_ref):
    b = pl.program_id(2)
    sig = jax.nn.sigmoid(gate_ref[...])

    @pl.when(b == 0)
    def _():
        acc_ref[...] = sig * jnp.dot(oa_ref[...], w_ref[0], preferred_element_type=F32)

    @pl.when(b == 1)
    def _():
        acc_ref[...] += sig * jnp.dot(ob_ref[...], w_ref[0], preferred_element_type=F32)

    @pl.when(b == 2)
    def _():
        o_ref[...] = (acc_ref[...] + sig * jnp.dot(oc_ref[...], w_ref[0],
                                                   preferred_element_type=F32)).astype(o_ref.dtype)


def _merge(o_a, o_b, o_c, w_br, proj, tm, tn):
    m = o_a.shape[0]
    gate_blk0 = COL_GATE // tn
    per_branch = D_MODEL // tn
    o_spec = pl.BlockSpec((tm, BRANCH_W), lambda i, j, b: (i, 0))
    return pl.pallas_call(
        _merge_kernel, grid=(m // tm, D_MODEL // tn, 3),
        in_specs=[o_spec, o_spec, o_spec,
                  pl.BlockSpec((1, BRANCH_W, tn), lambda i, j, b: (b, 0, j)),
                  pl.BlockSpec((tm, tn), lambda i, j, b: (i, gate_blk0 + b * per_branch + j))],
        out_specs=pl.BlockSpec((tm, tn), lambda i, j, b: (i, j)),
        out_shape=jax.ShapeDtypeStruct((m, D_MODEL), BF16),
        scratch_shapes=[pltpu.VMEM((tm, tn), F32)],
        compiler_params=_cparams(("parallel", "parallel", "arbitrary")))(o_a, o_b, o_c, w_br, proj)


def _rms(x, g):
    return x * lax.rsqrt(jnp.mean(x * x, axis=-1, keepdims=True) + EPS) * g


def _two_source_specs(tr, nb_first):
    first = pl.BlockSpec((tr, D_MODEL), lambda i: (jnp.minimum(i, nb_first - 1), 0))
    second = pl.BlockSpec((tr, D_MODEL), lambda i: (jnp.maximum(i - nb_first, 0), 0))
    return first, second


def _prenorm_kernel(xa_ref, xb_ref, g_ref, h_ref, *, nb_first):
    i = pl.program_id(0)

    @pl.when(i < nb_first)
    def _():
        h_ref[...] = _rms(xa_ref[...], g_ref[...]).astype(h_ref.dtype)

    @pl.when(i >= nb_first)
    def _():
        h_ref[...] = _rms(xb_ref[...], g_ref[...]).astype(h_ref.dtype)


def _prenorm(xa, xb, g, tr):
    ma, mb = xa.shape[0], xb.shape[0]
    sa, sb = _two_source_specs(tr, ma // tr)
    return pl.pallas_call(
        functools.partial(_prenorm_kernel, nb_first=ma // tr), grid=((ma + mb) // tr,),
        in_specs=[sa, sb, pl.BlockSpec((1, D_MODEL), lambda i: (0, 0))],
        out_specs=pl.BlockSpec((tr, D_MODEL), lambda i: (i, 0)),
        out_shape=jax.ShapeDtypeStruct((ma + mb, D_MODEL), BF16),
        compiler_params=_cparams(("arbitrary",)))(xa, xb, g.reshape(1, D_MODEL))


def _post_kernel(*refs, nb_first, n_src, want_h):
    x_refs = refs[:n_src]
    y_ref, gp_ref, gn_ref, xo_ref = refs[n_src:n_src + 4]
    r = _rms(y_ref[...], gp_ref[...])

    def finish(x_ref):
        xn = x_ref[...] + r
        xo_ref[...] = xn
        if want_h:
            refs[n_src + 4][...] = _rms(xn, gn_ref[...]).astype(BF16)

    if n_src == 1:
        finish(x_refs[0])
    else:
        i = pl.program_id(0)
        pl.when(i < nb_first)(lambda: finish(x_refs[0]))
        pl.when(i >= nb_first)(lambda: finish(x_refs[1]))


def _post(xs, y, g_post, g_next, tr):
    m = y.shape[0]
    want_h = g_next is not None
    row = pl.BlockSpec((tr, D_MODEL), lambda i: (i, 0))
    vec = pl.BlockSpec((1, D_MODEL), lambda i: (0, 0))
    nb_first = xs[0].shape[0] // tr
    x_specs = list(_two_source_specs(tr, nb_first)) if len(xs) == 2 else [row]
    out_shape = [jax.ShapeDtypeStruct((m, D_MODEL), F32)]
    out_specs = [row]
    if want_h:
        out_shape.append(jax.ShapeDtypeStruct((m, D_MODEL), BF16))
        out_specs.append(row)
    gn = (g_next if want_h else g_post).reshape(1, D_MODEL)
    res = pl.pallas_call(
        functools.partial(_post_kernel, nb_first=nb_first, n_src=len(xs), want_h=want_h), grid=(m // tr,),
        in_specs=x_specs + [row, vec, vec], out_specs=out_specs, out_shape=out_shape,
        compiler_params=_cparams(("arbitrary",)))(*xs, y, g_post.reshape(1, D_MODEL), gn)
    return (res[0], res[1]) if want_h else (res[0], None)


def _post_split_kernel(x_ref, y_ref, gp_ref, oa_ref, ob_ref, *, nb_first):
    i = pl.program_id(0)
    xn = x_ref[...] + _rms(y_ref[...], gp_ref[...])

    @pl.when(i < nb_first)
    def _():
        oa_ref[...] = xn

    @pl.when(i >= nb_first)
    def _():
        ob_ref[...] = xn


def _post_split(x, y, g_post, rows_first, tr):
    m = x.shape[0]
    row = pl.BlockSpec((tr, D_MODEL), lambda i: (i, 0))
    vec = pl.BlockSpec((1, D_MODEL), lambda i: (0, 0))
    oa, ob = _two_source_specs(tr, rows_first // tr)
    return pl.pallas_call(
        functools.partial(_post_split_kernel, nb_first=rows_first // tr), grid=(m // tr,),
        in_specs=[row, row, vec], out_specs=[oa, ob],
        out_shape=[jax.ShapeDtypeStruct((rows_first, D_MODEL), F32),
                   jax.ShapeDtypeStruct((m - rows_first, D_MODEL), F32)],
        compiler_params=_cparams(("arbitrary",)))(x, y, g_post.reshape(1, D_MODEL))


def _gdn_prep_kernel(a_ref, b_ref, alog_ref, dt_ref, gc_ref, beta_ref, *, prompt_blocks):
    i = pl.program_id(0)
    x = a_ref[...] + dt_ref[...]
    softplus = jnp.maximum(x, 0.0) + jnp.log1p(jnp.exp(-jnp.abs(x)))
    g = -jnp.exp(alog_ref[...]) * softplus
    chunk = jnp.where(i < prompt_blocks, MIX_CHUNK, 8)
    rmod = lax.broadcasted_iota(jnp.int32, g.shape, 0) & (chunk - 1)
    s = 1
    while s < MIX_CHUNK:
        g = g + jnp.where(rmod >= s, pltpu.roll(g, s, axis=0), 0.0)
        s *= 2
    gc_ref[...] = g
    beta_ref[...] = jax.nn.sigmoid(b_ref[...])


def _gdn_prep(ab, a_log, dt_bias, tr, prompt_rows):
    m = ab.shape[0]
    pad = lambda v: jnp.pad(v.astype(F32), (0, 128 - H_A)).reshape(1, 128)
    blk = lambda c: pl.BlockSpec((tr, 128), lambda i: (i, c))
    vec = pl.BlockSpec((1, 128), lambda i: (0, 0))
    return pl.pallas_call(
        functools.partial(_gdn_prep_kernel, prompt_blocks=prompt_rows // tr), grid=(m // tr,),
        in_specs=[blk(0), blk(1), vec, vec], out_specs=[blk(0), blk(0)],
        out_shape=[jax.ShapeDtypeStruct((m, 128), F32)] * 2,
        compiler_params=_cparams(("parallel",)))(ab, ab, pad(a_log), pad(dt_bias))


def _gdn_kernel(x_ref, z_ref, gcol_ref, bcol_ref, grow_ref, s0_ref, buf_ref, cw_ref, ng_ref,
                o_ref, s_ref, tail_ref, xc_ref, *, c, group):
    @pl.when(pl.program_id(1) == 0)
    def _():
        s_ref[...] = s0_ref[...]
        xc_ref[:8, :] = buf_ref[0]

    xc_ref[8:, :] = x_ref[...]
    tail_ref[0] = x_ref[c - 8:, :]

    def conv(lo):
        acc = xc_ref[pl.ds(8 - (CONV_W - 1), c), lo:lo + 128] * cw_ref[0:1, lo:lo + 128]
        for j in range(1, CONV_W):
            acc = acc + xc_ref[pl.ds(8 - (CONV_W - 1) + j, c), lo:lo + 128] * cw_ref[j:j + 1, lo:lo + 128]
        return jax.nn.silu(acc)

    def l2n(t):
        return t * lax.rsqrt(jnp.sum(t * t, axis=-1, keepdims=True) + EPS)

    ii = lax.broadcasted_iota(jnp.int32, (c, c), 0)
    jj = lax.broadcasted_iota(jnp.int32, (c, c), 1)
    tri = ii >= jj
    strict = ii > jj
    ng = ng_ref[...]
    hk = H_A * DK_A
    for h0 in range(0, H_A, group):
        heads = range(h0, h0 + group)
        st = []
        for h in heads:
            lo = h * DK_A
            q = l2n(conv(lo)) * (DK_A ** -0.5)
            k = l2n(conv(hk + lo))
            v = conv(2 * hk + lo)
            gc = gcol_ref[:, h:h + 1]
            beta = bcol_ref[:, h:h + 1]
            gr = grow_ref[0, 0, h:h + 1, :]
            glast = gr[:, c - 1:c]
            decay = jnp.exp(jnp.where(tri, gc - gr, -jnp.inf))
            egc = jnp.exp(gc)
            kb = k * beta
            kbf = k.astype(BF16)
            m = lax.dot_general(jnp.concatenate([kb, q], axis=0).astype(BF16), kbf,
                                (((1,), (1,)), ((), ())), preferred_element_type=F32)
            s = s_ref[0, h]
            sq = _bdot(jnp.concatenate([kb * egc, q * egc], axis=0), s)
            st.append(dict(lo=lo, m=m, sq=sq, s=s, decay=decay, vb=v * beta,
                           kd=(k * jnp.exp(glast - gc)).astype(BF16), cd=jnp.exp(glast)))
        for d in st:
            d["p"] = (-jnp.where(strict, d["m"][:c] * d["decay"], 0.0)).astype(BF16)
            d["attn"] = (d["m"][c:] * d["decay"]).astype(BF16)
            d["x"] = d["vb"] - d["sq"][:c]
        n_sq = 1
        while n_sq < c:
            last = 2 * n_sq >= c
            for d in st:
                d["x"] = d["x"] + jnp.dot(d["p"], d["x"].astype(BF16), preferred_element_type=F32)
            if not last:
                for d in st:
                    d["p"] = jnp.dot(d["p"], d["p"], preferred_element_type=F32).astype(BF16)
            n_sq *= 2
        for h, d in zip(heads, st):
            xb = d["x"].astype(BF16)
            o = d["sq"][c:] + jnp.dot(d["attn"], xb, preferred_element_type=F32)
            s_ref[0, h] = d["s"] * d["cd"] + lax.dot_general(
                d["kd"], xb, (((0,), (0,)), ((), ())), preferred_element_type=F32)
            lo = d["lo"]
            o = o * lax.rsqrt(jnp.mean(o * o, axis=-1, keepdims=True) + EPS) * ng
            o = o * jax.nn.silu(z_ref[:, lo:lo + DV_A])
            o_ref[:, lo:lo + DV_A] = o.astype(o_ref.dtype)
    xc_ref[:8, :] = xc_ref[c:c + 8, :]


def _gdn(proj, gc, beta, grow, s0, layer, buf8, conv_w, norm_g, row_off, n, l, c, group, out_dtype):
    nc = l // c
    base = row_off // c
    rows = lambda w, col: pl.BlockSpec((c, w), lambda i, j: (base + i * nc + j, col))
    return pl.pallas_call(
        functools.partial(_gdn_kernel, c=c, group=group), grid=(n, nc),
        in_specs=[rows(CONV_DIM, 0), rows(BRANCH_W, COL_Z // BRANCH_W), rows(128, 0), rows(128, 0),
                  pl.BlockSpec((1, 1, H_A, c), lambda i, j: (i, j, 0, 0)),
                  pl.BlockSpec((None, 1, H_A, DK_A, DV_A), lambda i, j: (layer, i, 0, 0, 0)),
                  pl.BlockSpec((1, 8, CONV_DIM), lambda i, j: (i, 0, 0)),
                  pl.BlockSpec((CONV_W, CONV_DIM), lambda i, j: (0, 0)),
                  pl.BlockSpec((1, DV_A), lambda i, j: (0, 0))],
        out_specs=[pl.BlockSpec((c, BRANCH_W), lambda i, j: (i * nc + j, 0)),
                   pl.BlockSpec((1, H_A, DK_A, DV_A), lambda i, j: (i, 0, 0, 0)),
                   pl.BlockSpec((1, 8, CONV_DIM), lambda i, j: (i, 0, 0))],
        out_shape=[jax.ShapeDtypeStruct((n * l, BRANCH_W), out_dtype),
                   jax.ShapeDtypeStruct((n, H_A, DK_A, DV_A), F32),
                   jax.ShapeDtypeStruct((n, 8, CONV_DIM), F32)],
        scratch_shapes=[pltpu.VMEM((c + 8, CONV_DIM), F32)],
        compiler_params=_cparams(("parallel", "arbitrary")))(
            proj, proj, gc, beta, grow, s0, buf8, conv_w, norm_g.reshape(1, DV_A))


def _ret_kernel(qk_ref, v_ref, gate_ref, cos_ref, sin_ref, r0_ref, o_ref, r_ref, *, c):
    @pl.when(pl.program_id(1) == 0)
    def _():
        r_ref[...] = r0_ref[...]

    ii = lax.broadcasted_iota(jnp.int32, (c, c), 0)
    jj = lax.broadcasted_iota(jnp.int32, (c, c), 1)
    rel = (ii - jj).astype(F32)
    idx = lax.broadcasted_iota(jnp.int32, (c, 1), 0).astype(F32)
    even = (lax.broadcasted_iota(jnp.int32, (c, DK_B), 1) & 1) == 0
    cos = cos_ref[...]
    sin = sin_ref[...]

    def rot(x):
        swapped = jnp.where(even, pltpu.roll(x, DK_B - 1, axis=1), pltpu.roll(x, 1, axis=1))
        return x * cos + swapped * sin

    lgs = [math.log1p(-2.0 ** (-5.0 - h)) for h in range(H_B)]
    qs = [rot(qk_ref[:, h * DK_B:(h + 1) * DK_B]) for h in range(H_B)]
    ks = [rot(qk_ref[:, H_B * DK_B + h * DK_B:H_B * DK_B + (h + 1) * DK_B]) * (DK_B ** -0.5)
          for h in range(H_B)]
    vs = [v_ref[:, h * DV_B:(h + 1) * DV_B].astype(BF16) for h in range(H_B)]
    rs = [r_ref[0, h] for h in range(H_B)]
    scores = [_bdot_nt(qs[h], ks[h]) for h in range(H_B)]
    cross = [_bdot(qs[h] * jnp.exp(lgs[h] * (idx + 1.0)), rs[h]) for h in range(H_B)]
    for h in range(H_B):
        r_ref[0, h] = rs[h] * math.exp(lgs[h] * c) + _bdot_tn(
            ks[h] * jnp.exp(lgs[h] * (c - 1.0 - idx)), vs[h])
    inners = [_bdot(scores[h] * jnp.where(rel >= 0, jnp.exp(lgs[h] * jnp.maximum(rel, 0.0)), 0.0), vs[h])
              for h in range(H_B)]
    for h in range(H_B):
        o = inners[h] + cross[h]
        mu = jnp.mean(o, axis=-1, keepdims=True)
        d = o - mu
        var = jnp.mean(d * d, axis=-1, keepdims=True)
        o = d * lax.rsqrt(var + EPS)
        o = jax.nn.silu(gate_ref[:, h * DV_B:(h + 1) * DV_B]) * o
        o_ref[:, h * DV_B:(h + 1) * DV_B] = o.astype(o_ref.dtype)


def _rope_tables(pos):
    half = DK_B // 2
    inv_freq = 1.0 / (ROPE_BASE ** jnp.linspace(0.0, 1.0, half, dtype=F32))
    ang = pos.astype(F32)[:, None] * inv_freq[None, :]
    sin = jnp.sin(ang)
    cos = jnp.cos(ang)
    cos2 = jnp.stack([cos, cos], axis=-1).reshape(-1, DK_B)
    sin2 = jnp.stack([-sin, sin], axis=-1).reshape(-1, DK_B)
    return cos2, sin2


def _retention(proj, cos2, sin2, r0, layer, row_off, n, l, c, out_dtype):
    nc = l // c
    base = row_off // c
    col = lambda off: pl.BlockSpec((c, BRANCH_W), lambda i, j: (base + i * nc + j, off // BRANCH_W))
    tab = pl.BlockSpec((c, DK_B), lambda i, j: (j, 0))
    return pl.pallas_call(
        functools.partial(_ret_kernel, c=c), grid=(n, nc),
        in_specs=[col(COL_QKB), col(COL_VB), col(COL_GB), tab, tab,
                  pl.BlockSpec((None, 1, H_B, DK_B, DV_B), lambda i, j: (layer, i, 0, 0, 0))],
        out_specs=[pl.BlockSpec((c, BRANCH_W), lambda i, j: (i * nc + j, 0)),
                   pl.BlockSpec((1, H_B, DK_B, DV_B), lambda i, j: (i, 0, 0, 0))],
        out_shape=[jax.ShapeDtypeStruct((n * l, BRANCH_W), out_dtype),
                   jax.ShapeDtypeStruct((n, H_B, DK_B, DV_B), F32)],
        compiler_params=_cparams(("parallel", "arbitrary")))(proj, proj, proj, cos2, sin2, r0)


def _gelu(x):
    return 0.5 * x * (1.0 + lax.erf(x * (2.0 ** -0.5)))


def _gmlp_kernel(u_ref, v_ref, w_ref, bcol_ref, lng_ref, lnb_ref, o_ref, *vrows_ref, seq):
    u32 = _gelu(u_ref[...])
    v32 = _gelu(v_ref[...])
    mu = jnp.mean(v32, axis=-1, keepdims=True)
    d = v32 - mu
    var = jnp.mean(d * d, axis=-1, keepdims=True)
    vn = d * lax.rsqrt(var + EPS) * lng_ref[...] + lnb_ref[...]
    if vrows_ref:
        vrows_ref[0][...] = vn
    ii = lax.broadcasted_iota(jnp.int32, (CHUNK_C, CHUNK_C), 0)
    jj = lax.broadcasted_iota(jnp.int32, (CHUNK_C, CHUNK_C), 1)
    mask = (ii >= jj) & ((ii & -seq) == (jj & -seq))
    for g in range(G_C):
        w = jnp.where(mask, w_ref[g], 0.0)
        mixed = _bdot(w, vn[:, g * DG_C:(g + 1) * DG_C]) + bcol_ref[:, g:g + 1]
        o_ref[:, g * DG_C:(g + 1) * DG_C] = (u32[:, g * DG_C:(g + 1) * DG_C] * mixed).astype(o_ref.dtype)


def _gmlp(proj, w_tile, b_col, ln_g, ln_b, row_off, rows, seq, want_vrows):
    base = row_off // CHUNK_C
    col = lambda off: pl.BlockSpec((CHUNK_C, BRANCH_W), lambda i: (base + i, off // BRANCH_W))
    row = pl.BlockSpec((CHUNK_C, BRANCH_W), lambda i: (i, 0))
    vec = pl.BlockSpec((1, BRANCH_W), lambda i: (0, 0))
    out_shape = [jax.ShapeDtypeStruct((rows, BRANCH_W), BF16)]
    out_specs = [row]
    if want_vrows:
        out_shape.append(jax.ShapeDtypeStruct((rows, BRANCH_W), F32))
        out_specs.append(row)
    res = pl.pallas_call(
        functools.partial(_gmlp_kernel, seq=seq), grid=(rows // CHUNK_C,),
        in_specs=[col(COL_U), col(COL_V),
                  pl.BlockSpec((G_C, CHUNK_C, CHUNK_C), lambda i: (0, 0, 0)),
                  pl.BlockSpec((CHUNK_C, G_C), lambda i: (0, 0)), vec, vec],
        out_specs=out_specs, out_shape=out_shape,
        compiler_params=_cparams(("parallel",)))(
            proj, proj, w_tile, b_col, ln_g.reshape(1, BRANCH_W), ln_b.reshape(1, BRANCH_W))
    return res if want_vrows else (res[0], None)


AB_COL0 = 4 * BRANCH_W
AB_COLS = 2 * H_A


def _pack_kernel(main_ref, next_ref, o_ref, *, tn, first_shifted):
    j = pl.program_id(0)

    @pl.when(j < first_shifted)
    def _():
        o_ref[...] = main_ref[...].astype(BF16)

    @pl.when(j >= first_shifted)
    def _():
        lane = lax.broadcasted_iota(jnp.int32, (main_ref.shape[0], 128), 1)
        keep = lane < 128 - AB_COLS
        tiles = [main_ref[:, t * 128:(t + 1) * 128] for t in range(tn // 128)] + [next_ref[...]]
        rolled = [pltpu.roll(t, 128 - AB_COLS, axis=1) for t in tiles]
        for t in range(tn // 128):
            o_ref[:, t * 128:(t + 1) * 128] = jnp.where(keep, rolled[t], rolled[t + 1]).astype(BF16)


def _pack_w_in(w_in, layer, tr, tn):
    k = w_in.shape[1]
    return pl.pallas_call(
        functools.partial(_pack_kernel, tn=tn, first_shifted=AB_COL0 // tn), grid=(N_PROJ // tn, k // tr),
        in_specs=[pl.BlockSpec((None, tr, tn), lambda j, r: (layer, r, j)),
                  pl.BlockSpec((None, tr, 128), lambda j, r: (layer, r, (j + 1) * (tn // 128)))],
        out_specs=pl.BlockSpec((tr, tn), lambda j, r: (r, j)),
        out_shape=jax.ShapeDtypeStruct((k, N_PROJ), BF16),
        compiler_params=_cparams(("parallel", "parallel")))(w_in, w_in)


def _cast_rows_kernel(w_ref, o_ref, *, tr, rows_valid):
    row = pl.program_id(0) * tr + lax.broadcasted_iota(jnp.int32, w_ref.shape, 0)
    o_ref[...] = jnp.where(row < rows_valid, w_ref[...], 0.0).astype(BF16)


def _cast_rows(w, layer, rows_out, tr):
    rows, cols = w.shape[1:]
    return pl.pallas_call(
        functools.partial(_cast_rows_kernel, tr=tr, rows_valid=rows), grid=(rows_out // tr,),
        in_specs=[pl.BlockSpec((None, tr, cols), lambda r: (layer, r, 0))],
        out_specs=pl.BlockSpec((tr, cols), lambda r: (r, 0)),
        out_shape=jax.ShapeDtypeStruct((rows_out, cols), BF16),
        compiler_params=_cparams(("parallel",)))(w)


def _ab_weights(w_in, layer):
    a = jnp.pad(w_in[layer, :, AB_COL0:AB_COL0 + H_A], ((0, 0), (0, 128 - H_A)))
    b = jnp.pad(w_in[layer, :, AB_COL0 + H_A:AB_COL0 + AB_COLS], ((0, 0), (0, 128 - H_A)))
    return jnp.concatenate([a, b], axis=1).astype(BF16)


def kernel(x_prompt, x_sample, state_gdn, state_conv, state_ret, w_in, conv_w, a_log, dt_bias, gdn_norm, w_s, b_s, ln_c_g, ln_c_b, w_br, w_o, g_pre_mix, g_post_mix, g_pre_ffn, g_post_ffn, w_ffn_gate, w_ffn_up, w_ffn_down):
    n_p, l_p, _ = x_prompt.shape
    n_s, l_s, _ = x_sample.shape
    depth = w_in.shape[0]
    rows_p, rows_s = n_p * l_p, n_s * l_s
    xs = (x_prompt.reshape(rows_p, D_MODEL), x_sample.reshape(rows_s, D_MODEL))
    tr = 256

    cos_p, sin_p = _rope_tables(jnp.arange(l_p))
    cos_s, sin_s = _rope_tables(PAST_LEN + jnp.arange(l_s))
    zero_gdn = jnp.zeros((1, n_p, H_A, DK_A, DV_A), F32)
    zero_ret = jnp.zeros((1, n_p, H_B, DK_B, DV_B), F32)
    zero_buf8 = jnp.zeros((n_p, 8, CONV_DIM), F32)
    reps = CHUNK_C // l_s

    outs = {k: [] for k in ("gdn_p", "gdn_s", "conv_p", "conv_s", "ret_p", "ret_s", "vrows")}
    h = _prenorm(xs[0], xs[1], g_pre_mix[0], tr)
    for l in range(depth):
        proj = _matmul(h, _pack_w_in(w_in, l, 2048, 512), F32, 1024, 1024)
        ab = _matmul(h, _ab_weights(w_in, l), F32, 1024, 256)

        gc, beta = _gdn_prep(ab, a_log[l], dt_bias[l], 1024, rows_p)
        grow_p = gc[:rows_p, :H_A].reshape(n_p, l_p // MIX_CHUNK, MIX_CHUNK, H_A).transpose(0, 1, 3, 2)
        grow_s = gc[rows_p:, :H_A].reshape(n_s, 1, l_s, H_A).transpose(0, 1, 3, 2)
        buf8_s = jnp.pad(state_conv[l], ((0, 0), (8 - (CONV_W - 1), 0), (0, 0)))
        oa_p, sg_p, tail_p = _gdn(proj, gc, beta, grow_p, zero_gdn, 0, zero_buf8, conv_w[l], gdn_norm[l],
                                  0, n_p, l_p, MIX_CHUNK, GDN_GROUP, BF16)
        oa_s, sg_s, tail_s = _gdn(proj, gc, beta, grow_s, state_gdn, l, buf8_s, conv_w[l], gdn_norm[l],
                                  rows_p, n_s, l_s, l_s, GDN_GROUP, F32)
        outs["conv_p"].append(tail_p[:, 8 - (CONV_W - 1):])
        outs["conv_s"].append(tail_s[:, 8 - (CONV_W - 1):])
        outs["gdn_p"].append(sg_p)
        outs["gdn_s"].append(sg_s)

        ob_p, sr_p = _retention(proj, cos_p, sin_p, zero_ret, 0, 0, n_p, l_p, RET_CHUNK, BF16)
        ob_s, sr_s = _retention(proj, cos_s, sin_s, state_ret, l, rows_p, n_s, l_s, l_s, F32)
        outs["ret_p"].append(sr_p)
        outs["ret_s"].append(sr_s)

        oc_p, _ = _gmlp(proj, w_s[l], b_s[l].T, ln_c_g[l], ln_c_b[l], 0, rows_p, CHUNK_C, False)
        w_tile_s = jnp.tile(w_s[l][:, :l_s, :l_s], (1, reps, reps))
        b_col_s = jnp.tile(b_s[l][:, :l_s].T, (reps, 1))
        oc_s, vrows = _gmlp(proj, w_tile_s, b_col_s, ln_c_g[l], ln_c_b[l], rows_p, rows_s, l_s, True)
        outs["vrows"].append(vrows.reshape(n_s, l_s, BRANCH_W))

        o_a = jnp.concatenate([oa_p, oa_s.astype(BF16)], axis=0)
        o_b = jnp.concatenate([ob_p, ob_s.astype(BF16)], axis=0)
        o_c = jnp.concatenate([oc_p, oc_s], axis=0)
        w_br_l = _cast_rows(w_br.reshape(depth, 3 * BRANCH_W, D_MODEL), l, 3 * BRANCH_W, 512)
        merged = _merge(o_a, o_b, o_c, w_br_l.reshape(3, BRANCH_W, D_MODEL), proj, 1024, 1024)
        mix = _matmul_ws(merged, w_o, l, F32, 1024, 512)
        x, h = _post(xs, mix, g_post_mix[l], g_pre_ffn[l], tr)

        wd = _cast_rows(w_ffn_down, l, D_FF_PAD, 512)
        f1 = _ffn_up(h, w_ffn_gate, w_ffn_up, l, 512, 512, D_FF_PAD)
        f = _matmul_kgrid(f1, wd, F32, 1024, 1024, D_FF_PAD // 4)
        if l + 1 < depth:
            x, h = _post((x,), f, g_post_ffn[l], g_pre_mix[l + 1], tr)
            xs = (x,)
        else:
            y_p, y_s = _post_split(x, f, g_post_ffn[l], rows_p, tr)

    y_p = y_p.reshape(n_p, l_p, D_MODEL)
    y_s = y_s.reshape(n_s, l_s, D_MODEL)
    st = lambda k: jnp.stack(outs[k])
    return (y_p, y_s, st("gdn_p"), st("gdn_s"), st("conv_p"), st("conv_s"), st("ret_p"), st("ret_s"),
            st("vrows"))
```

```python
import functools
import math

import jax
import jax.numpy as jnp
from jax import lax
from jax.experimental import pallas as pl
from jax.experimental.pallas import tpu as pltpu

F32 = jnp.float32
BF16 = jnp.bfloat16

D_MODEL = 4096
BRANCH_W = 2048
H_A, DK_A, DV_A = 16, 128, 128
CONV_W = 4
CONV_DIM = 6144
H_B, DK_B, DV_B = 8, 128, 256
G_C, DG_C, CHUNK_C = 8, 256, 128
D_FF = 11008
D_FF_PAD = 11264
EPS = 1e-6
ROPE_BASE = 10000.0
PAST_LEN = 16384
MIX_CHUNK = 64
RET_CHUNK = 128
GDN_GROUP = 8

COL_QKV, COL_Z, COL_QKB, COL_VB, COL_GB, COL_U, COL_V, COL_GATE = (
    0, 6144, 8192, 10240, 12288, 14336, 16384, 18432)
N_PROJ = 30720
VMEM_LIMIT = 56 * 1024 * 1024


def _cparams(sem):
    return pltpu.CompilerParams(dimension_semantics=sem, vmem_limit_bytes=VMEM_LIMIT)


def _bdot(a, b):
    return jnp.dot(a.astype(BF16), b.astype(BF16), preferred_element_type=F32)


def _bdot_nt(a, b):
    return lax.dot_general(a.astype(BF16), b.astype(BF16), (((1,), (1,)), ((), ())),
                           preferred_element_type=F32)


def _bdot_tn(a, b):
    return lax.dot_general(a.astype(BF16), b.astype(BF16), (((0,), (0,)), ((), ())),
                           preferred_element_type=F32)


def _mm_nt_kernel(a_ref, bt_ref, o_ref):
    o_ref[...] = lax.dot_general(a_ref[...], bt_ref[...].astype(BF16), (((1,), (1,)), ((), ())),
                                 preferred_element_type=F32).astype(o_ref.dtype)


def _matmul_nt(a, bt, out_dtype, tm, tn):
    m, k = a.shape
    n = bt.shape[0]
    return pl.pallas_call(
        _mm_nt_kernel, grid=(m // tm, n // tn),
        in_specs=[pl.BlockSpec((tm, k), lambda i, j: (i, 0)),
                  pl.BlockSpec((tn, k), lambda i, j: (j, 0))],
        out_specs=pl.BlockSpec((tm, tn), lambda i, j: (i, j)),
        out_shape=jax.ShapeDtypeStruct((m, n), out_dtype),
        compiler_params=_cparams(("parallel", "parallel")))(a, bt)


def _mm_ws_kernel(a_ref, b_ref, o_ref, bw_ref):
    @pl.when(pl.program_id(1) == 0)
    def _():
        bw_ref[...] = b_ref[...].astype(BF16)

    o_ref[...] = jnp.dot(a_ref[...], bw_ref[...], preferred_element_type=F32).astype(o_ref.dtype)


def _matmul_ws(a, b, layer, out_dtype, tm, tn):
    m, k = a.shape
    n = b.shape[2]
    return pl.pallas_call(
        _mm_ws_kernel, grid=(n // tn, m // tm),
        in_specs=[pl.BlockSpec((tm, k), lambda j, i: (i, 0)),
                  pl.BlockSpec((None, k, tn), lambda j, i: (layer, 0, j))],
        out_specs=pl.BlockSpec((tm, tn), lambda j, i: (i, j)),
        out_shape=jax.ShapeDtypeStruct((m, n), out_dtype),
        scratch_shapes=[pltpu.VMEM((k, tn), BF16)],
        compiler_params=_cparams(("arbitrary", "arbitrary")))(a, b)


def _mm_acc_kernel(a_ref, b_ref, o_ref, acc_ref):
    k = pl.program_id(2)

    @pl.when(k == 0)
    def _():
        acc_ref[...] = jnp.zeros_like(acc_ref)

    acc_ref[...] += jnp.dot(a_ref[...], b_ref[...], preferred_element_type=F32)

    @pl.when(k == pl.num_programs(2) - 1)
    def _():
        o_ref[...] = acc_ref[...].astype(o_ref.dtype)


def _matmul_kgrid(a, b, out_dtype, tm, tn, tk):
    m, k = a.shape
    n = b.shape[1]
    return pl.pallas_call(
        _mm_acc_kernel, grid=(m // tm, n // tn, k // tk),
        in_specs=[pl.BlockSpec((tm, tk), lambda i, j, kk: (i, kk)),
                  pl.BlockSpec((tk, tn), lambda i, j, kk: (kk, j))],
        out_specs=pl.BlockSpec((tm, tn), lambda i, j, kk: (i, j)),
        out_shape=jax.ShapeDtypeStruct((m, n), out_dtype),
        scratch_shapes=[pltpu.VMEM((tm, tn), F32)],
        compiler_params=_cparams(("parallel", "parallel", "arbitrary")))(a, b)


def _ffn_up_kernel(h_ref, wg_ref, wu_ref, o_ref, wgb_ref, wub_ref, *, tn, n_valid):
    j = pl.program_id(0)

    @pl.when(pl.program_id(1) == 0)
    def _():
        wgb_ref[...] = wg_ref[...].astype(BF16)
        wub_ref[...] = wu_ref[...].astype(BF16)

    h = h_ref[...]
    g = jnp.dot(h, wgb_ref[...], preferred_element_type=F32)
    u = jnp.dot(h, wub_ref[...], preferred_element_type=F32)
    col = j * tn + lax.broadcasted_iota(jnp.int32, g.shape, 1)
    o_ref[...] = jnp.where(col < n_valid, jax.nn.silu(g) * u, 0.0).astype(o_ref.dtype)


def _ffn_up(h, wg, wu, layer, tm, tn, n_out):
    m, k = h.shape
    n = wg.shape[2]
    wspec = pl.BlockSpec((None, k, tn), lambda j, i: (layer, 0, j))
    return pl.pallas_call(
        functools.partial(_ffn_up_kernel, tn=tn, n_valid=n), grid=(n_out // tn, m // tm),
        in_specs=[pl.BlockSpec((tm, k), lambda j, i: (i, 0)), wspec, wspec],
        out_specs=pl.BlockSpec((tm, tn), lambda j, i: (i, j)),
        out_shape=jax.ShapeDtypeStruct((m, n_out), BF16),
        scratch_shapes=[pltpu.VMEM((k, tn), BF16)] * 2,
        compiler_params=_cparams(("arbitrary", "arbitrary")))(h, wg, wu)


def _merge_kernel(oa_ref, ob_ref, oc_ref, w_ref, gate_ref, o_ref, acc_ref):
    b = pl.program_id(2)
    sig = jax.nn.sigmoid(gate_ref[...])

    @pl.when(b == 0)
    def _():
        acc_ref[...] = sig * jnp.dot(oa_ref[...], w_ref[0], preferred_element_type=F32)

    @pl.when(b == 1)
    def _():
        acc_ref[...] += sig * jnp.dot(ob_ref[...], w_ref[0], preferred_element_type=F32)

    @pl.when(b == 2)
    def _():
        o_ref[...] = (acc_ref[...] + sig * jnp.dot(oc_ref[...], w_ref[0],
                                                   preferred_element_type=F32)).astype(o_ref.dtype)


def _merge(o_a, o_b, o_c, w_br, proj, tm, tn):
    m = o_a.shape[0]
    gate_blk0 = COL_GATE // tn
    per_branch = D_MODEL // tn
    o_spec = pl.BlockSpec((tm, BRANCH_W), lambda i, j, b: (i, 0))
    return pl.pallas_call(
        _merge_kernel, grid=(m // tm, D_MODEL // tn, 3),
        in_specs=[o_spec, o_spec, o_spec,
                  pl.BlockSpec((1, BRANCH_W, tn), lambda i, j, b: (b, 0, j)),
                  pl.BlockSpec((tm, tn), lambda i, j, b: (i, gate_blk0 + b * per_branch + j))],
        out_specs=pl.BlockSpec((tm, tn), lambda i, j, b: (i, j)),
        out_shape=jax.ShapeDtypeStruct((m, D_MODEL), BF16),
        scratch_shapes=[pltpu.VMEM((tm, tn), F32)],
        compiler_params=_cparams(("parallel", "parallel", "arbitrary")))(o_a, o_b, o_c, w_br, proj)


def _rms(x, g):
    return x * lax.rsqrt(jnp.mean(x * x, axis=-1, keepdims=True) + EPS) * g


def _two_source_specs(tr, nb_first):
    first = pl.BlockSpec((tr, D_MODEL), lambda i: (jnp.minimum(i, nb_first - 1), 0))
    second = pl.BlockSpec((tr, D_MODEL), lambda i: (jnp.maximum(i - nb_first, 0), 0))
    return first, second


def _prenorm_kernel(xa_ref, xb_ref, g_ref, h_ref, *, nb_first):
    i = pl.program_id(0)

    @pl.when(i < nb_first)
    def _():
        h_ref[...] = _rms(xa_ref[...], g_ref[...]).astype(h_ref.dtype)

    @pl.when(i >= nb_first)
    def _():
        h_ref[...] = _rms(xb_ref[...], g_ref[...]).astype(h_ref.dtype)


def _prenorm(xa, xb, g, tr):
    ma, mb = xa.shape[0], xb.shape[0]
    sa, sb = _two_source_specs(tr, ma // tr)
    return pl.pallas_call(
        functools.partial(_prenorm_kernel, nb_first=ma // tr), grid=((ma + mb) // tr,),
        in_specs=[sa, sb, pl.BlockSpec((1, D_MODEL), lambda i: (0, 0))],
        out_specs=pl.BlockSpec((tr, D_MODEL), lambda i: (i, 0)),
        out_shape=jax.ShapeDtypeStruct((ma + mb, D_MODEL), BF16),
        compiler_params=_cparams(("arbitrary",)))(xa, xb, g.reshape(1, D_MODEL))


def _post_kernel(*refs, nb_first, n_src, want_h):
    x_refs = refs[:n_src]
    y_ref, gp_ref, gn_ref, xo_ref = refs[n_src:n_src + 4]
    r = _rms(y_ref[...], gp_ref[...])

    def finish(x_ref):
        xn = x_ref[...] + r
        xo_ref[...] = xn
        if want_h:
            refs[n_src + 4][...] = _rms(xn, gn_ref[...]).astype(BF16)

    if n_src == 1:
        finish(x_refs[0])
    else:
        i = pl.program_id(0)
        pl.when(i < nb_first)(lambda: finish(x_refs[0]))
        pl.when(i >= nb_first)(lambda: finish(x_refs[1]))


def _post(xs, y, g_post, g_next, tr):
    m = y.shape[0]
    want_h = g_next is not None
    row = pl.BlockSpec((tr, D_MODEL), lambda i: (i, 0))
    vec = pl.BlockSpec((1, D_MODEL), lambda i: (0, 0))
    nb_first = xs[0].shape[0] // tr
    x_specs = list(_two_source_specs(tr, nb_first)) if len(xs) == 2 else [row]
    out_shape = [jax.ShapeDtypeStruct((m, D_MODEL), F32)]
    out_specs = [row]
    if want_h:
        out_shape.append(jax.ShapeDtypeStruct((m, D_MODEL), BF16))
        out_specs.append(row)
    gn = (g_next if want_h else g_post).reshape(1, D_MODEL)
    res = pl.pallas_call(
        functools.partial(_post_kernel, nb_first=nb_first, n_src=len(xs), want_h=want_h), grid=(m // tr,),
        in_specs=x_specs + [row, vec, vec], out_specs=out_specs, out_shape=out_shape,
        compiler_params=_cparams(("arbitrary",)))(*xs, y, g_post.reshape(1, D_MODEL), gn)
    return (res[0], res[1]) if want_h else (res[0], None)


def _post_split_kernel(x_ref, y_ref, gp_ref, oa_ref, ob_ref, *, nb_first):
    i = pl.program_id(0)
    xn = x_ref[...] + _rms(y_ref[...], gp_ref[...])

    @pl.when(i < nb_first)
    def _():
        oa_ref[...] = xn

    @pl.when(i >= nb_first)
    def _():
        ob_ref[...] = xn


def _post_split(x, y, g_post, rows_first, tr):
    m = x.shape[0]
    row = pl.BlockSpec((tr, D_MODEL), lambda i: (i, 0))
    vec = pl.BlockSpec((1, D_MODEL), lambda i: (0, 0))
    oa, ob = _two_source_specs(tr, rows_first // tr)
    return pl.pallas_call(
        functools.partial(_post_split_kernel, nb_first=rows_first // tr), grid=(m // tr,),
        in_specs=[row, row, vec], out_specs=[oa, ob],
        out_shape=[jax.ShapeDtypeStruct((rows_first, D_MODEL), F32),
                   jax.ShapeDtypeStruct((m - rows_first, D_MODEL), F32)],
        compiler_params=_cparams(("arbitrary",)))(x, y, g_post.reshape(1, D_MODEL))


def _gdn_prep_kernel(a_ref, b_ref, alog_ref, dt_ref, gc_ref, beta_ref, *, prompt_blocks):
    i = pl.program_id(0)
    x = a_ref[...] + dt_ref[...]
    softplus = jnp.maximum(x, 0.0) + jnp.log1p(jnp.exp(-jnp.abs(x)))
    g = -jnp.exp(alog_ref[...]) * softplus
    chunk = jnp.where(i < prompt_blocks, MIX_CHUNK, 8)
    rmod = lax.broadcasted_iota(jnp.int32, g.shape, 0) & (chunk - 1)
    s = 1
    while s < MIX_CHUNK:
        g = g + jnp.where(rmod >= s, pltpu.roll(g, s, axis=0), 0.0)
        s *= 2
    gc_ref[...] = g
    beta_ref[...] = jax.nn.sigmoid(b_ref[...])


def _gdn_prep(ab, a_log, dt_bias, tr, prompt_rows):
    m = ab.shape[0]
    pad = lambda v: jnp.pad(v.astype(F32), (0, 128 - H_A)).reshape(1, 128)
    blk = lambda c: pl.BlockSpec((tr, 128), lambda i: (i, c))
    vec = pl.BlockSpec((1, 128), lambda i: (0, 0))
    return pl.pallas_call(
        functools.partial(_gdn_prep_kernel, prompt_blocks=prompt_rows // tr), grid=(m // tr,),
        in_specs=[blk(0), blk(1), vec, vec], out_specs=[blk(0), blk(0)],
        out_shape=[jax.ShapeDtypeStruct((m, 128), F32)] * 2,
        compiler_params=_cparams(("parallel",)))(ab, ab, pad(a_log), pad(dt_bias))


def _gdn_kernel(x_ref, z_ref, gcol_ref, bcol_ref, grow_ref, s0_ref, buf_ref, cw_ref, ng_ref, stack_ref,
                o_ref, s_ref, tail_ref, xc_ref, *, c, group):
    @pl.when(pl.program_id(1) == 0)
    def _():
        s_ref[...] = s0_ref[...]
        xc_ref[:8, :] = buf_ref[0]

    xc_ref[8:, :] = x_ref[...]
    tail_ref[0] = x_ref[c - 8:, :]

    def conv(lo):
        xcol = xc_ref[:, lo:lo + 128]
        acc = xcol[8:] * cw_ref[CONV_W - 1:CONV_W, lo:lo + 128]
        for s in range(1, CONV_W):
            acc = acc + pltpu.roll(xcol, s, axis=0)[8:] * cw_ref[CONV_W - 1 - s:CONV_W - s, lo:lo + 128]
        return jax.nn.silu(acc)

    def l2n(t):
        return t * lax.rsqrt(jnp.sum(t * t, axis=-1, keepdims=True) + EPS)

    ii = lax.broadcasted_iota(jnp.int32, (c, c), 0)
    jj = lax.broadcasted_iota(jnp.int32, (c, c), 1)
    tri = ii >= jj
    strict = ii > jj
    ng = ng_ref[...]
    hk = H_A * DK_A
    for h0 in range(0, H_A, group):
        heads = range(h0, h0 + group)
        st = []
        for h in heads:
            lo = h * DK_A
            q = l2n(conv(lo)) * (DK_A ** -0.5)
            k = l2n(conv(hk + lo))
            v = conv(2 * hk + lo)
            gc = gcol_ref[:, h:h + 1]
            beta = bcol_ref[:, h:h + 1]
            gr = grow_ref[0, 0, h:h + 1, :]
            glast = gr[:, c - 1:c]
            decay = jnp.exp(jnp.where(tri, gc - gr, -jnp.inf))
            egc = jnp.exp(gc)
            kb = k * beta
            kbf = k.astype(BF16)
            m = lax.dot_general(jnp.concatenate([kb, q], axis=0).astype(BF16), kbf,
                                (((1,), (1,)), ((), ())), preferred_element_type=F32)
            s = s_ref[0, h]
            sq = _bdot(jnp.concatenate([kb * egc, q * egc], axis=0), s)
            st.append(dict(lo=lo, m=m, sq=sq, s=s, decay=decay, vb=v * beta,
                           kd=(k * jnp.exp(glast - gc)).astype(BF16), cd=jnp.exp(glast)))
        for d in st:
            d["p"] = (-jnp.where(strict, d["m"][:c] * d["decay"], 0.0)).astype(BF16)
            d["attn"] = (d["m"][c:] * d["decay"]).astype(BF16)
            d["x"] = d["vb"] - d["sq"][:c]
        n_sq = 1
        while n_sq < c:
            last = 2 * n_sq >= c
            for d in st:
                d["x"] = d["x"] + jnp.dot(d["p"], d["x"].astype(BF16), preferred_element_type=F32)
            if not last:
                for d in st:
                    d["p"] = jnp.dot(d["p"], d["p"], preferred_element_type=F32).astype(BF16)
            n_sq *= 2
        for h, d in zip(heads, st):
            xb = d["x"].astype(BF16)
            o = d["sq"][c:] + jnp.dot(d["attn"], xb, preferred_element_type=F32)
            s_ref[0, h] = d["s"] * d["cd"] + lax.dot_general(
                d["kd"], xb, (((0,), (0,)), ((), ())), preferred_element_type=F32)
            lo = d["lo"]
            o = o * lax.rsqrt(jnp.mean(o * o, axis=-1, keepdims=True) + EPS) * ng
            o = o * jax.nn.silu(z_ref[:, lo:lo + DV_A])
            o_ref[:, lo:lo + DV_A] = o.astype(o_ref.dtype)
    xc_ref[:8, :] = xc_ref[c:c + 8, :]


def _gdn(proj, gc, beta, grow, s0, layer, stack, layer_out, buf8, conv_w, norm_g, row_off, n, l, c, group,
         out_dtype):
    nc = l // c
    base = row_off // c
    rows = lambda w, col: pl.BlockSpec((c, w), lambda i, j: (base + i * nc + j, col))
    return pl.pallas_call(
        functools.partial(_gdn_kernel, c=c, group=group), grid=(n, nc),
        in_specs=[rows(CONV_DIM, 0), rows(BRANCH_W, COL_Z // BRANCH_W), rows(128, 0), rows(128, 0),
                  pl.BlockSpec((1, 1, H_A, c), lambda i, j: (i, j, 0, 0)),
                  pl.BlockSpec((None, 1, H_A, DK_A, DV_A), lambda i, j: (layer, i, 0, 0, 0)),
                  pl.BlockSpec((1, 8, CONV_DIM), lambda i, j: (i, 0, 0)),
                  pl.BlockSpec((CONV_W, CONV_DIM), lambda i, j: (0, 0)),
                  pl.BlockSpec((1, DV_A), lambda i, j: (0, 0)),
                  pl.BlockSpec(memory_space=pl.ANY)],
        out_specs=[pl.BlockSpec((c, BRANCH_W), lambda i, j: (i * nc + j, 0)),
                   pl.BlockSpec((None, 1, H_A, DK_A, DV_A), lambda i, j: (layer_out, i, 0, 0, 0)),
                   pl.BlockSpec((1, 8, CONV_DIM), lambda i, j: (i, 0, 0))],
        out_shape=[jax.ShapeDtypeStruct((n * l, BRANCH_W), out_dtype),
                   jax.ShapeDtypeStruct(stack.shape, F32),
                   jax.ShapeDtypeStruct((n, 8, CONV_DIM), F32)],
        scratch_shapes=[pltpu.VMEM((c + 8, CONV_DIM), F32)],
        input_output_aliases={9: 1},
        compiler_params=_cparams(("parallel", "arbitrary")))(
            proj, proj, gc, beta, grow, s0, buf8, conv_w, norm_g.reshape(1, DV_A), stack)


def _ret_kernel(qk_ref, v_ref, gate_ref, cos_ref, sin_ref, r0_ref, stack_ref, o_ref, r_ref, *, c):
    @pl.when(pl.program_id(1) == 0)
    def _():
        r_ref[...] = r0_ref[...]

    ii = lax.broadcasted_iota(jnp.int32, (c, c), 0)
    jj = lax.broadcasted_iota(jnp.int32, (c, c), 1)
    rel = (ii - jj).astype(F32)
    idx = lax.broadcasted_iota(jnp.int32, (c, 1), 0).astype(F32)
    even = (lax.broadcasted_iota(jnp.int32, (c, DK_B), 1) & 1) == 0
    cos = cos_ref[...]
    sin = sin_ref[...]

    def rot(x):
        swapped = jnp.where(even, pltpu.roll(x, DK_B - 1, axis=1), pltpu.roll(x, 1, axis=1))
        return x * cos + swapped * sin

    lgs = [math.log1p(-2.0 ** (-5.0 - h)) for h in range(H_B)]
    qs = [rot(qk_ref[:, h * DK_B:(h + 1) * DK_B]) for h in range(H_B)]
    ks = [rot(qk_ref[:, H_B * DK_B + h * DK_B:H_B * DK_B + (h + 1) * DK_B]) * (DK_B ** -0.5)
          for h in range(H_B)]
    vs = [v_ref[:, h * DV_B:(h + 1) * DV_B].astype(BF16) for h in range(H_B)]
    rs = [r_ref[0, h] for h in range(H_B)]
    scores = [_bdot_nt(qs[h], ks[h]) for h in range(H_B)]
    cross = [_bdot(qs[h] * jnp.exp(lgs[h] * (idx + 1.0)), rs[h]) for h in range(H_B)]
    for h in range(H_B):
        r_ref[0, h] = rs[h] * math.exp(lgs[h] * c) + _bdot_tn(
            ks[h] * jnp.exp(lgs[h] * (c - 1.0 - idx)), vs[h])
    inners = [_bdot(scores[h] * jnp.where(rel >= 0, jnp.exp(lgs[h] * jnp.maximum(rel, 0.0)), 0.0), vs[h])
              for h in range(H_B)]
    for h in range(H_B):
        o = inners[h] + cross[h]
        mu = jnp.mean(o, axis=-1, keepdims=True)
        d = o - mu
        var = jnp.mean(d * d, axis=-1, keepdims=True)
        o = d * lax.rsqrt(var + EPS)
        o = jax.nn.silu(gate_ref[:, h * DV_B:(h + 1) * DV_B]) * o
        o_ref[:, h * DV_B:(h + 1) * DV_B] = o.astype(o_ref.dtype)


def _rope_tables(pos):
    half = DK_B // 2
    inv_freq = 1.0 / (ROPE_BASE ** jnp.linspace(0.0, 1.0, half, dtype=F32))
    ang = pos.astype(F32)[:, None] * inv_freq[None, :]
    sin = jnp.sin(ang)
    cos = jnp.cos(ang)
    cos2 = jnp.stack([cos, cos], axis=-1).reshape(-1, DK_B)
    sin2 = jnp.stack([-sin, sin], axis=-1).reshape(-1, DK_B)
    return cos2, sin2


def _retention(proj, cos2, sin2, r0, layer, stack, layer_out, row_off, n, l, c, out_dtype):
    nc = l // c
    base = row_off // c
    col = lambda off: pl.BlockSpec((c, BRANCH_W), lambda i, j: (base + i * nc + j, off // BRANCH_W))
    tab = pl.BlockSpec((c, DK_B), lambda i, j: (j, 0))
    return pl.pallas_call(
        functools.partial(_ret_kernel, c=c), grid=(n, nc),
        in_specs=[col(COL_QKB), col(COL_VB), col(COL_GB), tab, tab,
                  pl.BlockSpec((None, 1, H_B, DK_B, DV_B), lambda i, j: (layer, i, 0, 0, 0)),
                  pl.BlockSpec(memory_space=pl.ANY)],
        out_specs=[pl.BlockSpec((c, BRANCH_W), lambda i, j: (i * nc + j, 0)),
                   pl.BlockSpec((None, 1, H_B, DK_B, DV_B), lambda i, j: (layer_out, i, 0, 0, 0))],
        out_shape=[jax.ShapeDtypeStruct((n * l, BRANCH_W), out_dtype),
                   jax.ShapeDtypeStruct(stack.shape, F32)],
        input_output_aliases={6: 1},
        compiler_params=_cparams(("parallel", "arbitrary")))(proj, proj, proj, cos2, sin2, r0, stack)


def _gelu(x):
    return 0.5 * x * (1.0 + lax.erf(x * (2.0 ** -0.5)))


def _gmlp_kernel(u_ref, v_ref, w_ref, bcol_ref, lng_ref, lnb_ref, o_ref, *vrows_ref, seq):
    u32 = _gelu(u_ref[...])
    v32 = _gelu(v_ref[...])
    mu = jnp.mean(v32, axis=-1, keepdims=True)
    d = v32 - mu
    var = jnp.mean(d * d, axis=-1, keepdims=True)
    vn = d * lax.rsqrt(var + EPS) * lng_ref[...] + lnb_ref[...]
    if vrows_ref:
        vrows_ref[0][...] = vn
    ii = lax.broadcasted_iota(jnp.int32, (CHUNK_C, CHUNK_C), 0)
    jj = lax.broadcasted_iota(jnp.int32, (CHUNK_C, CHUNK_C), 1)
    mask = (ii >= jj) & ((ii & -seq) == (jj & -seq))
    for g in range(G_C):
        w = jnp.where(mask, w_ref[g], 0.0)
        mixed = _bdot(w, vn[:, g * DG_C:(g + 1) * DG_C]) + bcol_ref[:, g:g + 1]
        o_ref[:, g * DG_C:(g + 1) * DG_C] = (u32[:, g * DG_C:(g + 1) * DG_C] * mixed).astype(o_ref.dtype)


def _gmlp(proj, w_tile, b_col, ln_g, ln_b, row_off, rows, seq, want_vrows):
    base = row_off // CHUNK_C
    col = lambda off: pl.BlockSpec((CHUNK_C, BRANCH_W), lambda i: (base + i, off // BRANCH_W))
    row = pl.BlockSpec((CHUNK_C, BRANCH_W), lambda i: (i, 0))
    vec = pl.BlockSpec((1, BRANCH_W), lambda i: (0, 0))
    out_shape = [jax.ShapeDtypeStruct((rows, BRANCH_W), BF16)]
    out_specs = [row]
    if want_vrows:
        out_shape.append(jax.ShapeDtypeStruct((rows, BRANCH_W), F32))
        out_specs.append(row)
    res = pl.pallas_call(
        functools.partial(_gmlp_kernel, seq=seq), grid=(rows // CHUNK_C,),
        in_specs=[col(COL_U), col(COL_V),
                  pl.BlockSpec((G_C, CHUNK_C, CHUNK_C), lambda i: (0, 0, 0)),
                  pl.BlockSpec((CHUNK_C, G_C), lambda i: (0, 0)), vec, vec],
        out_specs=out_specs, out_shape=out_shape,
        compiler_params=_cparams(("parallel",)))(
            proj, proj, w_tile, b_col, ln_g.reshape(1, BRANCH_W), ln_b.reshape(1, BRANCH_W))
    return res if want_vrows else (res[0], None)


AB_COL0 = 4 * BRANCH_W
AB_COLS = 2 * H_A


def _pack_kernel(w_ref, o_ref):
    o_ref[...] = w_ref[0].astype(BF16)


def _pack_w_in_t(w_in_t, layer, tn):
    k = w_in_t.shape[2]

    def src_row(j):
        r = j * tn
        return pl.multiple_of(r + jnp.where(r >= AB_COL0, AB_COLS, 0), 8)

    return pl.pallas_call(
        _pack_kernel, grid=(N_PROJ // tn,),
        in_specs=[pl.BlockSpec((pl.Element(1), pl.Element(tn), pl.Element(k)),
                               lambda j: (layer, src_row(j), 0))],
        out_specs=pl.BlockSpec((tn, k), lambda j: (j, 0)),
        out_shape=jax.ShapeDtypeStruct((N_PROJ, k), BF16),
        compiler_params=_cparams(("parallel",)))(w_in_t)


def _cast_rows_kernel(w_ref, o_ref, *, tr, rows_valid):
    row = pl.program_id(0) * tr + lax.broadcasted_iota(jnp.int32, w_ref.shape, 0)
    o_ref[...] = jnp.where(row < rows_valid, w_ref[...], 0.0).astype(BF16)


def _cast_rows(w, layer, rows_out, tr):
    rows, cols = w.shape[1:]
    return pl.pallas_call(
        functools.partial(_cast_rows_kernel, tr=tr, rows_valid=rows), grid=(rows_out // tr,),
        in_specs=[pl.BlockSpec((None, tr, cols), lambda r: (layer, r, 0))],
        out_specs=pl.BlockSpec((tr, cols), lambda r: (r, 0)),
        out_shape=jax.ShapeDtypeStruct((rows_out, cols), BF16),
        compiler_params=_cparams(("parallel",)))(w)


def _ab_weights_t(w_in_t, layer):
    ab = w_in_t[layer, AB_COL0:AB_COL0 + AB_COLS, :]
    zeros = jnp.zeros((128 - H_A, ab.shape[1]), F32)
    return jnp.concatenate([ab[:H_A], zeros, ab[H_A:], zeros], axis=0)


def kernel(x_prompt, x_sample, state_gdn, state_conv, state_ret, w_in, conv_w, a_log, dt_bias, gdn_norm, w_s, b_s, ln_c_g, ln_c_b, w_br, w_o, g_pre_mix, g_post_mix, g_pre_ffn, g_post_ffn, w_ffn_gate, w_ffn_up, w_ffn_down):
    n_p, l_p, _ = x_prompt.shape
    n_s, l_s, _ = x_sample.shape
    depth = w_in.shape[0]
    rows_p, rows_s = n_p * l_p, n_s * l_s
    xs = (x_prompt.reshape(rows_p, D_MODEL), x_sample.reshape(rows_s, D_MODEL))
    tr = 256

    cos_p, sin_p = _rope_tables(jnp.arange(l_p))
    cos_s, sin_s = _rope_tables(PAST_LEN + jnp.arange(l_s))
    zero_gdn = jnp.zeros((1, n_p, H_A, DK_A, DV_A), F32)
    zero_ret = jnp.zeros((1, n_p, H_B, DK_B, DV_B), F32)
    zero_buf8 = jnp.zeros((n_p, 8, CONV_DIM), F32)
    reps = CHUNK_C // l_s

    w_in_t = jnp.swapaxes(w_in, 1, 2)

    gdn_p = jnp.zeros((depth, n_p, H_A, DK_A, DV_A), F32)
    gdn_s = jnp.zeros((depth, n_s, H_A, DK_A, DV_A), F32)
    ret_p = jnp.zeros((depth, n_p, H_B, DK_B, DV_B), F32)
    ret_s = jnp.zeros((depth, n_s, H_B, DK_B, DV_B), F32)
    outs = {k: [] for k in ("conv_p", "conv_s", "vrows")}
    h = _prenorm(xs[0], xs[1], g_pre_mix[0], tr)
    for l in range(depth):
        proj = _matmul_nt(h, _pack_w_in_t(w_in_t, l, 512), F32, 1024, 1024)
        ab = _matmul_nt(h, _ab_weights_t(w_in_t, l), F32, 1024, 256)

        gc, beta = _gdn_prep(ab, a_log[l], dt_bias[l], 1024, rows_p)
        grow_p = gc[:rows_p, :H_A].reshape(n_p, l_p // MIX_CHUNK, MIX_CHUNK, H_A).transpose(0, 1, 3, 2)
        grow_s = gc[rows_p:, :H_A].reshape(n_s, 1, l_s, H_A).transpose(0, 1, 3, 2)
        buf8_s = jnp.pad(state_conv[l], ((0, 0), (8 - (CONV_W - 1), 0), (0, 0)))
        oa_p, gdn_p, tail_p = _gdn(proj, gc, beta, grow_p, zero_gdn, 0, gdn_p, l, zero_buf8, conv_w[l],
                                   gdn_norm[l], 0, n_p, l_p, MIX_CHUNK, GDN_GROUP, BF16)
        oa_s, gdn_s, tail_s = _gdn(proj, gc, beta, grow_s, state_gdn, l, gdn_s, l, buf8_s, conv_w[l],
                                   gdn_norm[l], rows_p, n_s, l_s, l_s, GDN_GROUP, F32)
        outs["conv_p"].append(tail_p[:, 8 - (CONV_W - 1):])
        outs["conv_s"].append(tail_s[:, 8 - (CONV_W - 1):])

        ob_p, ret_p = _retention(proj, cos_p, sin_p, zero_ret, 0, ret_p, l, 0, n_p, l_p, RET_CHUNK, BF16)
        ob_s, ret_s = _retention(proj, cos_s, sin_s, state_ret, l, ret_s, l, rows_p, n_s, l_s, l_s, F32)

        oc_p, _ = _gmlp(proj, w_s[l], b_s[l].T, ln_c_g[l], ln_c_b[l], 0, rows_p, CHUNK_C, False)
        w_tile_s = jnp.tile(w_s[l][:, :l_s, :l_s], (1, reps, reps))
        b_col_s = jnp.tile(b_s[l][:, :l_s].T, (reps, 1))
        oc_s, vrows = _gmlp(proj, w_tile_s, b_col_s, ln_c_g[l], ln_c_b[l], rows_p, rows_s, l_s, True)
        outs["vrows"].append(vrows.reshape(n_s, l_s, BRANCH_W))

        o_a = jnp.concatenate([oa_p, oa_s.astype(BF16)], axis=0)
        o_b = jnp.concatenate([ob_p, ob_s.astype(BF16)], axis=0)
        o_c = jnp.concatenate([oc_p, oc_s], axis=0)
        w_br_l = _cast_rows(w_br.reshape(depth, 3 * BRANCH_W, D_MODEL), l, 3 * BRANCH_W, 512)
        merged = _merge(o_a, o_b, o_c, w_br_l.reshape(3, BRANCH_W, D_MODEL), proj, 1024, 1024)
        mix = _matmul_ws(merged, w_o, l, F32, 1024, 512)
        x, h = _post(xs, mix, g_post_mix[l], g_pre_ffn[l], tr)

        wd = _cast_rows(w_ffn_down, l, D_FF_PAD, 512)
        f1 = _ffn_up(h, w_ffn_gate, w_ffn_up, l, 512, 512, D_FF_PAD)
        f = _matmul_kgrid(f1, wd, F32, 1024, 1024, D_FF_PAD // 4)
        if l + 1 < depth:
            x, h = _post((x,), f, g_post_ffn[l], g_pre_mix[l + 1], tr)
            xs = (x,)
        else:
            y_p, y_s = _post_split(x, f, g_post_ffn[l], rows_p, tr)

    y_p = y_p.reshape(n_p, l_p, D_MODEL)
    y_s = y_s.reshape(n_s, l_s, D_MODEL)
    st = lambda k: jnp.stack(outs[k])
    return (y_p, y_s, gdn_p, gdn_s, st("conv_p"), st("conv_s"), ret_p, ret_s, st("vrows"))
```

```python
import functools
import math

import jax
import jax.numpy as jnp
from jax import lax
from jax.experimental import pallas as pl
from jax.experimental.pallas import tpu as pltpu

F32 = jnp.float32
BF16 = jnp.bfloat16

D_MODEL = 4096
BRANCH_W = 2048
H_A, DK_A, DV_A = 16, 128, 128
CONV_W = 4
CONV_DIM = 6144
H_B, DK_B, DV_B = 8, 128, 256
G_C, DG_C, CHUNK_C = 8, 256, 128
D_FF = 11008
D_FF_PAD = 11264
EPS = 1e-6
ROPE_BASE = 10000.0
PAST_LEN = 16384
MIX_CHUNK = 64
RET_CHUNK = 128
GDN_GROUP = 8
GDN_SAMPLE_SEQS = 2
RET_SAMPLE_SEQS = 4

COL_QKV, COL_Z, COL_QKB, COL_VB, COL_GB, COL_U, COL_V, COL_GATE = (
    0, 6144, 8192, 10240, 12288, 14336, 16384, 18432)
N_PROJ = 30720
VMEM_LIMIT = 56 * 1024 * 1024


def _cparams(sem):
    return pltpu.CompilerParams(dimension_semantics=sem, vmem_limit_bytes=VMEM_LIMIT)


def _sigmoid(x):
    return 0.5 * jnp.tanh(0.5 * x) + 0.5


def _bdot(a, b):
    return jnp.dot(a.astype(BF16), b.astype(BF16), preferred_element_type=F32)


def _bdot_nt(a, b):
    return lax.dot_general(a.astype(BF16), b.astype(BF16), (((1,), (1,)), ((), ())),
                           preferred_element_type=F32)


def _bdot_tn(a, b):
    return lax.dot_general(a.astype(BF16), b.astype(BF16), (((0,), (0,)), ((), ())),
                           preferred_element_type=F32)


def _mm_nt_kernel(a_ref, bt_ref, o_ref):
    o_ref[...] = lax.dot_general(a_ref[...], bt_ref[...].astype(BF16), (((1,), (1,)), ((), ())),
                                 preferred_element_type=F32).astype(o_ref.dtype)


def _matmul_nt(a, bt, out_dtype, tm, tn):
    m, k = a.shape
    n = bt.shape[0]
    return pl.pallas_call(
        _mm_nt_kernel, grid=(m // tm, n // tn),
        in_specs=[pl.BlockSpec((tm, k), lambda i, j: (i, 0)),
                  pl.BlockSpec((tn, k), lambda i, j: (j, 0))],
        out_specs=pl.BlockSpec((tm, tn), lambda i, j: (i, j)),
        out_shape=jax.ShapeDtypeStruct((m, n), out_dtype),
        compiler_params=_cparams(("parallel", "parallel")))(a, bt)


def _mm_ws_kernel(a_ref, b_ref, o_ref, bw_ref):
    @pl.when(pl.program_id(1) == 0)
    def _():
        bw_ref[...] = b_ref[...].astype(BF16)

    o_ref[...] = jnp.dot(a_ref[...], bw_ref[...], preferred_element_type=F32).astype(o_ref.dtype)


def _matmul_ws(a, b, layer, out_dtype, tm, tn):
    m, k = a.shape
    n = b.shape[2]
    return pl.pallas_call(
        _mm_ws_kernel, grid=(n // tn, m // tm),
        in_specs=[pl.BlockSpec((tm, k), lambda j, i: (i, 0)),
                  pl.BlockSpec((None, k, tn), lambda j, i: (layer, 0, j))],
        out_specs=pl.BlockSpec((tm, tn), lambda j, i: (i, j)),
        out_shape=jax.ShapeDtypeStruct((m, n), out_dtype),
        scratch_shapes=[pltpu.VMEM((k, tn), BF16)],
        compiler_params=_cparams(("arbitrary", "arbitrary")))(a, b)


def _mm_acc_kernel(a_ref, b_ref, o_ref, acc_ref):
    k = pl.program_id(2)

    @pl.when(k == 0)
    def _():
        acc_ref[...] = jnp.zeros_like(acc_ref)

    acc_ref[...] += jnp.dot(a_ref[...], b_ref[...], preferred_element_type=F32)

    @pl.when(k == pl.num_programs(2) - 1)
    def _():
        o_ref[...] = acc_ref[...].astype(o_ref.dtype)


def _matmul_kgrid(a, b, out_dtype, tm, tn, tk):
    m, k = a.shape
    n = b.shape[1]
    return pl.pallas_call(
        _mm_acc_kernel, grid=(m // tm, n // tn, k // tk),
        in_specs=[pl.BlockSpec((tm, tk), lambda i, j, kk: (i, kk)),
                  pl.BlockSpec((tk, tn), lambda i, j, kk: (kk, j))],
        out_specs=pl.BlockSpec((tm, tn), lambda i, j, kk: (i, j)),
        out_shape=jax.ShapeDtypeStruct((m, n), out_dtype),
        scratch_shapes=[pltpu.VMEM((tm, tn), F32)],
        compiler_params=_cparams(("parallel", "parallel", "arbitrary")))(a, b)


def _ffn_up_kernel(h_ref, wg_ref, wu_ref, o_ref, wgb_ref, wub_ref, *, tn, n_valid):
    j = pl.program_id(0)

    @pl.when(pl.program_id(1) == 0)
    def _():
        wgb_ref[...] = wg_ref[...].astype(BF16)
        wub_ref[...] = wu_ref[...].astype(BF16)

    h = h_ref[...]
    g = jnp.dot(h, wgb_ref[...], preferred_element_type=F32)
    u = jnp.dot(h, wub_ref[...], preferred_element_type=F32)
    col = j * tn + lax.broadcasted_iota(jnp.int32, g.shape, 1)
    o_ref[...] = jnp.where(col < n_valid, g * _sigmoid(g) * u, 0.0).astype(o_ref.dtype)


def _ffn_up(h, wg, wu, layer, tm, tn, n_out):
    m, k = h.shape
    n = wg.shape[2]
    wspec = pl.BlockSpec((None, k, tn), lambda j, i: (layer, 0, j))
    return pl.pallas_call(
        functools.partial(_ffn_up_kernel, tn=tn, n_valid=n), grid=(n_out // tn, m // tm),
        in_specs=[pl.BlockSpec((tm, k), lambda j, i: (i, 0)), wspec, wspec],
        out_specs=pl.BlockSpec((tm, tn), lambda j, i: (i, j)),
        out_shape=jax.ShapeDtypeStruct((m, n_out), BF16),
        scratch_shapes=[pltpu.VMEM((k, tn), BF16)] * 2,
        compiler_params=_cparams(("arbitrary", "arbitrary")))(h, wg, wu)


def _merge_kernel(oa_ref, ob_ref, oc_ref, w_ref, gate_ref, o_ref, acc_ref):
    b = pl.program_id(2)
    sig = _sigmoid(gate_ref[...])

    @pl.when(b == 0)
    def _():
        acc_ref[...] = sig * jnp.dot(oa_ref[...], w_ref[0], preferred_element_type=F32)

    @pl.when(b == 1)
    def _():
        acc_ref[...] += sig * jnp.dot(ob_ref[...], w_ref[0], preferred_element_type=F32)

    @pl.when(b == 2)
    def _():
        o_ref[...] = (acc_ref[...] + sig * jnp.dot(oc_ref[...], w_ref[0],
                                                   preferred_element_type=F32)).astype(o_ref.dtype)


def _merge(o_a, o_b, o_c, w_br, proj, tm, tn):
    m = o_a.shape[0]
    gate_blk0 = COL_GATE // tn
    per_branch = D_MODEL // tn
    o_spec = pl.BlockSpec((tm, BRANCH_W), lambda i, j, b: (i, 0))
    return pl.pallas_call(
        _merge_kernel, grid=(m // tm, D_MODEL // tn, 3),
        in_specs=[o_spec, o_spec, o_spec,
                  pl.BlockSpec((1, BRANCH_W, tn), lambda i, j, b: (b, 0, j)),
                  pl.BlockSpec((tm, tn), lambda i, j, b: (i, gate_blk0 + b * per_branch + j))],
        out_specs=pl.BlockSpec((tm, tn), lambda i, j, b: (i, j)),
        out_shape=jax.ShapeDtypeStruct((m, D_MODEL), BF16),
        scratch_shapes=[pltpu.VMEM((tm, tn), F32)],
        compiler_params=_cparams(("parallel", "parallel", "arbitrary")))(o_a, o_b, o_c, w_br, proj)


def _rms(x, g):
    return x * lax.rsqrt(jnp.mean(x * x, axis=-1, keepdims=True) + EPS) * g


def _two_source_specs(tr, nb_first):
    first = pl.BlockSpec((tr, D_MODEL), lambda i: (jnp.minimum(i, nb_first - 1), 0))
    second = pl.BlockSpec((tr, D_MODEL), lambda i: (jnp.maximum(i - nb_first, 0), 0))
    return first, second


def _prenorm_kernel(xa_ref, xb_ref, g_ref, h_ref, *, nb_first):
    i = pl.program_id(0)

    @pl.when(i < nb_first)
    def _():
        h_ref[...] = _rms(xa_ref[...], g_ref[...]).astype(h_ref.dtype)

    @pl.when(i >= nb_first)
    def _():
        h_ref[...] = _rms(xb_ref[...], g_ref[...]).astype(h_ref.dtype)


def _prenorm(xa, xb, g, tr):
    ma, mb = xa.shape[0], xb.shape[0]
    sa, sb = _two_source_specs(tr, ma // tr)
    return pl.pallas_call(
        functools.partial(_prenorm_kernel, nb_first=ma // tr), grid=((ma + mb) // tr,),
        in_specs=[sa, sb, pl.BlockSpec((1, D_MODEL), lambda i: (0, 0))],
        out_specs=pl.BlockSpec((tr, D_MODEL), lambda i: (i, 0)),
        out_shape=jax.ShapeDtypeStruct((ma + mb, D_MODEL), BF16),
        compiler_params=_cparams(("arbitrary",)))(xa, xb, g.reshape(1, D_MODEL))


def _post_kernel(*refs, nb_first, n_src, want_h):
    x_refs = refs[:n_src]
    y_ref, gp_ref, gn_ref, xo_ref = refs[n_src:n_src + 4]
    r = _rms(y_ref[...], gp_ref[...])

    def finish(x_ref):
        xn = x_ref[...] + r
        xo_ref[...] = xn
        if want_h:
            refs[n_src + 4][...] = _rms(xn, gn_ref[...]).astype(BF16)

    if n_src == 1:
        finish(x_refs[0])
    else:
        i = pl.program_id(0)
        pl.when(i < nb_first)(lambda: finish(x_refs[0]))
        pl.when(i >= nb_first)(lambda: finish(x_refs[1]))


def _post(xs, y, g_post, g_next, tr):
    m = y.shape[0]
    want_h = g_next is not None
    row = pl.BlockSpec((tr, D_MODEL), lambda i: (i, 0))
    vec = pl.BlockSpec((1, D_MODEL), lambda i: (0, 0))
    nb_first = xs[0].shape[0] // tr
    x_specs = list(_two_source_specs(tr, nb_first)) if len(xs) == 2 else [row]
    out_shape = [jax.ShapeDtypeStruct((m, D_MODEL), F32)]
    out_specs = [row]
    if want_h:
        out_shape.append(jax.ShapeDtypeStruct((m, D_MODEL), BF16))
        out_specs.append(row)
    gn = (g_next if want_h else g_post).reshape(1, D_MODEL)
    res = pl.pallas_call(
        functools.partial(_post_kernel, nb_first=nb_first, n_src=len(xs), want_h=want_h), grid=(m // tr,),
        in_specs=x_specs + [row, vec, vec], out_specs=out_specs, out_shape=out_shape,
        compiler_params=_cparams(("arbitrary",)))(*xs, y, g_post.reshape(1, D_MODEL), gn)
    return (res[0], res[1]) if want_h else (res[0], None)


def _post_split_kernel(x_ref, y_ref, gp_ref, oa_ref, ob_ref, *, nb_first):
    i = pl.program_id(0)
    xn = x_ref[...] + _rms(y_ref[...], gp_ref[...])

    @pl.when(i < nb_first)
    def _():
        oa_ref[...] = xn

    @pl.when(i >= nb_first)
    def _():
        ob_ref[...] = xn


def _post_split(x, y, g_post, rows_first, tr):
    m = x.shape[0]
    row = pl.BlockSpec((tr, D_MODEL), lambda i: (i, 0))
    vec = pl.BlockSpec((1, D_MODEL), lambda i: (0, 0))
    oa, ob = _two_source_specs(tr, rows_first // tr)
    return pl.pallas_call(
        functools.partial(_post_split_kernel, nb_first=rows_first // tr), grid=(m // tr,),
        in_specs=[row, row, vec], out_specs=[oa, ob],
        out_shape=[jax.ShapeDtypeStruct((rows_first, D_MODEL), F32),
                   jax.ShapeDtypeStruct((m - rows_first, D_MODEL), F32)],
        compiler_params=_cparams(("arbitrary",)))(x, y, g_post.reshape(1, D_MODEL))


def _gdn_prep_kernel(a_ref, b_ref, alog_ref, dt_ref, gc_ref, beta_ref, *, prompt_blocks):
    i = pl.program_id(0)
    x = a_ref[...] + dt_ref[...]
    softplus = jnp.maximum(x, 0.0) + jnp.log1p(jnp.exp(-jnp.abs(x)))
    g = -jnp.exp(alog_ref[...]) * softplus
    chunk = jnp.where(i < prompt_blocks, MIX_CHUNK, 8)
    rmod = lax.broadcasted_iota(jnp.int32, g.shape, 0) & (chunk - 1)
    s = 1
    while s < MIX_CHUNK:
        g = g + jnp.where(rmod >= s, pltpu.roll(g, s, axis=0), 0.0)
        s *= 2
    gc_ref[...] = g
    beta_ref[...] = jax.nn.sigmoid(b_ref[...])


def _gdn_prep(ab, a_log, dt_bias, tr, prompt_rows):
    m = ab.shape[0]
    pad = lambda v: jnp.pad(v.astype(F32), (0, 128 - H_A)).reshape(1, 128)
    blk = lambda c: pl.BlockSpec((tr, 128), lambda i: (i, c))
    vec = pl.BlockSpec((1, 128), lambda i: (0, 0))
    return pl.pallas_call(
        functools.partial(_gdn_prep_kernel, prompt_blocks=prompt_rows // tr), grid=(m // tr,),
        in_specs=[blk(0), blk(1), vec, vec], out_specs=[blk(0), blk(0)],
        out_shape=[jax.ShapeDtypeStruct((m, 128), F32)] * 2,
        compiler_params=_cparams(("parallel",)))(ab, ab, pad(a_log), pad(dt_bias))


def _gdn_kernel(x_ref, z_ref, gcol_ref, bcol_ref, grow_ref, s0_ref, buf_ref, cw_ref, ng_ref, stack_ref, obuf_ref,
                o_ref, s_ref, tail_ref, xc_ref, *, c, group, nseq):
    @pl.when(pl.program_id(1) == 0)
    def _():
        s_ref[...] = s0_ref[...]
        xc_ref[:, :8, :] = buf_ref[...]

    for q in range(nseq):
        xc_ref[q, 8:, :] = x_ref[q * c:(q + 1) * c, :]
        tail_ref[q] = x_ref[(q + 1) * c - 8:(q + 1) * c, :]

    def conv(q, lo):
        xcol = xc_ref[q, :, lo:lo + 128]
        acc = xcol[8:] * cw_ref[CONV_W - 1:CONV_W, lo:lo + 128]
        for s in range(1, CONV_W):
            acc = acc + pltpu.roll(xcol, s, axis=0)[8:] * cw_ref[CONV_W - 1 - s:CONV_W - s, lo:lo + 128]
        return jax.nn.silu(acc)

    def l2n(t):
        return t * lax.rsqrt(jnp.sum(t * t, axis=-1, keepdims=True) + EPS)

    ii = lax.broadcasted_iota(jnp.int32, (c, c), 0)
    jj = lax.broadcasted_iota(jnp.int32, (c, c), 1)
    tri = ii >= jj
    strict = ii > jj
    ng = ng_ref[...]
    hk = H_A * DK_A
    for h0 in range(0, H_A, group):
        st = []
        for q in range(nseq):
            rows = slice(q * c, (q + 1) * c)
            for h in range(h0, h0 + group):
                lo = h * DK_A
                qh = l2n(conv(q, lo)) * (DK_A ** -0.5)
                k = l2n(conv(q, hk + lo))
                v = conv(q, 2 * hk + lo)
                gc = gcol_ref[rows, h:h + 1]
                beta = bcol_ref[rows, h:h + 1]
                gr = grow_ref[q, 0, h:h + 1, :]
                glast = gr[:, c - 1:c]
                decay = jnp.exp(jnp.where(tri, gc - gr, -jnp.inf))
                egc = jnp.exp(gc)
                kb = k * beta
                m = lax.dot_general(jnp.concatenate([kb, qh], axis=0).astype(BF16), k.astype(BF16),
                                    (((1,), (1,)), ((), ())), preferred_element_type=F32)
                s = s_ref[q, h]
                sq = _bdot(jnp.concatenate([kb * egc, qh * egc], axis=0), s)
                st.append(dict(q=q, h=h, rows=rows, m=m, sq=sq, s=s, decay=decay, vb=v * beta,
                               kd=(k * jnp.exp(glast - gc)).astype(BF16), cd=jnp.exp(glast)))
        for d in st:
            d["p"] = (-jnp.where(strict, d["m"][:c] * d["decay"], 0.0)).astype(BF16)
            d["attn"] = (d["m"][c:] * d["decay"]).astype(BF16)
            d["x"] = d["vb"] - d["sq"][:c]
        n_sq = 1
        while n_sq < c:
            last = 2 * n_sq >= c
            for d in st:
                d["x"] = d["x"] + jnp.dot(d["p"], d["x"].astype(BF16), preferred_element_type=F32)
            if not last:
                for d in st:
                    d["p"] = jnp.dot(d["p"], d["p"], preferred_element_type=F32).astype(BF16)
            n_sq *= 2
        outs = {}
        for d in st:
            xb = d["x"].astype(BF16)
            o = d["sq"][c:] + jnp.dot(d["attn"], xb, preferred_element_type=F32)
            s_ref[d["q"], d["h"]] = d["s"] * d["cd"] + lax.dot_general(
                d["kd"], xb, (((0,), (0,)), ((), ())), preferred_element_type=F32)
            lo = d["h"] * DV_A
            o = o * lax.rsqrt(jnp.mean(o * o, axis=-1, keepdims=True) + EPS) * ng
            outs[d["q"], d["h"]] = o * jax.nn.silu(z_ref[d["rows"], lo:lo + DV_A])
        for h in range(h0, h0 + group):
            o = outs[0, h] if nseq == 1 else jnp.concatenate([outs[q, h] for q in range(nseq)], axis=0)
            o_ref[:, h * DV_A:(h + 1) * DV_A] = o.astype(o_ref.dtype)
    xc_ref[:, :8, :] = xc_ref[:, c:c + 8, :]


def _gdn(proj, gc, beta, grow, s0, layer, stack, layer_out, o_buf, buf8, conv_w, norm_g, row_off, n, l, c,
         group, nseq):
    nc = l // c
    assert nseq == 1 or nc == 1
    blk = nseq * c
    base = row_off // blk
    rows = lambda w, col: pl.BlockSpec((blk, w), lambda i, j: (base + i * nc + j, col))
    return pl.pallas_call(
        functools.partial(_gdn_kernel, c=c, group=group, nseq=nseq), grid=(n // nseq, nc),
        in_specs=[rows(CONV_DIM, 0), rows(BRANCH_W, COL_Z // BRANCH_W), rows(128, 0), rows(128, 0),
                  pl.BlockSpec((nseq, 1, H_A, c), lambda i, j: (i, j, 0, 0)),
                  pl.BlockSpec((None, nseq, H_A, DK_A, DV_A), lambda i, j: (layer, i, 0, 0, 0)),
                  pl.BlockSpec((nseq, 8, CONV_DIM), lambda i, j: (i, 0, 0)),
                  pl.BlockSpec((CONV_W, CONV_DIM), lambda i, j: (0, 0)),
                  pl.BlockSpec((1, DV_A), lambda i, j: (0, 0)),
                  pl.BlockSpec(memory_space=pl.ANY), pl.BlockSpec(memory_space=pl.ANY)],
        out_specs=[rows(BRANCH_W, 0),
                   pl.BlockSpec((None, nseq, H_A, DK_A, DV_A), lambda i, j: (layer_out, i, 0, 0, 0)),
                   pl.BlockSpec((nseq, 8, CONV_DIM), lambda i, j: (i, 0, 0))],
        out_shape=[jax.ShapeDtypeStruct(o_buf.shape, o_buf.dtype),
                   jax.ShapeDtypeStruct(stack.shape, F32),
                   jax.ShapeDtypeStruct((n, 8, CONV_DIM), F32)],
        scratch_shapes=[pltpu.VMEM((nseq, c + 8, CONV_DIM), F32)],
        input_output_aliases={9: 1, 10: 0},
        compiler_params=_cparams(("parallel", "arbitrary")))(
            proj, proj, gc, beta, grow, s0, buf8, conv_w, norm_g.reshape(1, DV_A), stack, o_buf)


def _ret_kernel(qk_ref, v_ref, gate_ref, cos_ref, sin_ref, r0_ref, stack_ref, obuf_ref, o_ref, r_ref, *, c, nseq):
    @pl.when(pl.program_id(1) == 0)
    def _():
        r_ref[...] = r0_ref[...]

    ii = lax.broadcasted_iota(jnp.int32, (c, c), 0)
    jj = lax.broadcasted_iota(jnp.int32, (c, c), 1)
    rel = (ii - jj).astype(F32)
    idx = lax.broadcasted_iota(jnp.int32, (c, 1), 0).astype(F32)
    even = (lax.broadcasted_iota(jnp.int32, (c, DK_B), 1) & 1) == 0
    cos = cos_ref[...]
    sin = sin_ref[...]

    def rot(x):
        swapped = jnp.where(even, pltpu.roll(x, DK_B - 1, axis=1), pltpu.roll(x, 1, axis=1))
        return x * cos + swapped * sin

    units = [(q, h) for q in range(nseq) for h in range(H_B)]
    lg = {h: math.log1p(-2.0 ** (-5.0 - h)) for h in range(H_B)}
    rows = {q: slice(q * c, (q + 1) * c) for q in range(nseq)}
    qs = {u: rot(qk_ref[rows[u[0]], u[1] * DK_B:(u[1] + 1) * DK_B]) for u in units}
    ks = {u: rot(qk_ref[rows[u[0]], (H_B + u[1]) * DK_B:(H_B + u[1] + 1) * DK_B]) * (DK_B ** -0.5)
          for u in units}
    vs = {u: v_ref[rows[u[0]], u[1] * DV_B:(u[1] + 1) * DV_B].astype(BF16) for u in units}
    rs = {u: r_ref[u] for u in units}
    scores = {u: _bdot_nt(qs[u], ks[u]) for u in units}
    cross = {u: _bdot(qs[u] * jnp.exp(lg[u[1]] * (idx + 1.0)), rs[u]) for u in units}
    for u in units:
        r_ref[u] = rs[u] * math.exp(lg[u[1]] * c) + _bdot_tn(ks[u] * jnp.exp(lg[u[1]] * (c - 1.0 - idx)), vs[u])
    inners = {u: _bdot(scores[u] * jnp.where(rel >= 0, jnp.exp(lg[u[1]] * jnp.maximum(rel, 0.0)), 0.0), vs[u])
              for u in units}
    outs = {}
    for u in units:
        o = inners[u] + cross[u]
        mu = jnp.mean(o, axis=-1, keepdims=True)
        d = o - mu
        var = jnp.mean(d * d, axis=-1, keepdims=True)
        o = d * lax.rsqrt(var + EPS)
        outs[u] = jax.nn.silu(gate_ref[rows[u[0]], u[1] * DV_B:(u[1] + 1) * DV_B]) * o
    for h in range(H_B):
        o = outs[0, h] if nseq == 1 else jnp.concatenate([outs[q, h] for q in range(nseq)], axis=0)
        o_ref[:, h * DV_B:(h + 1) * DV_B] = o.astype(o_ref.dtype)


def _rope_tables(pos):
    half = DK_B // 2
    inv_freq = 1.0 / (ROPE_BASE ** jnp.linspace(0.0, 1.0, half, dtype=F32))
    ang = pos.astype(F32)[:, None] * inv_freq[None, :]
    sin = jnp.sin(ang)
    cos = jnp.cos(ang)
    cos2 = jnp.stack([cos, cos], axis=-1).reshape(-1, DK_B)
    sin2 = jnp.stack([-sin, sin], axis=-1).reshape(-1, DK_B)
    return cos2, sin2


def _retention(proj, cos2, sin2, r0, layer, stack, layer_out, o_buf, row_off, n, l, c, nseq):
    nc = l // c
    assert nseq == 1 or nc == 1
    blk = nseq * c
    base = row_off // blk
    col = lambda off: pl.BlockSpec((blk, BRANCH_W), lambda i, j: (base + i * nc + j, off // BRANCH_W))
    tab = pl.BlockSpec((c, DK_B), lambda i, j: (j, 0))
    return pl.pallas_call(
        functools.partial(_ret_kernel, c=c, nseq=nseq), grid=(n // nseq, nc),
        in_specs=[col(COL_QKB), col(COL_VB), col(COL_GB), tab, tab,
                  pl.BlockSpec((None, nseq, H_B, DK_B, DV_B), lambda i, j: (layer, i, 0, 0, 0)),
                  pl.BlockSpec(memory_space=pl.ANY), pl.BlockSpec(memory_space=pl.ANY)],
        out_specs=[col(0),
                   pl.BlockSpec((None, nseq, H_B, DK_B, DV_B), lambda i, j: (layer_out, i, 0, 0, 0))],
        out_shape=[jax.ShapeDtypeStruct(o_buf.shape, o_buf.dtype),
                   jax.ShapeDtypeStruct(stack.shape, F32)],
        input_output_aliases={6: 1, 7: 0},
        compiler_params=_cparams(("parallel", "arbitrary")))(proj, proj, proj, cos2, sin2, r0, stack, o_buf)


def _gelu(x):
    return 0.5 * x * (1.0 + lax.erf(x * (2.0 ** -0.5)))


def _gmlp_kernel(u_ref, v_ref, w_ref, bcol_ref, lng_ref, lnb_ref, obuf_ref, o_ref, *vrows_ref, seq):
    u32 = _gelu(u_ref[...])
    v32 = _gelu(v_ref[...])
    mu = jnp.mean(v32, axis=-1, keepdims=True)
    d = v32 - mu
    var = jnp.mean(d * d, axis=-1, keepdims=True)
    vn = d * lax.rsqrt(var + EPS) * lng_ref[...] + lnb_ref[...]
    if vrows_ref:
        vrows_ref[0][...] = vn
    ii = lax.broadcasted_iota(jnp.int32, (CHUNK_C, CHUNK_C), 0)
    jj = lax.broadcasted_iota(jnp.int32, (CHUNK_C, CHUNK_C), 1)
    mask = (ii >= jj) & ((ii & -seq) == (jj & -seq))
    for g in range(G_C):
        w = jnp.where(mask, w_ref[g], 0.0)
        mixed = _bdot(w, vn[:, g * DG_C:(g + 1) * DG_C]) + bcol_ref[:, g:g + 1]
        o_ref[:, g * DG_C:(g + 1) * DG_C] = (u32[:, g * DG_C:(g + 1) * DG_C] * mixed).astype(o_ref.dtype)


def _gmlp(proj, w_tile, b_col, ln_g, ln_b, o_buf, row_off, rows, seq, want_vrows):
    base = row_off // CHUNK_C
    col = lambda off: pl.BlockSpec((CHUNK_C, BRANCH_W), lambda i: (base + i, off // BRANCH_W))
    vec = pl.BlockSpec((1, BRANCH_W), lambda i: (0, 0))
    out_shape = [jax.ShapeDtypeStruct(o_buf.shape, o_buf.dtype)]
    out_specs = [col(0)]
    if want_vrows:
        out_shape.append(jax.ShapeDtypeStruct((rows, BRANCH_W), F32))
        out_specs.append(pl.BlockSpec((CHUNK_C, BRANCH_W), lambda i: (i, 0)))
    res = pl.pallas_call(
        functools.partial(_gmlp_kernel, seq=seq), grid=(rows // CHUNK_C,),
        in_specs=[col(COL_U), col(COL_V),
                  pl.BlockSpec((G_C, CHUNK_C, CHUNK_C), lambda i: (0, 0, 0)),
                  pl.BlockSpec((CHUNK_C, G_C), lambda i: (0, 0)), vec, vec,
                  pl.BlockSpec(memory_space=pl.ANY)],
        out_specs=out_specs, out_shape=out_shape, input_output_aliases={6: 0},
        compiler_params=_cparams(("parallel",)))(
            proj, proj, w_tile, b_col, ln_g.reshape(1, BRANCH_W), ln_b.reshape(1, BRANCH_W), o_buf)
    return res if want_vrows else (res[0], None)


AB_COL0 = 4 * BRANCH_W
AB_COLS = 2 * H_A


def _pack_kernel(w_ref, o_ref):
    o_ref[...] = w_ref[0].astype(BF16)


def _pack_w_in_t(w_in_t, layer, tn):
    k = w_in_t.shape[2]

    def src_row(j):
        r = j * tn
        return pl.multiple_of(r + jnp.where(r >= AB_COL0, AB_COLS, 0), 8)

    return pl.pallas_call(
        _pack_kernel, grid=(N_PROJ // tn,),
        in_specs=[pl.BlockSpec((pl.Element(1), pl.Element(tn), pl.Element(k)),
                               lambda j: (layer, src_row(j), 0))],
        out_specs=pl.BlockSpec((tn, k), lambda j: (j, 0)),
        out_shape=jax.ShapeDtypeStruct((N_PROJ, k), BF16),
        compiler_params=_cparams(("parallel",)))(w_in_t)


def _cast_rows_kernel(w_ref, o_ref, *, tr, rows_valid):
    row = pl.program_id(0) * tr + lax.broadcasted_iota(jnp.int32, w_ref.shape, 0)
    o_ref[...] = jnp.where(row < rows_valid, w_ref[...], 0.0).astype(BF16)


def _cast_rows(w, layer, rows_out, tr):
    rows, cols = w.shape[1:]
    return pl.pallas_call(
        functools.partial(_cast_rows_kernel, tr=tr, rows_valid=rows), grid=(rows_out // tr,),
        in_specs=[pl.BlockSpec((None, tr, cols), lambda r: (layer, r, 0))],
        out_specs=pl.BlockSpec((tr, cols), lambda r: (r, 0)),
        out_shape=jax.ShapeDtypeStruct((rows_out, cols), BF16),
        compiler_params=_cparams(("parallel",)))(w)


def _ab_weights_t(w_in_t, layer):
    ab = w_in_t[layer, AB_COL0:AB_COL0 + AB_COLS, :]
    zeros = jnp.zeros((128 - H_A, ab.shape[1]), F32)
    return jnp.concatenate([ab[:H_A], zeros, ab[H_A:], zeros], axis=0)


def kernel(x_prompt, x_sample, state_gdn, state_conv, state_ret, w_in, conv_w, a_log, dt_bias, gdn_norm, w_s, b_s, ln_c_g, ln_c_b, w_br, w_o, g_pre_mix, g_post_mix, g_pre_ffn, g_post_ffn, w_ffn_gate, w_ffn_up, w_ffn_down):
    n_p, l_p, _ = x_prompt.shape
    n_s, l_s, _ = x_sample.shape
    depth = w_in.shape[0]
    rows_p, rows_s = n_p * l_p, n_s * l_s
    m = rows_p + rows_s
    xs = (x_prompt.reshape(rows_p, D_MODEL), x_sample.reshape(rows_s, D_MODEL))
    tr = 256

    cos_p, sin_p = _rope_tables(jnp.arange(l_p))
    cos_s, sin_s = _rope_tables(PAST_LEN + jnp.arange(l_s))
    zero_gdn = jnp.zeros((1, n_p, H_A, DK_A, DV_A), F32)
    zero_ret = jnp.zeros((1, n_p, H_B, DK_B, DV_B), F32)
    zero_buf8 = jnp.zeros((n_p, 8, CONV_DIM), F32)
    reps = CHUNK_C // l_s

    w_in_t = jnp.swapaxes(w_in, 1, 2)

    gdn_p = jnp.zeros((depth, n_p, H_A, DK_A, DV_A), F32)
    gdn_s = jnp.zeros((depth, n_s, H_A, DK_A, DV_A), F32)
    ret_p = jnp.zeros((depth, n_p, H_B, DK_B, DV_B), F32)
    ret_s = jnp.zeros((depth, n_s, H_B, DK_B, DV_B), F32)
    outs = {k: [] for k in ("conv_p", "conv_s", "vrows")}
    h = _prenorm(xs[0], xs[1], g_pre_mix[0], tr)
    for l in range(depth):
        proj = _matmul_nt(h, _pack_w_in_t(w_in_t, l, 512), F32, 1024, 1024)
        ab = _matmul_nt(h, _ab_weights_t(w_in_t, l), F32, 1024, 256)

        gc, beta = _gdn_prep(ab, a_log[l], dt_bias[l], 1024, rows_p)
        grow_p = gc[:rows_p, :H_A].reshape(n_p, l_p // MIX_CHUNK, MIX_CHUNK, H_A).transpose(0, 1, 3, 2)
        grow_s = gc[rows_p:, :H_A].reshape(n_s, 1, l_s, H_A).transpose(0, 1, 3, 2)
        buf8_s = jnp.pad(state_conv[l], ((0, 0), (8 - (CONV_W - 1), 0), (0, 0)))
        o_a = jnp.zeros((m, BRANCH_W), BF16)
        o_b = jnp.zeros((m, BRANCH_W), BF16)
        o_c = jnp.zeros((m, BRANCH_W), BF16)
        o_a, gdn_p, tail_p = _gdn(proj, gc, beta, grow_p, zero_gdn, 0, gdn_p, l, o_a, zero_buf8, conv_w[l],
                                  gdn_norm[l], 0, n_p, l_p, MIX_CHUNK, H_A, 1)
        o_a, gdn_s, tail_s = _gdn(proj, gc, beta, grow_s, state_gdn, l, gdn_s, l, o_a, buf8_s, conv_w[l],
                                  gdn_norm[l], rows_p, n_s, l_s, l_s, GDN_GROUP, GDN_SAMPLE_SEQS)
        outs["conv_p"].append(tail_p[:, 8 - (CONV_W - 1):])
        outs["conv_s"].append(tail_s[:, 8 - (CONV_W - 1):])

        o_b, ret_p = _retention(proj, cos_p, sin_p, zero_ret, 0, ret_p, l, o_b, 0, n_p, l_p, RET_CHUNK, 1)
        o_b, ret_s = _retention(proj, cos_s, sin_s, state_ret, l, ret_s, l, o_b, rows_p, n_s, l_s, l_s,
                                RET_SAMPLE_SEQS)

        o_c, _ = _gmlp(proj, w_s[l], b_s[l].T, ln_c_g[l], ln_c_b[l], o_c, 0, rows_p, CHUNK_C, False)
        w_tile_s = jnp.tile(w_s[l][:, :l_s, :l_s], (1, reps, reps))
        b_col_s = jnp.tile(b_s[l][:, :l_s].T, (reps, 1))
        o_c, vrows = _gmlp(proj, w_tile_s, b_col_s, ln_c_g[l], ln_c_b[l], o_c, rows_p, rows_s, l_s, True)
        outs["vrows"].append(vrows.reshape(n_s, l_s, BRANCH_W))

        w_br_l = _cast_rows(w_br.reshape(depth, 3 * BRANCH_W, D_MODEL), l, 3 * BRANCH_W, 512)
        merged = _merge(o_a, o_b, o_c, w_br_l.reshape(3, BRANCH_W, D_MODEL), proj, 1024, 1024)
        mix = _matmul_ws(merged, w_o, l, F32, 1024, 512)
        x, h = _post(xs, mix, g_post_mix[l], g_pre_ffn[l], tr)

        wd = _cast_rows(w_ffn_down, l, D_FF_PAD, 512)
        f1 = _ffn_up(h, w_ffn_gate, w_ffn_up, l, 512, 512, D_FF_PAD)
        f = _matmul_kgrid(f1, wd, F32, 1024, 1024, D_FF_PAD // 4)
        if l + 1 < depth:
            x, h = _post((x,), f, g_post_ffn[l], g_pre_mix[l + 1], tr)
            xs = (x,)
        else:
            y_p, y_s = _post_split(x, f, g_post_ffn[l], rows_p, tr)

    y_p = y_p.reshape(n_p, l_p, D_MODEL)
    y_s = y_s.reshape(n_s, l_s, D_MODEL)
    st = lambda k: jnp.stack(outs[k])
    return (y_p, y_s, gdn_p, gdn_s, st("conv_p"), st("conv_s"), ret_p, ret_s, st("vrows"))
```

```python
import functools
import math
from typing import Callable, NamedTuple

import jax
import jax.numpy as jnp
from jax import lax
from jax.experimental import pallas as pl
from jax.experimental.pallas import tpu as pltpu

F32 = jnp.float32
BF16 = jnp.bfloat16

D_MODEL = 4096
BRANCH_W = 2048
H_A, DK_A, DV_A = 16, 128, 128
CONV_W = 4
CONV_DIM = 6144
H_B, DK_B, DV_B = 8, 128, 256
G_C, DG_C, CHUNK_C = 8, 256, 128
D_FF = 11008
D_FF_PAD = 11264
EPS = 1e-6
ROPE_BASE = 10000.0
PAST_LEN = 16384
MIX_CHUNK = 64
RET_CHUNK = 128
GDN_GROUP = 8
GDN_SAMPLE_SEQS = 2
RET_SAMPLE_SEQS = 4

COL_QKV, COL_Z, COL_QKB, COL_VB, COL_GB, COL_U, COL_V, COL_GATE = (
    0, 6144, 8192, 10240, 12288, 14336, 16384, 18432)
N_PROJ = 30720
VMEM_LIMIT = 56 * 1024 * 1024
VMEM_LIMIT_HOST = 60 * 1024 * 1024


def _cparams(sem, vmem=VMEM_LIMIT):
    return pltpu.CompilerParams(dimension_semantics=sem, vmem_limit_bytes=vmem)


def _sigmoid(x):
    return 0.5 * jnp.tanh(0.5 * x) + 0.5


def _bdot(a, b):
    return jnp.dot(a.astype(BF16), b.astype(BF16), preferred_element_type=F32)


def _bdot_nt(a, b):
    return lax.dot_general(a.astype(BF16), b.astype(BF16), (((1,), (1,)), ((), ())),
                           preferred_element_type=F32)


def _bdot_tn(a, b):
    return lax.dot_general(a.astype(BF16), b.astype(BF16), (((0,), (0,)), ((), ())),
                           preferred_element_type=F32)


class _SideCast(NamedTuple):
    w: jax.Array
    layer: int
    blk: int
    n_blocks: int
    src_row: Callable
    n_copy: int


def _mm_nt_kernel(a_ref, bt_ref, *refs, side, steps_per_row):
    n_side = len(side)
    o_ref = refs[n_side]
    o_ref[...] = lax.dot_general(a_ref[...], bt_ref[...].astype(BF16), (((1,), (1,)), ((), ())),
                                 preferred_element_type=F32).astype(o_ref.dtype)
    t = pl.program_id(0) * steps_per_row + pl.program_id(1)
    for job, w_ref, so_ref in zip(side, refs[:n_side], refs[n_side + 1:]):
        v = w_ref[0].astype(BF16)
        so_ref[...] = v if job.n_copy >= job.n_blocks else jnp.where(t < job.n_copy, v, jnp.zeros_like(v))


def _matmul_nt(a, bt, out_dtype, tm, tn, side=()):
    m, k = a.shape
    n = bt.shape[0]
    nj = n // tn
    assert all(job.n_blocks <= (m // tm) * nj for job in side)

    def blk_idx(job, i, j):
        return jnp.minimum(i * nj + j, job.n_blocks - 1)

    side_in = [pl.BlockSpec((pl.Element(1), pl.Element(job.blk), pl.Element(job.w.shape[2])),
                            lambda i, j, job=job: (job.layer, pl.multiple_of(job.src_row(blk_idx(job, i, j)), 8), 0))
               for job in side]
    side_out = [pl.BlockSpec((job.blk, job.w.shape[2]), lambda i, j, job=job: (blk_idx(job, i, j), 0))
                for job in side]
    res = pl.pallas_call(
        functools.partial(_mm_nt_kernel, side=tuple(side), steps_per_row=nj), grid=(m // tm, nj),
        in_specs=[pl.BlockSpec((tm, k), lambda i, j: (i, 0)),
                  pl.BlockSpec((tn, k), lambda i, j: (j, 0))] + side_in,
        out_specs=[pl.BlockSpec((tm, tn), lambda i, j: (i, j))] + side_out,
        out_shape=[jax.ShapeDtypeStruct((m, n), out_dtype)]
        + [jax.ShapeDtypeStruct((job.n_blocks * job.blk, job.w.shape[2]), BF16) for job in side],
        compiler_params=_cparams(("arbitrary", "arbitrary"), VMEM_LIMIT_HOST if side else VMEM_LIMIT))(
            a, bt, *[job.w for job in side])
    return res


def _mm_ws_kernel(a_ref, b_ref, o_ref, bw_ref):
    @pl.when(pl.program_id(1) == 0)
    def _():
        bw_ref[...] = b_ref[...].astype(BF16)

    o_ref[...] = jnp.dot(a_ref[...], bw_ref[...], preferred_element_type=F32).astype(o_ref.dtype)


def _matmul_ws(a, b, layer, out_dtype, tm, tn):
    m, k = a.shape
    n = b.shape[2]
    return pl.pallas_call(
        _mm_ws_kernel, grid=(n // tn, m // tm),
        in_specs=[pl.BlockSpec((tm, k), lambda j, i: (i, 0)),
                  pl.BlockSpec((None, k, tn), lambda j, i: (layer, 0, j))],
        out_specs=pl.BlockSpec((tm, tn), lambda j, i: (i, j)),
        out_shape=jax.ShapeDtypeStruct((m, n), out_dtype),
        scratch_shapes=[pltpu.VMEM((k, tn), BF16)],
        compiler_params=_cparams(("arbitrary", "arbitrary")))(a, b)


def _mm_acc_kernel(a_ref, b_ref, o_ref, acc_ref):
    k = pl.program_id(2)

    @pl.when(k == 0)
    def _():
        acc_ref[...] = jnp.zeros_like(acc_ref)

    acc_ref[...] += jnp.dot(a_ref[...], b_ref[...], preferred_element_type=F32)

    @pl.when(k == pl.num_programs(2) - 1)
    def _():
        o_ref[...] = acc_ref[...].astype(o_ref.dtype)


def _matmul_kgrid(a, b, out_dtype, tm, tn, tk):
    m, k = a.shape
    n = b.shape[1]
    return pl.pallas_call(
        _mm_acc_kernel, grid=(m // tm, n // tn, k // tk),
        in_specs=[pl.BlockSpec((tm, tk), lambda i, j, kk: (i, kk)),
                  pl.BlockSpec((tk, tn), lambda i, j, kk: (kk, j))],
        out_specs=pl.BlockSpec((tm, tn), lambda i, j, kk: (i, j)),
        out_shape=jax.ShapeDtypeStruct((m, n), out_dtype),
        scratch_shapes=[pltpu.VMEM((tm, tn), F32)],
        compiler_params=_cparams(("parallel", "parallel", "arbitrary")))(a, b)


def _ffn_up_kernel(h_ref, wg_ref, wu_ref, o_ref, wgb_ref, wub_ref, *, tn, n_valid):
    j = pl.program_id(0)

    @pl.when(pl.program_id(1) == 0)
    def _():
        wgb_ref[...] = wg_ref[...].astype(BF16)
        wub_ref[...] = wu_ref[...].astype(BF16)

    h = h_ref[...]
    g = jnp.dot(h, wgb_ref[...], preferred_element_type=F32)
    u = jnp.dot(h, wub_ref[...], preferred_element_type=F32)
    col = j * tn + lax.broadcasted_iota(jnp.int32, g.shape, 1)
    o_ref[...] = jnp.where(col < n_valid, g * _sigmoid(g) * u, 0.0).astype(o_ref.dtype)


def _ffn_up(h, wg, wu, layer, tm, tn, n_out):
    m, k = h.shape
    n = wg.shape[2]
    wspec = pl.BlockSpec((None, k, tn), lambda j, i: (layer, 0, j))
    return pl.pallas_call(
        functools.partial(_ffn_up_kernel, tn=tn, n_valid=n), grid=(n_out // tn, m // tm),
        in_specs=[pl.BlockSpec((tm, k), lambda j, i: (i, 0)), wspec, wspec],
        out_specs=pl.BlockSpec((tm, tn), lambda j, i: (i, j)),
        out_shape=jax.ShapeDtypeStruct((m, n_out), BF16),
        scratch_shapes=[pltpu.VMEM((k, tn), BF16)] * 2,
        compiler_params=_cparams(("arbitrary", "arbitrary")))(h, wg, wu)


def _merge_kernel(oa_ref, ob_ref, oc_ref, w_ref, gate_ref, o_ref, acc_ref):
    b = pl.program_id(2)
    sig = _sigmoid(gate_ref[...])

    @pl.when(b == 0)
    def _():
        acc_ref[...] = sig * jnp.dot(oa_ref[...], w_ref[0], preferred_element_type=F32)

    @pl.when(b == 1)
    def _():
        acc_ref[...] += sig * jnp.dot(ob_ref[...], w_ref[0], preferred_element_type=F32)

    @pl.when(b == 2)
    def _():
        o_ref[...] = (acc_ref[...] + sig * jnp.dot(oc_ref[...], w_ref[0],
                                                   preferred_element_type=F32)).astype(o_ref.dtype)


def _merge(o_a, o_b, o_c, w_br, proj, tm, tn):
    m = o_a.shape[0]
    gate_blk0 = COL_GATE // tn
    per_branch = D_MODEL // tn
    o_spec = pl.BlockSpec((tm, BRANCH_W), lambda i, j, b: (i, 0))
    return pl.pallas_call(
        _merge_kernel, grid=(m // tm, D_MODEL // tn, 3),
        in_specs=[o_spec, o_spec, o_spec,
                  pl.BlockSpec((1, BRANCH_W, tn), lambda i, j, b: (b, 0, j)),
                  pl.BlockSpec((tm, tn), lambda i, j, b: (i, gate_blk0 + b * per_branch + j))],
        out_specs=pl.BlockSpec((tm, tn), lambda i, j, b: (i, j)),
        out_shape=jax.ShapeDtypeStruct((m, D_MODEL), BF16),
        scratch_shapes=[pltpu.VMEM((tm, tn), F32)],
        compiler_params=_cparams(("parallel", "parallel", "arbitrary")))(o_a, o_b, o_c, w_br, proj)


def _rms(x, g):
    return x * lax.rsqrt(jnp.mean(x * x, axis=-1, keepdims=True) + EPS) * g


def _two_source_specs(tr, nb_first):
    first = pl.BlockSpec((tr, D_MODEL), lambda i: (jnp.minimum(i, nb_first - 1), 0))
    second = pl.BlockSpec((tr, D_MODEL), lambda i: (jnp.maximum(i - nb_first, 0), 0))
    return first, second


def _prenorm_kernel(xa_ref, xb_ref, g_ref, h_ref, *, nb_first):
    i = pl.program_id(0)

    @pl.when(i < nb_first)
    def _():
        h_ref[...] = _rms(xa_ref[...], g_ref[...]).astype(h_ref.dtype)

    @pl.when(i >= nb_first)
    def _():
        h_ref[...] = _rms(xb_ref[...], g_ref[...]).astype(h_ref.dtype)


def _prenorm(xa, xb, g, tr):
    ma, mb = xa.shape[0], xb.shape[0]
    sa, sb = _two_source_specs(tr, ma // tr)
    return pl.pallas_call(
        functools.partial(_prenorm_kernel, nb_first=ma // tr), grid=((ma + mb) // tr,),
        in_specs=[sa, sb, pl.BlockSpec((1, D_MODEL), lambda i: (0, 0))],
        out_specs=pl.BlockSpec((tr, D_MODEL), lambda i: (i, 0)),
        out_shape=jax.ShapeDtypeStruct((ma + mb, D_MODEL), BF16),
        compiler_params=_cparams(("arbitrary",)))(xa, xb, g.reshape(1, D_MODEL))


def _post_kernel(*refs, nb_first, n_src, want_h):
    x_refs = refs[:n_src]
    y_ref, gp_ref, gn_ref, xo_ref = refs[n_src:n_src + 4]
    r = _rms(y_ref[...].astype(F32), gp_ref[...])

    def finish(x_ref):
        xn = x_ref[...] + r
        xo_ref[...] = xn
        if want_h:
            refs[n_src + 4][...] = _rms(xn, gn_ref[...]).astype(BF16)

    if n_src == 1:
        finish(x_refs[0])
    else:
        i = pl.program_id(0)
        pl.when(i < nb_first)(lambda: finish(x_refs[0]))
        pl.when(i >= nb_first)(lambda: finish(x_refs[1]))


def _post(xs, y, g_post, g_next, tr):
    m = y.shape[0]
    want_h = g_next is not None
    row = pl.BlockSpec((tr, D_MODEL), lambda i: (i, 0))
    vec = pl.BlockSpec((1, D_MODEL), lambda i: (0, 0))
    nb_first = xs[0].shape[0] // tr
    x_specs = list(_two_source_specs(tr, nb_first)) if len(xs) == 2 else [row]
    out_shape = [jax.ShapeDtypeStruct((m, D_MODEL), F32)]
    out_specs = [row]
    if want_h:
        out_shape.append(jax.ShapeDtypeStruct((m, D_MODEL), BF16))
        out_specs.append(row)
    gn = (g_next if want_h else g_post).reshape(1, D_MODEL)
    res = pl.pallas_call(
        functools.partial(_post_kernel, nb_first=nb_first, n_src=len(xs), want_h=want_h), grid=(m // tr,),
        in_specs=x_specs + [row, vec, vec], out_specs=out_specs, out_shape=out_shape,
        compiler_params=_cparams(("arbitrary",)))(*xs, y, g_post.reshape(1, D_MODEL), gn)
    return (res[0], res[1]) if want_h else (res[0], None)


def _post_split_kernel(x_ref, y_ref, gp_ref, oa_ref, ob_ref, *, nb_first):
    i = pl.program_id(0)
    xn = x_ref[...] + _rms(y_ref[...].astype(F32), gp_ref[...])

    @pl.when(i < nb_first)
    def _():
        oa_ref[...] = xn

    @pl.when(i >= nb_first)
    def _():
        ob_ref[...] = xn


def _post_split(x, y, g_post, rows_first, tr):
    m = x.shape[0]
    row = pl.BlockSpec((tr, D_MODEL), lambda i: (i, 0))
    vec = pl.BlockSpec((1, D_MODEL), lambda i: (0, 0))
    oa, ob = _two_source_specs(tr, rows_first // tr)
    return pl.pallas_call(
        functools.partial(_post_split_kernel, nb_first=rows_first // tr), grid=(m // tr,),
        in_specs=[row, row, vec], out_specs=[oa, ob],
        out_shape=[jax.ShapeDtypeStruct((rows_first, D_MODEL), F32),
                   jax.ShapeDtypeStruct((m - rows_first, D_MODEL), F32)],
        compiler_params=_cparams(("arbitrary",)))(x, y, g_post.reshape(1, D_MODEL))


def _gdn_prep_kernel(a_ref, b_ref, alog_ref, dt_ref, gc_ref, beta_ref, *, prompt_blocks):
    i = pl.program_id(0)
    x = a_ref[...] + dt_ref[...]
    softplus = jnp.maximum(x, 0.0) + jnp.log1p(jnp.exp(-jnp.abs(x)))
    g = -jnp.exp(alog_ref[...]) * softplus
    chunk = jnp.where(i < prompt_blocks, MIX_CHUNK, 8)
    rmod = lax.broadcasted_iota(jnp.int32, g.shape, 0) & (chunk - 1)
    s = 1
    while s < MIX_CHUNK:
        g = g + jnp.where(rmod >= s, pltpu.roll(g, s, axis=0), 0.0)
        s *= 2
    gc_ref[...] = g
    beta_ref[...] = jax.nn.sigmoid(b_ref[...])


def _gdn_prep(ab, a_log, dt_bias, tr, prompt_rows):
    m = ab.shape[0]
    pad = lambda v: jnp.pad(v.astype(F32), (0, 128 - H_A)).reshape(1, 128)
    blk = lambda c: pl.BlockSpec((tr, 128), lambda i: (i, c))
    vec = pl.BlockSpec((1, 128), lambda i: (0, 0))
    return pl.pallas_call(
        functools.partial(_gdn_prep_kernel, prompt_blocks=prompt_rows // tr), grid=(m // tr,),
        in_specs=[blk(0), blk(1), vec, vec], out_specs=[blk(0), blk(0)],
        out_shape=[jax.ShapeDtypeStruct((m, 128), F32)] * 2,
        compiler_params=_cparams(("parallel",)))(ab, ab, pad(a_log), pad(dt_bias))


def _gdn_kernel(x_ref, z_ref, gcol_ref, bcol_ref, grow_ref, s0_ref, buf_ref, cw_ref, ng_ref, stack_ref, obuf_ref,
                o_ref, s_ref, tail_ref, xc_ref, *, c, group, nseq):
    @pl.when(pl.program_id(1) == 0)
    def _():
        s_ref[...] = s0_ref[...]
        xc_ref[:, :8, :] = buf_ref[...]

    for q in range(nseq):
        xc_ref[q, 8:, :] = x_ref[q * c:(q + 1) * c, :]
        tail_ref[q] = x_ref[(q + 1) * c - 8:(q + 1) * c, :]

    def conv(q, lo):
        xcol = xc_ref[q, :, lo:lo + 128]
        acc = xcol[8:] * cw_ref[CONV_W - 1:CONV_W, lo:lo + 128]
        for s in range(1, CONV_W):
            acc = acc + pltpu.roll(xcol, s, axis=0)[8:] * cw_ref[CONV_W - 1 - s:CONV_W - s, lo:lo + 128]
        return jax.nn.silu(acc)

    def l2n(t):
        return t * lax.rsqrt(jnp.sum(t * t, axis=-1, keepdims=True) + EPS)

    ii = lax.broadcasted_iota(jnp.int32, (c, c), 0)
    jj = lax.broadcasted_iota(jnp.int32, (c, c), 1)
    tri = ii >= jj
    strict = ii > jj
    ng = ng_ref[...]
    hk = H_A * DK_A
    for h0 in range(0, H_A, group):
        st = []
        for q in range(nseq):
            rows = slice(q * c, (q + 1) * c)
            for h in range(h0, h0 + group):
                lo = h * DK_A
                qh = l2n(conv(q, lo)) * (DK_A ** -0.5)
                k = l2n(conv(q, hk + lo))
                v = conv(q, 2 * hk + lo)
                gc = gcol_ref[rows, h:h + 1]
                beta = bcol_ref[rows, h:h + 1]
                gr = grow_ref[q, 0, h:h + 1, :]
                glast = gr[:, c - 1:c]
                decay = jnp.exp(jnp.where(tri, gc - gr, -jnp.inf))
                egc = jnp.exp(gc)
                kb = k * beta
                m = lax.dot_general(jnp.concatenate([kb, qh], axis=0).astype(BF16), k.astype(BF16),
                                    (((1,), (1,)), ((), ())), preferred_element_type=F32)
                s = s_ref[q, h]
                sq = _bdot(jnp.concatenate([kb * egc, qh * egc], axis=0), s)
                st.append(dict(q=q, h=h, rows=rows, m=m, sq=sq, s=s, decay=decay, vb=v * beta,
                               kd=(k * jnp.exp(glast - gc)).astype(BF16), cd=jnp.exp(glast)))
        for d in st:
            d["p"] = (-jnp.where(strict, d["m"][:c] * d["decay"], 0.0)).astype(BF16)
            d["attn"] = (d["m"][c:] * d["decay"]).astype(BF16)
            d["x"] = d["vb"] - d["sq"][:c]
        n_sq = 1
        while n_sq < c:
            last = 2 * n_sq >= c
            for d in st:
                d["x"] = d["x"] + jnp.dot(d["p"], d["x"].astype(BF16), preferred_element_type=F32)
            if not last:
                for d in st:
                    d["p"] = jnp.dot(d["p"], d["p"], preferred_element_type=F32).astype(BF16)
            n_sq *= 2
        outs = {}
        for d in st:
            xb = d["x"].astype(BF16)
            o = d["sq"][c:] + jnp.dot(d["attn"], xb, preferred_element_type=F32)
            s_ref[d["q"], d["h"]] = d["s"] * d["cd"] + lax.dot_general(
                d["kd"], xb, (((0,), (0,)), ((), ())), preferred_element_type=F32)
            lo = d["h"] * DV_A
            o = o * lax.rsqrt(jnp.mean(o * o, axis=-1, keepdims=True) + EPS) * ng
            outs[d["q"], d["h"]] = o * jax.nn.silu(z_ref[d["rows"], lo:lo + DV_A])
        for h in range(h0, h0 + group):
            o = outs[0, h] if nseq == 1 else jnp.concatenate([outs[q, h] for q in range(nseq)], axis=0)
            o_ref[:, h * DV_A:(h + 1) * DV_A] = o.astype(o_ref.dtype)
    xc_ref[:, :8, :] = xc_ref[:, c:c + 8, :]


def _gdn(proj, gc, beta, grow, s0, layer, stack, layer_out, o_buf, buf8, conv_w, norm_g, row_off, n, l, c,
         group, nseq):
    nc = l // c
    assert nseq == 1 or nc == 1
    blk = nseq * c
    base = row_off // blk
    rows = lambda w, col: pl.BlockSpec((blk, w), lambda i, j: (base + i * nc + j, col))
    return pl.pallas_call(
        functools.partial(_gdn_kernel, c=c, group=group, nseq=nseq), grid=(n // nseq, nc),
        in_specs=[rows(CONV_DIM, 0), rows(BRANCH_W, COL_Z // BRANCH_W), rows(128, 0), rows(128, 0),
                  pl.BlockSpec((nseq, 1, H_A, c), lambda i, j: (i, j, 0, 0)),
                  pl.BlockSpec((None, nseq, H_A, DK_A, DV_A), lambda i, j: (layer, i, 0, 0, 0)),
                  pl.BlockSpec((nseq, 8, CONV_DIM), lambda i, j: (i, 0, 0)),
                  pl.BlockSpec((CONV_W, CONV_DIM), lambda i, j: (0, 0)),
                  pl.BlockSpec((1, DV_A), lambda i, j: (0, 0)),
                  pl.BlockSpec(memory_space=pl.ANY), pl.BlockSpec(memory_space=pl.ANY)],
        out_specs=[rows(BRANCH_W, 0),
                   pl.BlockSpec((None, nseq, H_A, DK_A, DV_A), lambda i, j: (layer_out, i, 0, 0, 0)),
                   pl.BlockSpec((nseq, 8, CONV_DIM), lambda i, j: (i, 0, 0))],
        out_shape=[jax.ShapeDtypeStruct(o_buf.shape, o_buf.dtype),
                   jax.ShapeDtypeStruct(stack.shape, F32),
                   jax.ShapeDtypeStruct((n, 8, CONV_DIM), F32)],
        scratch_shapes=[pltpu.VMEM((nseq, c + 8, CONV_DIM), F32)],
        input_output_aliases={9: 1, 10: 0},
        compiler_params=_cparams(("parallel", "arbitrary")))(
            proj, proj, gc, beta, grow, s0, buf8, conv_w, norm_g.reshape(1, DV_A), stack, o_buf)


def _ret_kernel(qk_ref, v_ref, gate_ref, cos_ref, sin_ref, r0_ref, stack_ref, obuf_ref, o_ref, r_ref, *, c, nseq):
    @pl.when(pl.program_id(1) == 0)
    def _():
        r_ref[...] = r0_ref[...]

    ii = lax.broadcasted_iota(jnp.int32, (c, c), 0)
    jj = lax.broadcasted_iota(jnp.int32, (c, c), 1)
    rel = (ii - jj).astype(F32)
    idx = lax.broadcasted_iota(jnp.int32, (c, 1), 0).astype(F32)
    even = (lax.broadcasted_iota(jnp.int32, (c, DK_B), 1) & 1) == 0
    cos = cos_ref[...]
    sin = sin_ref[...]

    def rot(x):
        swapped = jnp.where(even, pltpu.roll(x, DK_B - 1, axis=1), pltpu.roll(x, 1, axis=1))
        return x * cos + swapped * sin

    units = [(q, h) for q in range(nseq) for h in range(H_B)]
    lg = {h: math.log1p(-2.0 ** (-5.0 - h)) for h in range(H_B)}
    rows = {q: slice(q * c, (q + 1) * c) for q in range(nseq)}
    qs = {u: rot(qk_ref[rows[u[0]], u[1] * DK_B:(u[1] + 1) * DK_B]) for u in units}
    ks = {u: rot(qk_ref[rows[u[0]], (H_B + u[1]) * DK_B:(H_B + u[1] + 1) * DK_B]) * (DK_B ** -0.5)
          for u in units}
    vs = {u: v_ref[rows[u[0]], u[1] * DV_B:(u[1] + 1) * DV_B].astype(BF16) for u in units}
    rs = {u: r_ref[u] for u in units}
    scores = {u: _bdot_nt(qs[u], ks[u]) for u in units}
    cross = {u: _bdot(qs[u] * jnp.exp(lg[u[1]] * (idx + 1.0)), rs[u]) for u in units}
    for u in units:
        r_ref[u] = rs[u] * math.exp(lg[u[1]] * c) + _bdot_tn(ks[u] * jnp.exp(lg[u[1]] * (c - 1.0 - idx)), vs[u])
    inners = {u: _bdot(scores[u] * jnp.where(rel >= 0, jnp.exp(lg[u[1]] * jnp.maximum(rel, 0.0)), 0.0), vs[u])
              for u in units}
    outs = {}
    for u in units:
        o = inners[u] + cross[u]
        mu = jnp.mean(o, axis=-1, keepdims=True)
        d = o - mu
        var = jnp.mean(d * d, axis=-1, keepdims=True)
        o = d * lax.rsqrt(var + EPS)
        outs[u] = jax.nn.silu(gate_ref[rows[u[0]], u[1] * DV_B:(u[1] + 1) * DV_B]) * o
    for h in range(H_B):
        o = outs[0, h] if nseq == 1 else jnp.concatenate([outs[q, h] for q in range(nseq)], axis=0)
        o_ref[:, h * DV_B:(h + 1) * DV_B] = o.astype(o_ref.dtype)


def _rope_tables(pos):
    half = DK_B // 2
    inv_freq = 1.0 / (ROPE_BASE ** jnp.linspace(0.0, 1.0, half, dtype=F32))
    ang = pos.astype(F32)[:, None] * inv_freq[None, :]
    sin = jnp.sin(ang)
    cos = jnp.cos(ang)
    cos2 = jnp.stack([cos, cos], axis=-1).reshape(-1, DK_B)
    sin2 = jnp.stack([-sin, sin], axis=-1).reshape(-1, DK_B)
    return cos2, sin2


def _retention(proj, cos2, sin2, r0, layer, stack, layer_out, o_buf, row_off, n, l, c, nseq):
    nc = l // c
    assert nseq == 1 or nc == 1
    blk = nseq * c
    base = row_off // blk
    col = lambda off: pl.BlockSpec((blk, BRANCH_W), lambda i, j: (base + i * nc + j, off // BRANCH_W))
    tab = pl.BlockSpec((c, DK_B), lambda i, j: (j, 0))
    return pl.pallas_call(
        functools.partial(_ret_kernel, c=c, nseq=nseq), grid=(n // nseq, nc),
        in_specs=[col(COL_QKB), col(COL_VB), col(COL_GB), tab, tab,
                  pl.BlockSpec((None, nseq, H_B, DK_B, DV_B), lambda i, j: (layer, i, 0, 0, 0)),
                  pl.BlockSpec(memory_space=pl.ANY), pl.BlockSpec(memory_space=pl.ANY)],
        out_specs=[col(0),
                   pl.BlockSpec((None, nseq, H_B, DK_B, DV_B), lambda i, j: (layer_out, i, 0, 0, 0))],
        out_shape=[jax.ShapeDtypeStruct(o_buf.shape, o_buf.dtype),
                   jax.ShapeDtypeStruct(stack.shape, F32)],
        input_output_aliases={6: 1, 7: 0},
        compiler_params=_cparams(("parallel", "arbitrary")))(proj, proj, proj, cos2, sin2, r0, stack, o_buf)


def _gelu(x):
    return 0.5 * x * (1.0 + lax.erf(x * (2.0 ** -0.5)))


def _gmlp_kernel(u_ref, v_ref, w_ref, bcol_ref, lng_ref, lnb_ref, obuf_ref, o_ref, *vrows_ref, seq):
    u32 = _gelu(u_ref[...])
    v32 = _gelu(v_ref[...])
    mu = jnp.mean(v32, axis=-1, keepdims=True)
    d = v32 - mu
    var = jnp.mean(d * d, axis=-1, keepdims=True)
    vn = d * lax.rsqrt(var + EPS) * lng_ref[...] + lnb_ref[...]
    if vrows_ref:
        vrows_ref[0][...] = vn
    ii = lax.broadcasted_iota(jnp.int32, (CHUNK_C, CHUNK_C), 0)
    jj = lax.broadcasted_iota(jnp.int32, (CHUNK_C, CHUNK_C), 1)
    mask = (ii >= jj) & ((ii & -seq) == (jj & -seq))
    for g in range(G_C):
        w = jnp.where(mask, w_ref[g], 0.0)
        mixed = _bdot(w, vn[:, g * DG_C:(g + 1) * DG_C]) + bcol_ref[:, g:g + 1]
        o_ref[:, g * DG_C:(g + 1) * DG_C] = (u32[:, g * DG_C:(g + 1) * DG_C] * mixed).astype(o_ref.dtype)


def _gmlp(proj, w_tile, b_col, ln_g, ln_b, o_buf, row_off, rows, seq, want_vrows):
    base = row_off // CHUNK_C
    col = lambda off: pl.BlockSpec((CHUNK_C, BRANCH_W), lambda i: (base + i, off // BRANCH_W))
    vec = pl.BlockSpec((1, BRANCH_W), lambda i: (0, 0))
    out_shape = [jax.ShapeDtypeStruct(o_buf.shape, o_buf.dtype)]
    out_specs = [col(0)]
    if want_vrows:
        out_shape.append(jax.ShapeDtypeStruct((rows, BRANCH_W), F32))
        out_specs.append(pl.BlockSpec((CHUNK_C, BRANCH_W), lambda i: (i, 0)))
    res = pl.pallas_call(
        functools.partial(_gmlp_kernel, seq=seq), grid=(rows // CHUNK_C,),
        in_specs=[col(COL_U), col(COL_V),
                  pl.BlockSpec((G_C, CHUNK_C, CHUNK_C), lambda i: (0, 0, 0)),
                  pl.BlockSpec((CHUNK_C, G_C), lambda i: (0, 0)), vec, vec,
                  pl.BlockSpec(memory_space=pl.ANY)],
        out_specs=out_specs, out_shape=out_shape, input_output_aliases={6: 0},
        compiler_params=_cparams(("parallel",)))(
            proj, proj, w_tile, b_col, ln_g.reshape(1, BRANCH_W), ln_b.reshape(1, BRANCH_W), o_buf)
    return res if want_vrows else (res[0], None)


AB_COL0 = 4 * BRANCH_W
AB_COLS = 2 * H_A
WD_BLK, WBR_BLK, PACK_BLK = 64, 32, 128


def _pack_kernel(w_ref, o_ref):
    o_ref[...] = w_ref[0].astype(BF16)


def _pack_src_row(j, tn):
    r = j * tn
    return r + jnp.where(r >= AB_COL0, AB_COLS, 0)


def _pack_w_in_t(w_in_t, layer, tn):
    k = w_in_t.shape[2]
    return pl.pallas_call(
        _pack_kernel, grid=(N_PROJ // tn,),
        in_specs=[pl.BlockSpec((pl.Element(1), pl.Element(tn), pl.Element(k)),
                               lambda j: (layer, pl.multiple_of(_pack_src_row(j, tn), 8), 0))],
        out_specs=pl.BlockSpec((tn, k), lambda j: (j, 0)),
        out_shape=jax.ShapeDtypeStruct((N_PROJ, k), BF16),
        compiler_params=_cparams(("parallel",)))(w_in_t)


def _ab_weights_t(w_in_t, layer):
    ab = w_in_t[layer, AB_COL0:AB_COL0 + AB_COLS, :]
    zeros = jnp.zeros((128 - H_A, ab.shape[1]), F32)
    return jnp.concatenate([ab[:H_A], zeros, ab[H_A:], zeros], axis=0)


def kernel(x_prompt, x_sample, state_gdn, state_conv, state_ret, w_in, conv_w, a_log, dt_bias, gdn_norm, w_s, b_s, ln_c_g, ln_c_b, w_br, w_o, g_pre_mix, g_post_mix, g_pre_ffn, g_post_ffn, w_ffn_gate, w_ffn_up, w_ffn_down):
    n_p, l_p, _ = x_prompt.shape
    n_s, l_s, _ = x_sample.shape
    depth = w_in.shape[0]
    rows_p, rows_s = n_p * l_p, n_s * l_s
    m = rows_p + rows_s
    xs = (x_prompt.reshape(rows_p, D_MODEL), x_sample.reshape(rows_s, D_MODEL))
    tr = 256

    cos_p, sin_p = _rope_tables(jnp.arange(l_p))
    cos_s, sin_s = _rope_tables(PAST_LEN + jnp.arange(l_s))
    zero_gdn = jnp.zeros((1, n_p, H_A, DK_A, DV_A), F32)
    zero_ret = jnp.zeros((1, n_p, H_B, DK_B, DV_B), F32)
    zero_buf8 = jnp.zeros((n_p, 8, CONV_DIM), F32)
    reps = CHUNK_C // l_s

    w_in_t = jnp.swapaxes(w_in, 1, 2)

    gdn_p = jnp.zeros((depth, n_p, H_A, DK_A, DV_A), F32)
    gdn_s = jnp.zeros((depth, n_s, H_A, DK_A, DV_A), F32)
    ret_p = jnp.zeros((depth, n_p, H_B, DK_B, DV_B), F32)
    ret_s = jnp.zeros((depth, n_s, H_B, DK_B, DV_B), F32)
    outs = {k: [] for k in ("conv_p", "conv_s", "vrows")}
    h = _prenorm(xs[0], xs[1], g_pre_mix[0], tr)
    w_br_rows = w_br.reshape(depth, 3 * BRANCH_W, D_MODEL)
    w_main = _pack_w_in_t(w_in_t, 0, 512)
    for l in range(depth):
        side = [_SideCast(w_ffn_down, l, WD_BLK, D_FF_PAD // WD_BLK,
                          lambda b: jnp.minimum(b, D_FF // WD_BLK - 1) * WD_BLK, D_FF // WD_BLK),
                _SideCast(w_br_rows, l, WBR_BLK, 3 * BRANCH_W // WBR_BLK, lambda b: b * WBR_BLK,
                          3 * BRANCH_W // WBR_BLK)]
        if l + 1 < depth:
            side.append(_SideCast(w_in_t, l + 1, PACK_BLK, N_PROJ // PACK_BLK,
                                  lambda b: _pack_src_row(b, PACK_BLK), N_PROJ // PACK_BLK))
        proj, wd, w_br_l, *w_main_next = _matmul_nt(h, w_main, F32, 1024, 1024, side)
        w_main = w_main_next[0] if w_main_next else None
        ab = _matmul_nt(h, _ab_weights_t(w_in_t, l), F32, 1024, 256)[0]

        gc, beta = _gdn_prep(ab, a_log[l], dt_bias[l], 1024, rows_p)
        grow_p = gc[:rows_p, :H_A].reshape(n_p, l_p // MIX_CHUNK, MIX_CHUNK, H_A).transpose(0, 1, 3, 2)
        grow_s = gc[rows_p:, :H_A].reshape(n_s, 1, l_s, H_A).transpose(0, 1, 3, 2)
        buf8_s = jnp.pad(state_conv[l], ((0, 0), (8 - (CONV_W - 1), 0), (0, 0)))
        o_a = jnp.zeros((m, BRANCH_W), BF16)
        o_b = jnp.zeros((m, BRANCH_W), BF16)
        o_c = jnp.zeros((m, BRANCH_W), BF16)
        o_a, gdn_p, tail_p = _gdn(proj, gc, beta, grow_p, zero_gdn, 0, gdn_p, l, o_a, zero_buf8, conv_w[l],
                                  gdn_norm[l], 0, n_p, l_p, MIX_CHUNK, H_A, 1)
        o_a, gdn_s, tail_s = _gdn(proj, gc, beta, grow_s, state_gdn, l, gdn_s, l, o_a, buf8_s, conv_w[l],
                                  gdn_norm[l], rows_p, n_s, l_s, l_s, GDN_GROUP, GDN_SAMPLE_SEQS)
        outs["conv_p"].append(tail_p[:, 8 - (CONV_W - 1):])
        outs["conv_s"].append(tail_s[:, 8 - (CONV_W - 1):])

        o_b, ret_p = _retention(proj, cos_p, sin_p, zero_ret, 0, ret_p, l, o_b, 0, n_p, l_p, RET_CHUNK, 1)
        o_b, ret_s = _retention(proj, cos_s, sin_s, state_ret, l, ret_s, l, o_b, rows_p, n_s, l_s, l_s,
                                RET_SAMPLE_SEQS)

        o_c, _ = _gmlp(proj, w_s[l], b_s[l].T, ln_c_g[l], ln_c_b[l], o_c, 0, rows_p, CHUNK_C, False)
        w_tile_s = jnp.tile(w_s[l][:, :l_s, :l_s], (1, reps, reps))
        b_col_s = jnp.tile(b_s[l][:, :l_s].T, (reps, 1))
        o_c, vrows = _gmlp(proj, w_tile_s, b_col_s, ln_c_g[l], ln_c_b[l], o_c, rows_p, rows_s, l_s, True)
        outs["vrows"].append(vrows.reshape(n_s, l_s, BRANCH_W))

        merged = _merge(o_a, o_b, o_c, w_br_l.reshape(3, BRANCH_W, D_MODEL), proj, 1024, 1024)
        mix = _matmul_ws(merged, w_o, l, BF16, 1024, 512)
        x, h = _post(xs, mix, g_post_mix[l], g_pre_ffn[l], tr)

        f1 = _ffn_up(h, w_ffn_gate, w_ffn_up, l, 512, 512, D_FF_PAD)
        f = _matmul_kgrid(f1, wd, BF16, 1024, 1024, D_FF_PAD // 4)
        if l + 1 < depth:
            x, h = _post((x,), f, g_post_ffn[l], g_pre_mix[l + 1], tr)
            xs = (x,)
        else:
            y_p, y_s = _post_split(x, f, g_post_ffn[l], rows_p, tr)

    y_p = y_p.reshape(n_p, l_p, D_MODEL)
    y_s = y_s.reshape(n_s, l_s, D_MODEL)
    st = lambda k: jnp.stack(outs[k])
    return (y_p, y_s, gdn_p, gdn_s, st("conv_p"), st("conv_s"), ret_p, ret_s, st("vrows"))
```

```python
import functools
import math
from typing import Callable, NamedTuple

import jax
import jax.numpy as jnp
from jax import lax
from jax.experimental import pallas as pl
from jax.experimental.pallas import tpu as pltpu

F32 = jnp.float32
BF16 = jnp.bfloat16

D_MODEL = 4096
BRANCH_W = 2048
H_A, DK_A, DV_A = 16, 128, 128
CONV_W = 4
CONV_DIM = 6144
H_B, DK_B, DV_B = 8, 128, 256
G_C, DG_C, CHUNK_C = 8, 256, 128
D_FF = 11008
D_FF_PAD = 11264
EPS = 1e-6
ROPE_BASE = 10000.0
PAST_LEN = 16384
MIX_CHUNK = 64
RET_CHUNK = 128
GDN_GROUP = 8
GDN_SAMPLE_SEQS = 2
RET_SAMPLE_SEQS = 4

COL_QKV, COL_Z, COL_QKB, COL_VB, COL_GB, COL_U, COL_V, COL_GATE = (
    0, 6144, 8192, 10240, 12288, 14336, 16384, 18432)
N_PROJ = 30720
VMEM_LIMIT = 56 * 1024 * 1024
VMEM_LIMIT_HOST = 60 * 1024 * 1024


def _cparams(sem, vmem=VMEM_LIMIT):
    return pltpu.CompilerParams(dimension_semantics=sem, vmem_limit_bytes=vmem)


def _sigmoid(x):
    return 0.5 * jnp.tanh(0.5 * x) + 0.5


def _bdot(a, b):
    return jnp.dot(a.astype(BF16), b.astype(BF16), preferred_element_type=F32)


def _bdot_nt(a, b):
    return lax.dot_general(a.astype(BF16), b.astype(BF16), (((1,), (1,)), ((), ())),
                           preferred_element_type=F32)


def _bdot_tn(a, b):
    return lax.dot_general(a.astype(BF16), b.astype(BF16), (((0,), (0,)), ((), ())),
                           preferred_element_type=F32)


class _SideCast(NamedTuple):
    w: jax.Array
    layer: int
    blk: int
    n_blocks: int
    src_row: Callable
    n_copy: int


def _mm_nt_kernel(a_ref, bt_ref, *refs, side, steps_per_row):
    n_side = len(side)
    o_ref = refs[n_side]
    o_ref[...] = lax.dot_general(a_ref[...], bt_ref[...].astype(BF16), (((1,), (1,)), ((), ())),
                                 preferred_element_type=F32).astype(o_ref.dtype)
    t = pl.program_id(0) * steps_per_row + pl.program_id(1)
    for job, w_ref, so_ref in zip(side, refs[:n_side], refs[n_side + 1:]):
        v = w_ref[0].astype(BF16)
        so_ref[...] = v if job.n_copy >= job.n_blocks else jnp.where(t < job.n_copy, v, jnp.zeros_like(v))


def _matmul_nt(a, bt, out_dtype, tm, tn, side=()):
    m, k = a.shape
    n = bt.shape[0]
    nj = n // tn
    assert all(job.n_blocks <= (m // tm) * nj for job in side)

    def blk_idx(job, i, j):
        return jnp.minimum(i * nj + j, job.n_blocks - 1)

    side_in = [pl.BlockSpec((pl.Element(1), pl.Element(job.blk), pl.Element(job.w.shape[2])),
                            lambda i, j, job=job: (job.layer, pl.multiple_of(job.src_row(blk_idx(job, i, j)), 8), 0))
               for job in side]
    side_out = [pl.BlockSpec((job.blk, job.w.shape[2]), lambda i, j, job=job: (blk_idx(job, i, j), 0))
                for job in side]
    res = pl.pallas_call(
        functools.partial(_mm_nt_kernel, side=tuple(side), steps_per_row=nj), grid=(m // tm, nj),
        in_specs=[pl.BlockSpec((tm, k), lambda i, j: (i, 0)),
                  pl.BlockSpec((tn, k), lambda i, j: (j, 0))] + side_in,
        out_specs=[pl.BlockSpec((tm, tn), lambda i, j: (i, j))] + side_out,
        out_shape=[jax.ShapeDtypeStruct((m, n), out_dtype)]
        + [jax.ShapeDtypeStruct((job.n_blocks * job.blk, job.w.shape[2]), BF16) for job in side],
        compiler_params=_cparams(("arbitrary", "arbitrary"), VMEM_LIMIT_HOST if side else VMEM_LIMIT))(
            a, bt, *[job.w for job in side])
    return res


def _mm_ws_kernel(a_ref, b_ref, o_ref, bw_ref):
    @pl.when(pl.program_id(1) == 0)
    def _():
        bw_ref[...] = b_ref[...].astype(BF16)

    o_ref[...] = jnp.dot(a_ref[...], bw_ref[...], preferred_element_type=F32).astype(o_ref.dtype)


def _matmul_ws(a, b, layer, out_dtype, tm, tn):
    m, k = a.shape
    n = b.shape[2]
    return pl.pallas_call(
        _mm_ws_kernel, grid=(n // tn, m // tm),
        in_specs=[pl.BlockSpec((tm, k), lambda j, i: (i, 0)),
                  pl.BlockSpec((None, k, tn), lambda j, i: (layer, 0, j))],
        out_specs=pl.BlockSpec((tm, tn), lambda j, i: (i, j)),
        out_shape=jax.ShapeDtypeStruct((m, n), out_dtype),
        scratch_shapes=[pltpu.VMEM((k, tn), BF16)],
        compiler_params=_cparams(("arbitrary", "arbitrary")))(a, b)


def _mm_acc_kernel(a_ref, b_ref, o_ref, acc_ref):
    k = pl.program_id(2)
    last = pl.num_programs(2) - 1

    @pl.when(k == 0)
    def _():
        acc_ref[...] = jnp.dot(a_ref[...], b_ref[...], preferred_element_type=F32)

    @pl.when((k > 0) & (k < last))
    def _():
        acc_ref[...] += jnp.dot(a_ref[...], b_ref[...], preferred_element_type=F32)

    @pl.when(k == last)
    def _():
        o_ref[...] = (acc_ref[...] + jnp.dot(a_ref[...], b_ref[...], preferred_element_type=F32)).astype(o_ref.dtype)


def _matmul_kgrid(a, b, out_dtype, tm, tn, tk):
    m, k = a.shape
    n = b.shape[1]
    assert k // tk >= 2
    return pl.pallas_call(
        _mm_acc_kernel, grid=(m // tm, n // tn, k // tk),
        in_specs=[pl.BlockSpec((tm, tk), lambda i, j, kk: (i, kk)),
                  pl.BlockSpec((tk, tn), lambda i, j, kk: (kk, j))],
        out_specs=pl.BlockSpec((tm, tn), lambda i, j, kk: (i, j)),
        out_shape=jax.ShapeDtypeStruct((m, n), out_dtype),
        scratch_shapes=[pltpu.VMEM((tm, tn), F32)],
        compiler_params=_cparams(("parallel", "parallel", "arbitrary")))(a, b)


def _ffn_up_kernel(h_ref, wg_ref, wu_ref, o_ref, wgb_ref, wub_ref, *, tn, n_valid):
    j = pl.program_id(0)

    @pl.when(pl.program_id(1) == 0)
    def _():
        wgb_ref[...] = wg_ref[...].astype(BF16)
        wub_ref[...] = wu_ref[...].astype(BF16)

    h = h_ref[...]
    g = jnp.dot(h, wgb_ref[...], preferred_element_type=F32)
    u = jnp.dot(h, wub_ref[...], preferred_element_type=F32)
    col = j * tn + lax.broadcasted_iota(jnp.int32, g.shape, 1)
    o_ref[...] = jnp.where(col < n_valid, g * _sigmoid(g) * u, 0.0).astype(o_ref.dtype)


def _ffn_up(h, wg, wu, layer, tm, tn, n_out):
    m, k = h.shape
    n = wg.shape[2]
    wspec = pl.BlockSpec((None, k, tn), lambda j, i: (layer, 0, j))
    return pl.pallas_call(
        functools.partial(_ffn_up_kernel, tn=tn, n_valid=n), grid=(n_out // tn, m // tm),
        in_specs=[pl.BlockSpec((tm, k), lambda j, i: (i, 0)), wspec, wspec],
        out_specs=pl.BlockSpec((tm, tn), lambda j, i: (i, j)),
        out_shape=jax.ShapeDtypeStruct((m, n_out), BF16),
        scratch_shapes=[pltpu.VMEM((k, tn), BF16)] * 2,
        compiler_params=_cparams(("arbitrary", "arbitrary")))(h, wg, wu)


def _merge_kernel(oa_ref, ob_ref, oc_ref, w_ref, ga_ref, gb_ref, gc_ref, o_ref):
    acc = _sigmoid(ga_ref[...]) * jnp.dot(oa_ref[...], w_ref[0], preferred_element_type=F32)
    acc = acc + _sigmoid(gb_ref[...]) * jnp.dot(ob_ref[...], w_ref[1], preferred_element_type=F32)
    acc = acc + _sigmoid(gc_ref[...]) * jnp.dot(oc_ref[...], w_ref[2], preferred_element_type=F32)
    o_ref[...] = acc.astype(o_ref.dtype)


def _merge(o_a, o_b, o_c, w_br, proj, tm, tn):
    m = o_a.shape[0]
    gate_blk0 = COL_GATE // tn
    per_branch = D_MODEL // tn
    o_spec = pl.BlockSpec((tm, BRANCH_W), lambda i, j: (i, 0))
    gate = lambda b: pl.BlockSpec((tm, tn), lambda i, j: (i, gate_blk0 + b * per_branch + j))
    return pl.pallas_call(
        _merge_kernel, grid=(m // tm, D_MODEL // tn),
        in_specs=[o_spec, o_spec, o_spec,
                  pl.BlockSpec((3, BRANCH_W, tn), lambda i, j: (0, 0, j)),
                  gate(0), gate(1), gate(2)],
        out_specs=pl.BlockSpec((tm, tn), lambda i, j: (i, j)),
        out_shape=jax.ShapeDtypeStruct((m, D_MODEL), BF16),
        compiler_params=_cparams(("parallel", "parallel")))(o_a, o_b, o_c, w_br, proj, proj, proj)


def _rms(x, g):
    return x * lax.rsqrt(jnp.mean(x * x, axis=-1, keepdims=True) + EPS) * g


def _two_source_specs(tr, nb_first):
    first = pl.BlockSpec((tr, D_MODEL), lambda i: (jnp.minimum(i, nb_first - 1), 0))
    second = pl.BlockSpec((tr, D_MODEL), lambda i: (jnp.maximum(i - nb_first, 0), 0))
    return first, second


def _prenorm_kernel(xa_ref, xb_ref, g_ref, h_ref, *, nb_first):
    i = pl.program_id(0)

    @pl.when(i < nb_first)
    def _():
        h_ref[...] = _rms(xa_ref[...], g_ref[...]).astype(h_ref.dtype)

    @pl.when(i >= nb_first)
    def _():
        h_ref[...] = _rms(xb_ref[...], g_ref[...]).astype(h_ref.dtype)


def _prenorm(xa, xb, g, tr):
    ma, mb = xa.shape[0], xb.shape[0]
    sa, sb = _two_source_specs(tr, ma // tr)
    return pl.pallas_call(
        functools.partial(_prenorm_kernel, nb_first=ma // tr), grid=((ma + mb) // tr,),
        in_specs=[sa, sb, pl.BlockSpec((1, D_MODEL), lambda i: (0, 0))],
        out_specs=pl.BlockSpec((tr, D_MODEL), lambda i: (i, 0)),
        out_shape=jax.ShapeDtypeStruct((ma + mb, D_MODEL), BF16),
        compiler_params=_cparams(("arbitrary",)))(xa, xb, g.reshape(1, D_MODEL))


def _post_kernel(*refs, nb_first, n_src, want_h):
    x_refs = refs[:n_src]
    y_ref, gp_ref, gn_ref, xo_ref = refs[n_src:n_src + 4]
    r = _rms(y_ref[...].astype(F32), gp_ref[...])

    def finish(x_ref):
        xn = x_ref[...] + r
        xo_ref[...] = xn
        if want_h:
            refs[n_src + 4][...] = _rms(xn, gn_ref[...]).astype(BF16)

    if n_src == 1:
        finish(x_refs[0])
    else:
        i = pl.program_id(0)
        pl.when(i < nb_first)(lambda: finish(x_refs[0]))
        pl.when(i >= nb_first)(lambda: finish(x_refs[1]))


def _post(xs, y, g_post, g_next, tr):
    m = y.shape[0]
    want_h = g_next is not None
    row = pl.BlockSpec((tr, D_MODEL), lambda i: (i, 0))
    vec = pl.BlockSpec((1, D_MODEL), lambda i: (0, 0))
    nb_first = xs[0].shape[0] // tr
    x_specs = list(_two_source_specs(tr, nb_first)) if len(xs) == 2 else [row]
    out_shape = [jax.ShapeDtypeStruct((m, D_MODEL), F32)]
    out_specs = [row]
    if want_h:
        out_shape.append(jax.ShapeDtypeStruct((m, D_MODEL), BF16))
        out_specs.append(row)
    gn = (g_next if want_h else g_post).reshape(1, D_MODEL)
    res = pl.pallas_call(
        functools.partial(_post_kernel, nb_first=nb_first, n_src=len(xs), want_h=want_h), grid=(m // tr,),
        in_specs=x_specs + [row, vec, vec], out_specs=out_specs, out_shape=out_shape,
        compiler_params=_cparams(("arbitrary",)))(*xs, y, g_post.reshape(1, D_MODEL), gn)
    return (res[0], res[1]) if want_h else (res[0], None)


def _post_split_kernel(x_ref, y_ref, gp_ref, oa_ref, ob_ref, *, nb_first):
    i = pl.program_id(0)
    xn = x_ref[...] + _rms(y_ref[...].astype(F32), gp_ref[...])

    @pl.when(i < nb_first)
    def _():
        oa_ref[...] = xn

    @pl.when(i >= nb_first)
    def _():
        ob_ref[...] = xn


def _post_split(x, y, g_post, rows_first, tr):
    m = x.shape[0]
    row = pl.BlockSpec((tr, D_MODEL), lambda i: (i, 0))
    vec = pl.BlockSpec((1, D_MODEL), lambda i: (0, 0))
    oa, ob = _two_source_specs(tr, rows_first // tr)
    return pl.pallas_call(
        functools.partial(_post_split_kernel, nb_first=rows_first // tr), grid=(m // tr,),
        in_specs=[row, row, vec], out_specs=[oa, ob],
        out_shape=[jax.ShapeDtypeStruct((rows_first, D_MODEL), F32),
                   jax.ShapeDtypeStruct((m - rows_first, D_MODEL), F32)],
        compiler_params=_cparams(("arbitrary",)))(x, y, g_post.reshape(1, D_MODEL))


def _gdn_prep_kernel(a_ref, b_ref, alog_ref, dt_ref, gc_ref, beta_ref, *, prompt_blocks):
    i = pl.program_id(0)
    x = a_ref[...] + dt_ref[...]
    softplus = jnp.maximum(x, 0.0) + jnp.log1p(jnp.exp(-jnp.abs(x)))
    g = -jnp.exp(alog_ref[...]) * softplus
    chunk = jnp.where(i < prompt_blocks, MIX_CHUNK, 8)
    rmod = lax.broadcasted_iota(jnp.int32, g.shape, 0) & (chunk - 1)
    s = 1
    while s < MIX_CHUNK:
        g = g + jnp.where(rmod >= s, pltpu.roll(g, s, axis=0), 0.0)
        s *= 2
    gc_ref[...] = g
    beta_ref[...] = jax.nn.sigmoid(b_ref[...])


def _gdn_prep(ab, a_log, dt_bias, tr, prompt_rows):
    m = ab.shape[0]
    pad = lambda v: jnp.pad(v.astype(F32), (0, 128 - H_A)).reshape(1, 128)
    blk = lambda c: pl.BlockSpec((tr, 128), lambda i: (i, c))
    vec = pl.BlockSpec((1, 128), lambda i: (0, 0))
    return pl.pallas_call(
        functools.partial(_gdn_prep_kernel, prompt_blocks=prompt_rows // tr), grid=(m // tr,),
        in_specs=[blk(0), blk(1), vec, vec], out_specs=[blk(0), blk(0)],
        out_shape=[jax.ShapeDtypeStruct((m, 128), F32)] * 2,
        compiler_params=_cparams(("parallel",)))(ab, ab, pad(a_log), pad(dt_bias))


def _gdn_kernel(x_ref, z_ref, gcol_ref, bcol_ref, grow_ref, s0_ref, buf_ref, cw_ref, ng_ref, stack_ref, obuf_ref,
                o_ref, s_ref, tail_ref, xc_ref, *, c, group, nseq):
    @pl.when(pl.program_id(1) == 0)
    def _():
        s_ref[...] = s0_ref[...]
        xc_ref[:, :8, :] = buf_ref[...]

    for q in range(nseq):
        xc_ref[q, 8:, :] = x_ref[q * c:(q + 1) * c, :]
        tail_ref[q] = x_ref[(q + 1) * c - 8:(q + 1) * c, :]

    def conv(q, lo):
        xcol = xc_ref[q, :, lo:lo + 128]
        acc = xcol[8:] * cw_ref[CONV_W - 1:CONV_W, lo:lo + 128]
        for s in range(1, CONV_W):
            acc = acc + pltpu.roll(xcol, s, axis=0)[8:] * cw_ref[CONV_W - 1 - s:CONV_W - s, lo:lo + 128]
        return jax.nn.silu(acc)

    def l2n(t):
        return t * lax.rsqrt(jnp.sum(t * t, axis=-1, keepdims=True) + EPS)

    ii = lax.broadcasted_iota(jnp.int32, (c, c), 0)
    jj = lax.broadcasted_iota(jnp.int32, (c, c), 1)
    tri = ii >= jj
    strict = ii > jj
    ng = ng_ref[...]
    hk = H_A * DK_A
    for h0 in range(0, H_A, group):
        st = []
        for q in range(nseq):
            rows = slice(q * c, (q + 1) * c)
            for h in range(h0, h0 + group):
                lo = h * DK_A
                qh = l2n(conv(q, lo)) * (DK_A ** -0.5)
                k = l2n(conv(q, hk + lo))
                v = conv(q, 2 * hk + lo)
                gc = gcol_ref[rows, h:h + 1]
                beta = bcol_ref[rows, h:h + 1]
                gr = grow_ref[q, 0, h:h + 1, :]
                glast = gr[:, c - 1:c]
                decay = jnp.exp(jnp.where(tri, gc - gr, -jnp.inf))
                egc = jnp.exp(gc)
                kb = k * beta
                m = lax.dot_general(jnp.concatenate([kb, qh], axis=0).astype(BF16), k.astype(BF16),
                                    (((1,), (1,)), ((), ())), preferred_element_type=F32)
                s = s_ref[q, h]
                sq = _bdot(jnp.concatenate([kb * egc, qh * egc], axis=0), s)
                st.append(dict(q=q, h=h, rows=rows, m=m, sq=sq, s=s, decay=decay, vb=v * beta,
                               kd=(k * jnp.exp(glast - gc)).astype(BF16), cd=jnp.exp(glast)))
        for d in st:
            d["p"] = (-jnp.where(strict, d["m"][:c] * d["decay"], 0.0)).astype(BF16)
            d["attn"] = (d["m"][c:] * d["decay"]).astype(BF16)
            d["x"] = d["vb"] - d["sq"][:c]
        n_sq = 1
        while n_sq < c:
            last = 2 * n_sq >= c
            for d in st:
                d["x"] = d["x"] + jnp.dot(d["p"], d["x"].astype(BF16), preferred_element_type=F32)
            if not last:
                for d in st:
                    d["p"] = jnp.dot(d["p"], d["p"], preferred_element_type=F32).astype(BF16)
            n_sq *= 2
        outs = {}
        for d in st:
            xb = d["x"].astype(BF16)
            o = d["sq"][c:] + jnp.dot(d["attn"], xb, preferred_element_type=F32)
            s_ref[d["q"], d["h"]] = d["s"] * d["cd"] + lax.dot_general(
                d["kd"], xb, (((0,), (0,)), ((), ())), preferred_element_type=F32)
            lo = d["h"] * DV_A
            o = o * lax.rsqrt(jnp.mean(o * o, axis=-1, keepdims=True) + EPS) * ng
            outs[d["q"], d["h"]] = o * jax.nn.silu(z_ref[d["rows"], lo:lo + DV_A])
        for h in range(h0, h0 + group):
            o = outs[0, h] if nseq == 1 else jnp.concatenate([outs[q, h] for q in range(nseq)], axis=0)
            o_ref[:, h * DV_A:(h + 1) * DV_A] = o.astype(o_ref.dtype)
    xc_ref[:, :8, :] = xc_ref[:, c:c + 8, :]


def _gdn(proj, gc, beta, grow, s0, layer, stack, layer_out, o_buf, buf8, conv_w, norm_g, row_off, n, l, c,
         group, nseq):
    nc = l // c
    assert nseq == 1 or nc == 1
    blk = nseq * c
    base = row_off // blk
    rows = lambda w, col: pl.BlockSpec((blk, w), lambda i, j: (base + i * nc + j, col))
    return pl.pallas_call(
        functools.partial(_gdn_kernel, c=c, group=group, nseq=nseq), grid=(n // nseq, nc),
        in_specs=[rows(CONV_DIM, 0), rows(BRANCH_W, COL_Z // BRANCH_W), rows(128, 0), rows(128, 0),
                  pl.BlockSpec((nseq, 1, H_A, c), lambda i, j: (i, j, 0, 0)),
                  pl.BlockSpec((None, nseq, H_A, DK_A, DV_A), lambda i, j: (layer, i, 0, 0, 0)),
                  pl.BlockSpec((nseq, 8, CONV_DIM), lambda i, j: (i, 0, 0)),
                  pl.BlockSpec((CONV_W, CONV_DIM), lambda i, j: (0, 0)),
                  pl.BlockSpec((1, DV_A), lambda i, j: (0, 0)),
                  pl.BlockSpec(memory_space=pl.ANY), pl.BlockSpec(memory_space=pl.ANY)],
        out_specs=[rows(BRANCH_W, 0),
                   pl.BlockSpec((None, nseq, H_A, DK_A, DV_A), lambda i, j: (layer_out, i, 0, 0, 0)),
                   pl.BlockSpec((nseq, 8, CONV_DIM), lambda i, j: (i, 0, 0))],
        out_shape=[jax.ShapeDtypeStruct(o_buf.shape, o_buf.dtype),
                   jax.ShapeDtypeStruct(stack.shape, F32),
                   jax.ShapeDtypeStruct((n, 8, CONV_DIM), F32)],
        scratch_shapes=[pltpu.VMEM((nseq, c + 8, CONV_DIM), F32)],
        input_output_aliases={9: 1, 10: 0},
        compiler_params=_cparams(("parallel", "arbitrary")))(
            proj, proj, gc, beta, grow, s0, buf8, conv_w, norm_g.reshape(1, DV_A), stack, o_buf)


def _ret_kernel(qk_ref, v_ref, gate_ref, cos_ref, sin_ref, r0_ref, stack_ref, obuf_ref, o_ref, r_ref, *, c, nseq):
    @pl.when(pl.program_id(1) == 0)
    def _():
        r_ref[...] = r0_ref[...]

    ii = lax.broadcasted_iota(jnp.int32, (c, c), 0)
    jj = lax.broadcasted_iota(jnp.int32, (c, c), 1)
    rel = (ii - jj).astype(F32)
    idx = lax.broadcasted_iota(jnp.int32, (c, 1), 0).astype(F32)
    even = (lax.broadcasted_iota(jnp.int32, (c, DK_B), 1) & 1) == 0
    cos = cos_ref[...]
    sin = sin_ref[...]

    def rot(x):
        swapped = jnp.where(even, pltpu.roll(x, DK_B - 1, axis=1), pltpu.roll(x, 1, axis=1))
        return x * cos + swapped * sin

    units = [(q, h) for q in range(nseq) for h in range(H_B)]
    lg = {h: math.log1p(-2.0 ** (-5.0 - h)) for h in range(H_B)}
    rows = {q: slice(q * c, (q + 1) * c) for q in range(nseq)}
    qs = {u: rot(qk_ref[rows[u[0]], u[1] * DK_B:(u[1] + 1) * DK_B]) for u in units}
    ks = {u: rot(qk_ref[rows[u[0]], (H_B + u[1]) * DK_B:(H_B + u[1] + 1) * DK_B]) * (DK_B ** -0.5)
          for u in units}
    vs = {u: v_ref[rows[u[0]], u[1] * DV_B:(u[1] + 1) * DV_B].astype(BF16) for u in units}
    rs = {u: r_ref[u] for u in units}
    scores = {u: _bdot_nt(qs[u], ks[u]) for u in units}
    cross = {u: _bdot(qs[u] * jnp.exp(lg[u[1]] * (idx + 1.0)), rs[u]) for u in units}
    for u in units:
        r_ref[u] = rs[u] * math.exp(lg[u[1]] * c) + _bdot_tn(ks[u] * jnp.exp(lg[u[1]] * (c - 1.0 - idx)), vs[u])
    inners = {u: _bdot(scores[u] * jnp.where(rel >= 0, jnp.exp(lg[u[1]] * jnp.maximum(rel, 0.0)), 0.0), vs[u])
              for u in units}
    outs = {}
    for u in units:
        o = inners[u] + cross[u]
        mu = jnp.mean(o, axis=-1, keepdims=True)
        d = o - mu
        var = jnp.mean(d * d, axis=-1, keepdims=True)
        o = d * lax.rsqrt(var + EPS)
        outs[u] = jax.nn.silu(gate_ref[rows[u[0]], u[1] * DV_B:(u[1] + 1) * DV_B]) * o
    for h in range(H_B):
        o = outs[0, h] if nseq == 1 else jnp.concatenate([outs[q, h] for q in range(nseq)], axis=0)
        o_ref[:, h * DV_B:(h + 1) * DV_B] = o.astype(o_ref.dtype)


def _rope_tables(pos):
    half = DK_B // 2
    inv_freq = 1.0 / (ROPE_BASE ** jnp.linspace(0.0, 1.0, half, dtype=F32))
    ang = pos.astype(F32)[:, None] * inv_freq[None, :]
    sin = jnp.sin(ang)
    cos = jnp.cos(ang)
    cos2 = jnp.stack([cos, cos], axis=-1).reshape(-1, DK_B)
    sin2 = jnp.stack([-sin, sin], axis=-1).reshape(-1, DK_B)
    return cos2, sin2


def _retention(proj, cos2, sin2, r0, layer, stack, layer_out, o_buf, row_off, n, l, c, nseq):
    nc = l // c
    assert nseq == 1 or nc == 1
    blk = nseq * c
    base = row_off // blk
    col = lambda off: pl.BlockSpec((blk, BRANCH_W), lambda i, j: (base + i * nc + j, off // BRANCH_W))
    tab = pl.BlockSpec((c, DK_B), lambda i, j: (j, 0))
    return pl.pallas_call(
        functools.partial(_ret_kernel, c=c, nseq=nseq), grid=(n // nseq, nc),
        in_specs=[col(COL_QKB), col(COL_VB), col(COL_GB), tab, tab,
                  pl.BlockSpec((None, nseq, H_B, DK_B, DV_B), lambda i, j: (layer, i, 0, 0, 0)),
                  pl.BlockSpec(memory_space=pl.ANY), pl.BlockSpec(memory_space=pl.ANY)],
        out_specs=[col(0),
                   pl.BlockSpec((None, nseq, H_B, DK_B, DV_B), lambda i, j: (layer_out, i, 0, 0, 0))],
        out_shape=[jax.ShapeDtypeStruct(o_buf.shape, o_buf.dtype),
                   jax.ShapeDtypeStruct(stack.shape, F32)],
        input_output_aliases={6: 1, 7: 0},
        compiler_params=_cparams(("parallel", "arbitrary")))(proj, proj, proj, cos2, sin2, r0, stack, o_buf)


def _gelu(x):
    return 0.5 * x * (1.0 + lax.erf(x * (2.0 ** -0.5)))


def _gmlp_kernel(u_ref, v_ref, w_ref, bcol_ref, lng_ref, lnb_ref, obuf_ref, o_ref, *vrows_ref, seq):
    u32 = _gelu(u_ref[...])
    v32 = _gelu(v_ref[...])
    mu = jnp.mean(v32, axis=-1, keepdims=True)
    d = v32 - mu
    var = jnp.mean(d * d, axis=-1, keepdims=True)
    vn = d * lax.rsqrt(var + EPS) * lng_ref[...] + lnb_ref[...]
    if vrows_ref:
        vrows_ref[0][...] = vn
    ii = lax.broadcasted_iota(jnp.int32, (CHUNK_C, CHUNK_C), 0)
    jj = lax.broadcasted_iota(jnp.int32, (CHUNK_C, CHUNK_C), 1)
    mask = (ii >= jj) & ((ii & -seq) == (jj & -seq))
    for g in range(G_C):
        w = jnp.where(mask, w_ref[g], 0.0)
        mixed = _bdot(w, vn[:, g * DG_C:(g + 1) * DG_C]) + bcol_ref[:, g:g + 1]
        o_ref[:, g * DG_C:(g + 1) * DG_C] = (u32[:, g * DG_C:(g + 1) * DG_C] * mixed).astype(o_ref.dtype)


def _gmlp(proj, w_tile, b_col, ln_g, ln_b, o_buf, row_off, rows, seq, want_vrows):
    base = row_off // CHUNK_C
    col = lambda off: pl.BlockSpec((CHUNK_C, BRANCH_W), lambda i: (base + i, off // BRANCH_W))
    vec = pl.BlockSpec((1, BRANCH_W), lambda i: (0, 0))
    out_shape = [jax.ShapeDtypeStruct(o_buf.shape, o_buf.dtype)]
    out_specs = [col(0)]
    if want_vrows:
        out_shape.append(jax.ShapeDtypeStruct((rows, BRANCH_W), F32))
        out_specs.append(pl.BlockSpec((CHUNK_C, BRANCH_W), lambda i: (i, 0)))
    res = pl.pallas_call(
        functools.partial(_gmlp_kernel, seq=seq), grid=(rows // CHUNK_C,),
        in_specs=[col(COL_U), col(COL_V),
                  pl.BlockSpec((G_C, CHUNK_C, CHUNK_C), lambda i: (0, 0, 0)),
                  pl.BlockSpec((CHUNK_C, G_C), lambda i: (0, 0)), vec, vec,
                  pl.BlockSpec(memory_space=pl.ANY)],
        out_specs=out_specs, out_shape=out_shape, input_output_aliases={6: 0},
        compiler_params=_cparams(("parallel",)))(
            proj, proj, w_tile, b_col, ln_g.reshape(1, BRANCH_W), ln_b.reshape(1, BRANCH_W), o_buf)
    return res if want_vrows else (res[0], None)


AB_COL0 = 4 * BRANCH_W
AB_COLS = 2 * H_A
WD_BLK, WBR_BLK, PACK_BLK = 64, 32, 128


def _pack_kernel(w_ref, o_ref):
    o_ref[...] = w_ref[0].astype(BF16)


def _pack_src_row(j, tn):
    r = j * tn
    return r + jnp.where(r >= AB_COL0, AB_COLS, 0)


def _pack_w_in_t(w_in_t, layer, tn):
    k = w_in_t.shape[2]
    return pl.pallas_call(
        _pack_kernel, grid=(N_PROJ // tn,),
        in_specs=[pl.BlockSpec((pl.Element(1), pl.Element(tn), pl.Element(k)),
                               lambda j: (layer, pl.multiple_of(_pack_src_row(j, tn), 8), 0))],
        out_specs=pl.BlockSpec((tn, k), lambda j: (j, 0)),
        out_shape=jax.ShapeDtypeStruct((N_PROJ, k), BF16),
        compiler_params=_cparams(("parallel",)))(w_in_t)


def _ab_weights_t(w_in_t, layer):
    ab = w_in_t[layer, AB_COL0:AB_COL0 + AB_COLS, :]
    zeros = jnp.zeros((128 - H_A, ab.shape[1]), F32)
    return jnp.concatenate([ab[:H_A], zeros, ab[H_A:], zeros], axis=0)


def kernel(x_prompt, x_sample, state_gdn, state_conv, state_ret, w_in, conv_w, a_log, dt_bias, gdn_norm, w_s, b_s, ln_c_g, ln_c_b, w_br, w_o, g_pre_mix, g_post_mix, g_pre_ffn, g_post_ffn, w_ffn_gate, w_ffn_up, w_ffn_down):
    n_p, l_p, _ = x_prompt.shape
    n_s, l_s, _ = x_sample.shape
    depth = w_in.shape[0]
    rows_p, rows_s = n_p * l_p, n_s * l_s
    m = rows_p + rows_s
    xs = (x_prompt.reshape(rows_p, D_MODEL), x_sample.reshape(rows_s, D_MODEL))
    tr = 256

    cos_p, sin_p = _rope_tables(jnp.arange(l_p))
    cos_s, sin_s = _rope_tables(PAST_LEN + jnp.arange(l_s))
    zero_gdn = jnp.zeros((1, n_p, H_A, DK_A, DV_A), F32)
    zero_ret = jnp.zeros((1, n_p, H_B, DK_B, DV_B), F32)
    zero_buf8 = jnp.zeros((n_p, 8, CONV_DIM), F32)
    reps = CHUNK_C // l_s

    w_in_t = jnp.swapaxes(w_in, 1, 2)

    gdn_p = jnp.zeros((depth, n_p, H_A, DK_A, DV_A), F32)
    gdn_s = jnp.zeros((depth, n_s, H_A, DK_A, DV_A), F32)
    ret_p = jnp.zeros((depth, n_p, H_B, DK_B, DV_B), F32)
    ret_s = jnp.zeros((depth, n_s, H_B, DK_B, DV_B), F32)
    outs = {k: [] for k in ("conv_p", "conv_s", "vrows")}
    h = _prenorm(xs[0], xs[1], g_pre_mix[0], tr)
    w_br_rows = w_br.reshape(depth, 3 * BRANCH_W, D_MODEL)
    w_main = _pack_w_in_t(w_in_t, 0, 512)
    for l in range(depth):
        side = [_SideCast(w_ffn_down, l, WD_BLK, D_FF_PAD // WD_BLK,
                          lambda b: jnp.minimum(b, D_FF // WD_BLK - 1) * WD_BLK, D_FF // WD_BLK),
                _SideCast(w_br_rows, l, WBR_BLK, 3 * BRANCH_W // WBR_BLK, lambda b: b * WBR_BLK,
                          3 * BRANCH_W // WBR_BLK)]
        if l + 1 < depth:
            side.append(_SideCast(w_in_t, l + 1, PACK_BLK, N_PROJ // PACK_BLK,
                                  lambda b: _pack_src_row(b, PACK_BLK), N_PROJ // PACK_BLK))
        proj, wd, w_br_l, *w_main_next = _matmul_nt(h, w_main, F32, 1024, 1024, side)
        w_main = w_main_next[0] if w_main_next else None
        ab = _matmul_nt(h, _ab_weights_t(w_in_t, l), F32, 1024, 256)[0]

        gc, beta = _gdn_prep(ab, a_log[l], dt_bias[l], 1024, rows_p)
        grow_p = gc[:rows_p, :H_A].reshape(n_p, l_p // MIX_CHUNK, MIX_CHUNK, H_A).transpose(0, 1, 3, 2)
        grow_s = gc[rows_p:, :H_A].reshape(n_s, 1, l_s, H_A).transpose(0, 1, 3, 2)
        buf8_s = jnp.pad(state_conv[l], ((0, 0), (8 - (CONV_W - 1), 0), (0, 0)))
        o_a = jnp.zeros((m, BRANCH_W), BF16)
        o_b = jnp.zeros((m, BRANCH_W), BF16)
        o_c = jnp.zeros((m, BRANCH_W), BF16)
        o_a, gdn_p, tail_p = _gdn(proj, gc, beta, grow_p, zero_gdn, 0, gdn_p, l, o_a, zero_buf8, conv_w[l],
                                  gdn_norm[l], 0, n_p, l_p, MIX_CHUNK, H_A, 1)
        o_a, gdn_s, tail_s = _gdn(proj, gc, beta, grow_s, state_gdn, l, gdn_s, l, o_a, buf8_s, conv_w[l],
                                  gdn_norm[l], rows_p, n_s, l_s, l_s, GDN_GROUP, GDN_SAMPLE_SEQS)
        outs["conv_p"].append(tail_p[:, 8 - (CONV_W - 1):])
        outs["conv_s"].append(tail_s[:, 8 - (CONV_W - 1):])

        o_b, ret_p = _retention(proj, cos_p, sin_p, zero_ret, 0, ret_p, l, o_b, 0, n_p, l_p, RET_CHUNK, 1)
        o_b, ret_s = _retention(proj, cos_s, sin_s, state_ret, l, ret_s, l, o_b, rows_p, n_s, l_s, l_s,
                                RET_SAMPLE_SEQS)

        o_c, _ = _gmlp(proj, w_s[l], b_s[l].T, ln_c_g[l], ln_c_b[l], o_c, 0, rows_p, CHUNK_C, False)
        w_tile_s = jnp.tile(w_s[l][:, :l_s, :l_s], (1, reps, reps))
        b_col_s = jnp.tile(b_s[l][:, :l_s].T, (reps, 1))
        o_c, vrows = _gmlp(proj, w_tile_s, b_col_s, ln_c_g[l], ln_c_b[l], o_c, rows_p, rows_s, l_s, True)
        outs["vrows"].append(vrows.reshape(n_s, l_s, BRANCH_W))

        merged = _merge(o_a, o_b, o_c, w_br_l.reshape(3, BRANCH_W, D_MODEL), proj, 1024, 512)
        mix = _matmul_ws(merged, w_o, l, BF16, 1024, 512)
        x, h = _post(xs, mix, g_post_mix[l], g_pre_ffn[l], tr)

        f1 = _ffn_up(h, w_ffn_gate, w_ffn_up, l, 512, 512, D_FF_PAD)
        f = _matmul_kgrid(f1, wd, BF16, 1024, 1024, D_FF_PAD // 2)
        if l + 1 < depth:
            x, h = _post((x,), f, g_post_ffn[l], g_pre_mix[l + 1], tr)
            xs = (x,)
        else:
            y_p, y_s = _post_split(x, f, g_post_ffn[l], rows_p, tr)

    y_p = y_p.reshape(n_p, l_p, D_MODEL)
    y_s = y_s.reshape(n_s, l_s, D_MODEL)
    st = lambda k: jnp.stack(outs[k])
    return (y_p, y_s, gdn_p, gdn_s, st("conv_p"), st("conv_s"), ret_p, ret_s, st("vrows"))
```

```python
import functools
import math
from typing import Callable, NamedTuple

import jax
import jax.numpy as jnp
from jax import lax
from jax.experimental import pallas as pl
from jax.experimental.pallas import tpu as pltpu

F32 = jnp.float32
BF16 = jnp.bfloat16

D_MODEL = 4096
BRANCH_W = 2048
H_A, DK_A, DV_A = 16, 128, 128
CONV_W = 4
CONV_DIM = 6144
H_B, DK_B, DV_B = 8, 128, 256
G_C, DG_C, CHUNK_C = 8, 256, 128
D_FF = 11008
D_FF_PAD = 11264
EPS = 1e-6
ROPE_BASE = 10000.0
PAST_LEN = 16384
MIX_CHUNK = 64
RET_CHUNK = 128
GDN_GROUP = 8
GDN_SAMPLE_SEQS = 2
RET_SAMPLE_SEQS = 4

COL_QKV, COL_Z, COL_QKB, COL_VB, COL_GB, COL_U, COL_V, COL_GATE = (
    0, 6144, 8192, 10240, 12288, 14336, 16384, 18432)
N_PROJ = 30720
VMEM_LIMIT = 56 * 1024 * 1024
VMEM_LIMIT_HOST = 60 * 1024 * 1024


def _cparams(sem, vmem=VMEM_LIMIT):
    return pltpu.CompilerParams(dimension_semantics=sem, vmem_limit_bytes=vmem)


def _sigmoid(x):
    return 0.5 * jnp.tanh(0.5 * x) + 0.5


def _bdot(a, b):
    return jnp.dot(a.astype(BF16), b.astype(BF16), preferred_element_type=F32)


def _bdot_nt(a, b):
    return lax.dot_general(a.astype(BF16), b.astype(BF16), (((1,), (1,)), ((), ())),
                           preferred_element_type=F32)


def _bdot_tn(a, b):
    return lax.dot_general(a.astype(BF16), b.astype(BF16), (((0,), (0,)), ((), ())),
                           preferred_element_type=F32)


class _SideCast(NamedTuple):
    w: jax.Array
    layer: int
    blk: int
    n_blocks: int
    src_row: Callable
    n_copy: int


def _side_specs(side, steps_inner):
    def blk_idx(job, g0, g1):
        return jnp.minimum(g0 * steps_inner + g1, job.n_blocks - 1)

    in_specs = [pl.BlockSpec((pl.Element(1), pl.Element(job.blk), pl.Element(job.w.shape[2])),
                             lambda g0, g1, job=job: (job.layer,
                                                      pl.multiple_of(job.src_row(blk_idx(job, g0, g1)), 8), 0))
                for job in side]
    out_specs = [pl.BlockSpec((job.blk, job.w.shape[2]), lambda g0, g1, job=job: (blk_idx(job, g0, g1), 0))
                 for job in side]
    out_shape = [jax.ShapeDtypeStruct((job.n_blocks * job.blk, job.w.shape[2]), BF16) for job in side]
    return in_specs, out_specs, out_shape


def _run_side_casts(side, steps_inner, w_refs, o_refs):
    t = pl.program_id(0) * steps_inner + pl.program_id(1)
    for job, w_ref, so_ref in zip(side, w_refs, o_refs):
        v = w_ref[0].astype(BF16)
        so_ref[...] = v if job.n_copy >= job.n_blocks else jnp.where(t < job.n_copy, v, jnp.zeros_like(v))


def _mm_nt_kernel(a_ref, bt_ref, *refs, side, steps_inner):
    n_side = len(side)
    o_ref = refs[n_side]
    o_ref[...] = lax.dot_general(a_ref[...], bt_ref[...].astype(BF16), (((1,), (1,)), ((), ())),
                                 preferred_element_type=F32).astype(o_ref.dtype)
    _run_side_casts(side, steps_inner, refs[:n_side], refs[n_side + 1:])


def _matmul_nt(a, bt, out_dtype, tm, tn, side=()):
    m, k = a.shape
    n = bt.shape[0]
    nj = n // tn
    assert all(job.n_blocks <= (m // tm) * nj for job in side)
    side_in, side_out, side_shape = _side_specs(side, nj)
    return pl.pallas_call(
        functools.partial(_mm_nt_kernel, side=tuple(side), steps_inner=nj), grid=(m // tm, nj),
        in_specs=[pl.BlockSpec((tm, k), lambda i, j: (i, 0)),
                  pl.BlockSpec((tn, k), lambda i, j: (j, 0))] + side_in,
        out_specs=[pl.BlockSpec((tm, tn), lambda i, j: (i, j))] + side_out,
        out_shape=[jax.ShapeDtypeStruct((m, n), out_dtype)] + side_shape,
        compiler_params=_cparams(("arbitrary", "arbitrary"), VMEM_LIMIT_HOST if side else VMEM_LIMIT))(
            a, bt, *[job.w for job in side])


def _mm_ws_kernel(a_ref, b_ref, *refs, side, steps_inner):
    n_side = len(side)
    o_ref, bw_ref = refs[n_side], refs[-1]

    @pl.when(pl.program_id(1) == 0)
    def _():
        bw_ref[...] = b_ref[...].astype(BF16)

    o_ref[...] = jnp.dot(a_ref[...], bw_ref[...], preferred_element_type=F32).astype(o_ref.dtype)
    _run_side_casts(side, steps_inner, refs[:n_side], refs[n_side + 1:-1])


def _matmul_ws(a, b, layer, out_dtype, tm, tn, side=()):
    m, k = a.shape
    n = b.shape[2]
    ni = m // tm
    assert all(job.n_blocks <= (n // tn) * ni for job in side)
    side_in, side_out, side_shape = _side_specs(side, ni)
    return pl.pallas_call(
        functools.partial(_mm_ws_kernel, side=tuple(side), steps_inner=ni), grid=(n // tn, ni),
        in_specs=[pl.BlockSpec((tm, k), lambda j, i: (i, 0)),
                  pl.BlockSpec((None, k, tn), lambda j, i: (layer, 0, j))] + side_in,
        out_specs=[pl.BlockSpec((tm, tn), lambda j, i: (i, j))] + side_out,
        out_shape=[jax.ShapeDtypeStruct((m, n), out_dtype)] + side_shape,
        scratch_shapes=[pltpu.VMEM((k, tn), BF16)],
        compiler_params=_cparams(("arbitrary", "arbitrary")))(a, b, *[job.w for job in side])


def _mm_acc_kernel(a_ref, b_ref, o_ref, acc_ref):
    k = pl.program_id(2)
    last = pl.num_programs(2) - 1

    @pl.when(k == 0)
    def _():
        acc_ref[...] = jnp.dot(a_ref[...], b_ref[...], preferred_element_type=F32)

    @pl.when((k > 0) & (k < last))
    def _():
        acc_ref[...] += jnp.dot(a_ref[...], b_ref[...], preferred_element_type=F32)

    @pl.when(k == last)
    def _():
        o_ref[...] = (acc_ref[...] + jnp.dot(a_ref[...], b_ref[...], preferred_element_type=F32)).astype(o_ref.dtype)


def _matmul_kgrid(a, b, out_dtype, tm, tn, tk):
    m, k = a.shape
    n = b.shape[1]
    assert k // tk >= 2
    return pl.pallas_call(
        _mm_acc_kernel, grid=(m // tm, n // tn, k // tk),
        in_specs=[pl.BlockSpec((tm, tk), lambda i, j, kk: (i, kk)),
                  pl.BlockSpec((tk, tn), lambda i, j, kk: (kk, j))],
        out_specs=pl.BlockSpec((tm, tn), lambda i, j, kk: (i, j)),
        out_shape=jax.ShapeDtypeStruct((m, n), out_dtype),
        scratch_shapes=[pltpu.VMEM((tm, tn), F32)],
        compiler_params=_cparams(("parallel", "parallel", "arbitrary")))(a, b)


def _ffn_up_kernel(h_ref, wg_ref, wu_ref, o_ref, *, tn, n_valid):
    h = h_ref[...]
    g = jnp.dot(h, wg_ref[...], preferred_element_type=F32)
    u = jnp.dot(h, wu_ref[...], preferred_element_type=F32)
    col = pl.program_id(1) * tn + lax.broadcasted_iota(jnp.int32, g.shape, 1)
    o_ref[...] = jnp.where(col < n_valid, g * _sigmoid(g) * u, 0.0).astype(o_ref.dtype)


def _ffn_up(h, wg, wu, tm, tn, n_out):
    m, k = h.shape
    n = wg.shape[1]
    wspec = pl.BlockSpec((k, tn), lambda i, j: (0, j))
    return pl.pallas_call(
        functools.partial(_ffn_up_kernel, tn=tn, n_valid=n), grid=(m // tm, n_out // tn),
        in_specs=[pl.BlockSpec((tm, k), lambda i, j: (i, 0)), wspec, wspec],
        out_specs=pl.BlockSpec((tm, tn), lambda i, j: (i, j)),
        out_shape=jax.ShapeDtypeStruct((m, n_out), BF16),
        compiler_params=_cparams(("parallel", "parallel")))(h, wg, wu)


def _merge_kernel(oa_ref, ob_ref, oc_ref, w_ref, ga_ref, gb_ref, gc_ref, o_ref):
    acc = _sigmoid(ga_ref[...]) * jnp.dot(oa_ref[...], w_ref[0], preferred_element_type=F32)
    acc = acc + _sigmoid(gb_ref[...]) * jnp.dot(ob_ref[...], w_ref[1], preferred_element_type=F32)
    acc = acc + _sigmoid(gc_ref[...]) * jnp.dot(oc_ref[...], w_ref[2], preferred_element_type=F32)
    o_ref[...] = acc.astype(o_ref.dtype)


def _merge(o_a, o_b, o_c, w_br, proj, tm, tn):
    m = o_a.shape[0]
    gate_blk0 = COL_GATE // tn
    per_branch = D_MODEL // tn
    o_spec = pl.BlockSpec((tm, BRANCH_W), lambda i, j: (i, 0))
    gate = lambda b: pl.BlockSpec((tm, tn), lambda i, j: (i, gate_blk0 + b * per_branch + j))
    return pl.pallas_call(
        _merge_kernel, grid=(m // tm, D_MODEL // tn),
        in_specs=[o_spec, o_spec, o_spec,
                  pl.BlockSpec((3, BRANCH_W, tn), lambda i, j: (0, 0, j)),
                  gate(0), gate(1), gate(2)],
        out_specs=pl.BlockSpec((tm, tn), lambda i, j: (i, j)),
        out_shape=jax.ShapeDtypeStruct((m, D_MODEL), BF16),
        compiler_params=_cparams(("parallel", "parallel")))(o_a, o_b, o_c, w_br, proj, proj, proj)


def _rms(x, g):
    return x * lax.rsqrt(jnp.mean(x * x, axis=-1, keepdims=True) + EPS) * g


def _two_source_specs(tr, nb_first):
    first = pl.BlockSpec((tr, D_MODEL), lambda i: (jnp.minimum(i, nb_first - 1), 0))
    second = pl.BlockSpec((tr, D_MODEL), lambda i: (jnp.maximum(i - nb_first, 0), 0))
    return first, second


def _prenorm_kernel(xa_ref, xb_ref, g_ref, h_ref, *, nb_first):
    i = pl.program_id(0)

    @pl.when(i < nb_first)
    def _():
        h_ref[...] = _rms(xa_ref[...], g_ref[...]).astype(h_ref.dtype)

    @pl.when(i >= nb_first)
    def _():
        h_ref[...] = _rms(xb_ref[...], g_ref[...]).astype(h_ref.dtype)


def _prenorm(xa, xb, g, tr):
    ma, mb = xa.shape[0], xb.shape[0]
    sa, sb = _two_source_specs(tr, ma // tr)
    return pl.pallas_call(
        functools.partial(_prenorm_kernel, nb_first=ma // tr), grid=((ma + mb) // tr,),
        in_specs=[sa, sb, pl.BlockSpec((1, D_MODEL), lambda i: (0, 0))],
        out_specs=pl.BlockSpec((tr, D_MODEL), lambda i: (i, 0)),
        out_shape=jax.ShapeDtypeStruct((ma + mb, D_MODEL), BF16),
        compiler_params=_cparams(("arbitrary",)))(xa, xb, g.reshape(1, D_MODEL))


def _post_kernel(*refs, nb_first, n_src, want_h):
    x_refs = refs[:n_src]
    y_ref, gp_ref, gn_ref, xo_ref = refs[n_src:n_src + 4]
    r = _rms(y_ref[...].astype(F32), gp_ref[...])

    def finish(x_ref):
        xn = x_ref[...] + r
        xo_ref[...] = xn
        if want_h:
            refs[n_src + 4][...] = _rms(xn, gn_ref[...]).astype(BF16)

    if n_src == 1:
        finish(x_refs[0])
    else:
        i = pl.program_id(0)
        pl.when(i < nb_first)(lambda: finish(x_refs[0]))
        pl.when(i >= nb_first)(lambda: finish(x_refs[1]))


def _post(xs, y, g_post, g_next, tr):
    m = y.shape[0]
    want_h = g_next is not None
    row = pl.BlockSpec((tr, D_MODEL), lambda i: (i, 0))
    vec = pl.BlockSpec((1, D_MODEL), lambda i: (0, 0))
    nb_first = xs[0].shape[0] // tr
    x_specs = list(_two_source_specs(tr, nb_first)) if len(xs) == 2 else [row]
    out_shape = [jax.ShapeDtypeStruct((m, D_MODEL), F32)]
    out_specs = [row]
    if want_h:
        out_shape.append(jax.ShapeDtypeStruct((m, D_MODEL), BF16))
        out_specs.append(row)
    gn = (g_next if want_h else g_post).reshape(1, D_MODEL)
    res = pl.pallas_call(
        functools.partial(_post_kernel, nb_first=nb_first, n_src=len(xs), want_h=want_h), grid=(m // tr,),
        in_specs=x_specs + [row, vec, vec], out_specs=out_specs, out_shape=out_shape,
        compiler_params=_cparams(("arbitrary",)))(*xs, y, g_post.reshape(1, D_MODEL), gn)
    return (res[0], res[1]) if want_h else (res[0], None)


def _post_split_kernel(x_ref, y_ref, gp_ref, oa_ref, ob_ref, *, nb_first):
    i = pl.program_id(0)
    xn = x_ref[...] + _rms(y_ref[...].astype(F32), gp_ref[...])

    @pl.when(i < nb_first)
    def _():
        oa_ref[...] = xn

    @pl.when(i >= nb_first)
    def _():
        ob_ref[...] = xn


def _post_split(x, y, g_post, rows_first, tr):
    m = x.shape[0]
    row = pl.BlockSpec((tr, D_MODEL), lambda i: (i, 0))
    vec = pl.BlockSpec((1, D_MODEL), lambda i: (0, 0))
    oa, ob = _two_source_specs(tr, rows_first // tr)
    return pl.pallas_call(
        functools.partial(_post_split_kernel, nb_first=rows_first // tr), grid=(m // tr,),
        in_specs=[row, row, vec], out_specs=[oa, ob],
        out_shape=[jax.ShapeDtypeStruct((rows_first, D_MODEL), F32),
                   jax.ShapeDtypeStruct((m - rows_first, D_MODEL), F32)],
        compiler_params=_cparams(("arbitrary",)))(x, y, g_post.reshape(1, D_MODEL))


def _state_view(s_all, fresh_layer):
    return s_all if fresh_layer is None else s_all.at[fresh_layer]


def _zero_other_slots(s_all, fresh_layer):
    if fresh_layer is not None:
        for d in range(s_all.shape[0]):
            if d != fresh_layer:
                s_all[d] = jnp.zeros(s_all.shape[1:], s_all.dtype)


def _gdn_prep_kernel(a_ref, b_ref, alog_ref, dt_ref, gc_ref, beta_ref, *, prompt_blocks):
    i = pl.program_id(0)
    x = a_ref[...] + dt_ref[...]
    softplus = jnp.maximum(x, 0.0) + jnp.log1p(jnp.exp(-jnp.abs(x)))
    g = -jnp.exp(alog_ref[...]) * softplus
    chunk = jnp.where(i < prompt_blocks, MIX_CHUNK, 8)
    rmod = lax.broadcasted_iota(jnp.int32, g.shape, 0) & (chunk - 1)
    s = 1
    while s < MIX_CHUNK:
        g = g + jnp.where(rmod >= s, pltpu.roll(g, s, axis=0), 0.0)
        s *= 2
    gc_ref[...] = g
    beta_ref[...] = jax.nn.sigmoid(b_ref[...])


def _gdn_prep(ab, a_log, dt_bias, tr, prompt_rows):
    m = ab.shape[0]
    pad = lambda v: jnp.pad(v.astype(F32), (0, 128 - H_A)).reshape(1, 128)
    blk = lambda c: pl.BlockSpec((tr, 128), lambda i: (i, c))
    vec = pl.BlockSpec((1, 128), lambda i: (0, 0))
    return pl.pallas_call(
        functools.partial(_gdn_prep_kernel, prompt_blocks=prompt_rows // tr), grid=(m // tr,),
        in_specs=[blk(0), blk(1), vec, vec], out_specs=[blk(0), blk(0)],
        out_shape=[jax.ShapeDtypeStruct((m, 128), F32)] * 2,
        compiler_params=_cparams(("parallel",)))(ab, ab, pad(a_log), pad(dt_bias))


def _gdn_kernel(x_ref, z_ref, gcol_ref, bcol_ref, grow_ref, s0_ref, buf_ref, cw_ref, ng_ref, *rest, c, group, nseq,
                fresh_layer):
    o_ref, s_all, tail_ref, xc_ref = rest[-4:]
    s_ref = _state_view(s_all, fresh_layer)

    @pl.when(pl.program_id(1) == 0)
    def _():
        _zero_other_slots(s_all, fresh_layer)
        s_ref[...] = s0_ref[...]
        xc_ref[:, :8, :] = buf_ref[...]

    for q in range(nseq):
        xc_ref[q, 8:, :] = x_ref[q * c:(q + 1) * c, :]
        tail_ref[q] = x_ref[(q + 1) * c - 8:(q + 1) * c, :]

    def conv(q, lo):
        xcol = xc_ref[q, :, lo:lo + 128]
        acc = xcol[8:] * cw_ref[CONV_W - 1:CONV_W, lo:lo + 128]
        for s in range(1, CONV_W):
            acc = acc + pltpu.roll(xcol, s, axis=0)[8:] * cw_ref[CONV_W - 1 - s:CONV_W - s, lo:lo + 128]
        return acc * _sigmoid(acc)

    def l2n(t, scale=1.0):
        return t * (lax.rsqrt(jnp.sum(t * t, axis=-1, keepdims=True) + EPS) * scale)

    ii = lax.broadcasted_iota(jnp.int32, (c, c), 0)
    jj = lax.broadcasted_iota(jnp.int32, (c, c), 1)
    tri = ii >= jj
    strict = ii > jj
    ng = ng_ref[...]
    hk = H_A * DK_A
    for h0 in range(0, H_A, group):
        st = []
        for q in range(nseq):
            rows = slice(q * c, (q + 1) * c)
            for h in range(h0, h0 + group):
                lo = h * DK_A
                qh = l2n(conv(q, lo), DK_A ** -0.5)
                k = l2n(conv(q, hk + lo))
                v = conv(q, 2 * hk + lo)
                gc = gcol_ref[rows, h:h + 1]
                beta = bcol_ref[rows, h:h + 1]
                gr = grow_ref[q, 0, h:h + 1, :]
                glast = gr[:, c - 1:c]
                decay = jnp.exp(jnp.where(tri, gc - gr, -jnp.inf))
                egc = jnp.exp(gc)
                kb = k * beta
                m = lax.dot_general(jnp.concatenate([kb, qh], axis=0).astype(BF16), k.astype(BF16),
                                    (((1,), (1,)), ((), ())), preferred_element_type=F32)
                s = s_ref[q, h]
                sq = _bdot(jnp.concatenate([kb * egc, qh * egc], axis=0), s)
                st.append(dict(q=q, h=h, rows=rows, m=m, sq=sq, s=s, decay=decay, vb=v * beta,
                               kd=(k * jnp.exp(glast - gc)).astype(BF16), cd=jnp.exp(glast)))
        for d in st:
            d["p"] = (-jnp.where(strict, d["m"][:c] * d["decay"], 0.0)).astype(BF16)
            d["attn"] = (d["m"][c:] * d["decay"]).astype(BF16)
            d["x"] = d["vb"] - d["sq"][:c]
        n_sq = 1
        while n_sq < c:
            last = 2 * n_sq >= c
            for d in st:
                d["x"] = d["x"] + jnp.dot(d["p"], d["x"].astype(BF16), preferred_element_type=F32)
            if not last:
                for d in st:
                    d["p"] = jnp.dot(d["p"], d["p"], preferred_element_type=F32).astype(BF16)
            n_sq *= 2
        outs = {}
        for d in st:
            xb = d["x"].astype(BF16)
            o = d["sq"][c:] + jnp.dot(d["attn"], xb, preferred_element_type=F32)
            s_ref[d["q"], d["h"]] = d["s"] * d["cd"] + lax.dot_general(
                d["kd"], xb, (((0,), (0,)), ((), ())), preferred_element_type=F32)
            lo = d["h"] * DV_A
            o = o * lax.rsqrt(jnp.mean(o * o, axis=-1, keepdims=True) + EPS) * ng
            z = z_ref[d["rows"], lo:lo + DV_A]
            outs[d["q"], d["h"]] = o * (z * _sigmoid(z))
        for h in range(h0, h0 + group):
            o = outs[0, h] if nseq == 1 else jnp.concatenate([outs[q, h] for q in range(nseq)], axis=0)
            o_ref[:, h * DV_A:(h + 1) * DV_A] = o.astype(o_ref.dtype)
    xc_ref[:, :8, :] = xc_ref[:, c:c + 8, :]


def _state_out(stack, depth, layer_out, n, nseq, inner):
    if stack is None:
        return (pl.BlockSpec((depth, nseq) + inner, lambda i, j: (0, i, 0, 0, 0)),
                jax.ShapeDtypeStruct((depth, n) + inner, F32))
    return (pl.BlockSpec((None, nseq) + inner, lambda i, j: (layer_out, i, 0, 0, 0)),
            jax.ShapeDtypeStruct(stack.shape, F32))


def _gdn(proj, gc, beta, grow, s0, layer, stack, depth, layer_out, o_buf, buf8, conv_w, norm_g, row_off, n, l, c,
         group, nseq):
    nc = l // c
    assert nseq == 1 or nc == 1
    blk = nseq * c
    base = row_off // blk
    rows = lambda w, col: pl.BlockSpec((blk, w), lambda i, j: (base + i * nc + j, col))
    state_spec, state_shape = _state_out(stack, depth, layer_out, n, nseq, (H_A, DK_A, DV_A))
    aliased = ([stack] if stack is not None else []) + [o_buf]
    n_in = 9
    return pl.pallas_call(
        functools.partial(_gdn_kernel, c=c, group=group, nseq=nseq,
                          fresh_layer=layer_out if stack is None else None),
        grid=(n // nseq, nc),
        in_specs=[rows(CONV_DIM, 0), rows(BRANCH_W, COL_Z // BRANCH_W), rows(128, 0), rows(128, 0),
                  pl.BlockSpec((nseq, 1, H_A, c), lambda i, j: (i, j, 0, 0)),
                  pl.BlockSpec((None, nseq, H_A, DK_A, DV_A), lambda i, j: (layer, i, 0, 0, 0)),
                  pl.BlockSpec((nseq, 8, CONV_DIM), lambda i, j: (i, 0, 0)),
                  pl.BlockSpec((CONV_W, CONV_DIM), lambda i, j: (0, 0)),
                  pl.BlockSpec((1, DV_A), lambda i, j: (0, 0))]
        + [pl.BlockSpec(memory_space=pl.ANY)] * len(aliased),
        out_specs=[rows(BRANCH_W, 0), state_spec, pl.BlockSpec((nseq, 8, CONV_DIM), lambda i, j: (i, 0, 0))],
        out_shape=[jax.ShapeDtypeStruct(o_buf.shape, o_buf.dtype), state_shape,
                   jax.ShapeDtypeStruct((n, 8, CONV_DIM), F32)],
        scratch_shapes=[pltpu.VMEM((nseq, c + 8, CONV_DIM), F32)],
        input_output_aliases={n_in + len(aliased) - 1: 0, **({n_in: 1} if stack is not None else {})},
        compiler_params=_cparams(("parallel", "arbitrary")))(
            proj, proj, gc, beta, grow, s0, buf8, conv_w, norm_g.reshape(1, DV_A), *aliased)


def _ret_kernel(qk_ref, v_ref, gate_ref, cos_ref, sin_ref, r0_ref, *rest, c, nseq, fresh_layer):
    o_ref = rest[-2]
    r_ref = _state_view(rest[-1], fresh_layer)

    @pl.when(pl.program_id(1) == 0)
    def _():
        _zero_other_slots(rest[-1], fresh_layer)
        r_ref[...] = r0_ref[...]

    ii = lax.broadcasted_iota(jnp.int32, (c, c), 0)
    jj = lax.broadcasted_iota(jnp.int32, (c, c), 1)
    rel = (ii - jj).astype(F32)
    idx = lax.broadcasted_iota(jnp.int32, (c, 1), 0).astype(F32)
    even = (lax.broadcasted_iota(jnp.int32, (c, DK_B), 1) & 1) == 0
    cos = cos_ref[...]
    sin = sin_ref[...]

    def rot(x):
        swapped = jnp.where(even, pltpu.roll(x, DK_B - 1, axis=1), pltpu.roll(x, 1, axis=1))
        return x * cos + swapped * sin

    units = [(q, h) for q in range(nseq) for h in range(H_B)]
    lg = {h: math.log1p(-2.0 ** (-5.0 - h)) for h in range(H_B)}
    rows = {q: slice(q * c, (q + 1) * c) for q in range(nseq)}
    qs = {u: rot(qk_ref[rows[u[0]], u[1] * DK_B:(u[1] + 1) * DK_B]) for u in units}
    ks = {u: rot(qk_ref[rows[u[0]], (H_B + u[1]) * DK_B:(H_B + u[1] + 1) * DK_B]) * (DK_B ** -0.5)
          for u in units}
    vs = {u: v_ref[rows[u[0]], u[1] * DV_B:(u[1] + 1) * DV_B].astype(BF16) for u in units}
    rs = {u: r_ref[u] for u in units}
    scores = {u: _bdot_nt(qs[u], ks[u]) for u in units}
    cross = {u: _bdot(qs[u] * jnp.exp(lg[u[1]] * (idx + 1.0)), rs[u]) for u in units}
    for u in units:
        r_ref[u] = rs[u] * math.exp(lg[u[1]] * c) + _bdot_tn(ks[u] * jnp.exp(lg[u[1]] * (c - 1.0 - idx)), vs[u])
    inners = {u: _bdot(scores[u] * jnp.where(rel >= 0, jnp.exp(lg[u[1]] * jnp.maximum(rel, 0.0)), 0.0), vs[u])
              for u in units}
    outs = {}
    for u in units:
        o = inners[u] + cross[u]
        mu = jnp.mean(o, axis=-1, keepdims=True)
        d = o - mu
        var = jnp.mean(d * d, axis=-1, keepdims=True)
        o = d * lax.rsqrt(var + EPS)
        gate = gate_ref[rows[u[0]], u[1] * DV_B:(u[1] + 1) * DV_B]
        outs[u] = gate * _sigmoid(gate) * o
    for h in range(H_B):
        o = outs[0, h] if nseq == 1 else jnp.concatenate([outs[q, h] for q in range(nseq)], axis=0)
        o_ref[:, h * DV_B:(h + 1) * DV_B] = o.astype(o_ref.dtype)


def _rope_tables(pos):
    half = DK_B // 2
    inv_freq = 1.0 / (ROPE_BASE ** jnp.linspace(0.0, 1.0, half, dtype=F32))
    ang = pos.astype(F32)[:, None] * inv_freq[None, :]
    sin = jnp.sin(ang)
    cos = jnp.cos(ang)
    cos2 = jnp.stack([cos, cos], axis=-1).reshape(-1, DK_B)
    sin2 = jnp.stack([-sin, sin], axis=-1).reshape(-1, DK_B)
    return cos2, sin2


def _retention(proj, cos2, sin2, r0, layer, stack, depth, layer_out, o_buf, row_off, n, l, c, nseq):
    nc = l // c
    assert nseq == 1 or nc == 1
    blk = nseq * c
    base = row_off // blk
    col = lambda off: pl.BlockSpec((blk, BRANCH_W), lambda i, j: (base + i * nc + j, off // BRANCH_W))
    tab = pl.BlockSpec((c, DK_B), lambda i, j: (j, 0))
    state_spec, state_shape = _state_out(stack, depth, layer_out, n, nseq, (H_B, DK_B, DV_B))
    aliased = ([stack] if stack is not None else []) + [o_buf]
    n_in = 6
    return pl.pallas_call(
        functools.partial(_ret_kernel, c=c, nseq=nseq, fresh_layer=layer_out if stack is None else None),
        grid=(n // nseq, nc),
        in_specs=[col(COL_QKB), col(COL_VB), col(COL_GB), tab, tab,
                  pl.BlockSpec((None, nseq, H_B, DK_B, DV_B), lambda i, j: (layer, i, 0, 0, 0))]
        + [pl.BlockSpec(memory_space=pl.ANY)] * len(aliased),
        out_specs=[col(0), state_spec],
        out_shape=[jax.ShapeDtypeStruct(o_buf.shape, o_buf.dtype), state_shape],
        input_output_aliases={n_in + len(aliased) - 1: 0, **({n_in: 1} if stack is not None else {})},
        compiler_params=_cparams(("parallel", "arbitrary")))(proj, proj, proj, cos2, sin2, r0, *aliased)


def _gelu(x):
    return 0.5 * x * (1.0 + lax.erf(x * (2.0 ** -0.5)))


def _gmlp_kernel(u_ref, v_ref, w_ref, bcol_ref, lng_ref, lnb_ref, obuf_ref, o_ref, *vrows_ref, seq):
    u32 = _gelu(u_ref[...])
    v32 = _gelu(v_ref[...])
    mu = jnp.mean(v32, axis=-1, keepdims=True)
    d = v32 - mu
    var = jnp.mean(d * d, axis=-1, keepdims=True)
    vn = d * lax.rsqrt(var + EPS) * lng_ref[...] + lnb_ref[...]
    if vrows_ref:
        vrows_ref[0][...] = vn
    ii = lax.broadcasted_iota(jnp.int32, (CHUNK_C, CHUNK_C), 0)
    jj = lax.broadcasted_iota(jnp.int32, (CHUNK_C, CHUNK_C), 1)
    mask = (ii >= jj) & ((ii & -seq) == (jj & -seq))
    for g in range(G_C):
        w = jnp.where(mask, w_ref[g], 0.0)
        mixed = _bdot(w, vn[:, g * DG_C:(g + 1) * DG_C]) + bcol_ref[:, g:g + 1]
        o_ref[:, g * DG_C:(g + 1) * DG_C] = (u32[:, g * DG_C:(g + 1) * DG_C] * mixed).astype(o_ref.dtype)


def _gmlp(proj, w_tile, b_col, ln_g, ln_b, o_buf, row_off, rows, seq, want_vrows):
    base = row_off // CHUNK_C
    col = lambda off: pl.BlockSpec((CHUNK_C, BRANCH_W), lambda i: (base + i, off // BRANCH_W))
    vec = pl.BlockSpec((1, BRANCH_W), lambda i: (0, 0))
    out_shape = [jax.ShapeDtypeStruct(o_buf.shape, o_buf.dtype)]
    out_specs = [col(0)]
    if want_vrows:
        out_shape.append(jax.ShapeDtypeStruct((rows, BRANCH_W), F32))
        out_specs.append(pl.BlockSpec((CHUNK_C, BRANCH_W), lambda i: (i, 0)))
    res = pl.pallas_call(
        functools.partial(_gmlp_kernel, seq=seq), grid=(rows // CHUNK_C,),
        in_specs=[col(COL_U), col(COL_V),
                  pl.BlockSpec((G_C, CHUNK_C, CHUNK_C), lambda i: (0, 0, 0)),
                  pl.BlockSpec((CHUNK_C, G_C), lambda i: (0, 0)), vec, vec,
                  pl.BlockSpec(memory_space=pl.ANY)],
        out_specs=out_specs, out_shape=out_shape, input_output_aliases={6: 0},
        compiler_params=_cparams(("parallel",)))(
            proj, proj, w_tile, b_col, ln_g.reshape(1, BRANCH_W), ln_b.reshape(1, BRANCH_W), o_buf)
    return res if want_vrows else (res[0], None)


AB_COL0 = 4 * BRANCH_W
AB_COLS = 2 * H_A
WD_BLK, WBR_BLK, PACK_BLK = 64, 32, 128
FFN_BLK = 64


def _pack_kernel(w_ref, o_ref):
    o_ref[...] = w_ref[0].astype(BF16)


def _pack_src_row(j, tn):
    r = j * tn
    return r + jnp.where(r >= AB_COL0, AB_COLS, 0)


def _pack_w_in_t(w_in_t, layer, tn):
    k = w_in_t.shape[2]
    return pl.pallas_call(
        _pack_kernel, grid=(N_PROJ // tn,),
        in_specs=[pl.BlockSpec((pl.Element(1), pl.Element(tn), pl.Element(k)),
                               lambda j: (layer, pl.multiple_of(_pack_src_row(j, tn), 8), 0))],
        out_specs=pl.BlockSpec((tn, k), lambda j: (j, 0)),
        out_shape=jax.ShapeDtypeStruct((N_PROJ, k), BF16),
        compiler_params=_cparams(("parallel",)))(w_in_t)


def _ab_weights_t(w_in_t, layer):
    ab = w_in_t[layer, AB_COL0:AB_COL0 + AB_COLS, :]
    zeros = jnp.zeros((128 - H_A, ab.shape[1]), F32)
    return jnp.concatenate([ab[:H_A], zeros, ab[H_A:], zeros], axis=0)


def kernel(x_prompt, x_sample, state_gdn, state_conv, state_ret, w_in, conv_w, a_log, dt_bias, gdn_norm, w_s, b_s, ln_c_g, ln_c_b, w_br, w_o, g_pre_mix, g_post_mix, g_pre_ffn, g_post_ffn, w_ffn_gate, w_ffn_up, w_ffn_down):
    n_p, l_p, _ = x_prompt.shape
    n_s, l_s, _ = x_sample.shape
    depth = w_in.shape[0]
    rows_p, rows_s = n_p * l_p, n_s * l_s
    m = rows_p + rows_s
    xs = (x_prompt.reshape(rows_p, D_MODEL), x_sample.reshape(rows_s, D_MODEL))
    tr = 256

    cos_p, sin_p = _rope_tables(jnp.arange(l_p))
    cos_s, sin_s = _rope_tables(PAST_LEN + jnp.arange(l_s))
    zero_gdn = jnp.zeros((1, n_p, H_A, DK_A, DV_A), F32)
    zero_ret = jnp.zeros((1, n_p, H_B, DK_B, DV_B), F32)
    zero_buf8 = jnp.zeros((n_p, 8, CONV_DIM), F32)
    reps = CHUNK_C // l_s

    w_in_t = jnp.swapaxes(w_in, 1, 2)

    gdn_p = gdn_s = ret_p = ret_s = None
    outs = {k: [] for k in ("conv_p", "conv_s", "vrows")}
    h = _prenorm(xs[0], xs[1], g_pre_mix[0], tr)
    w_br_rows = w_br.reshape(depth, 3 * BRANCH_W, D_MODEL)
    w_main = _pack_w_in_t(w_in_t, 0, 512)
    for l in range(depth):
        side = [_SideCast(w_ffn_down, l, WD_BLK, D_FF_PAD // WD_BLK,
                          lambda b: jnp.minimum(b, D_FF // WD_BLK - 1) * WD_BLK, D_FF // WD_BLK),
                _SideCast(w_br_rows, l, WBR_BLK, 3 * BRANCH_W // WBR_BLK, lambda b: b * WBR_BLK,
                          3 * BRANCH_W // WBR_BLK)]
        if l + 1 < depth:
            side.append(_SideCast(w_in_t, l + 1, PACK_BLK, N_PROJ // PACK_BLK,
                                  lambda b: _pack_src_row(b, PACK_BLK), N_PROJ // PACK_BLK))
        proj, wd, w_br_l, *w_main = _matmul_nt(h, w_main, F32, 1024, 1024, side)
        w_main = w_main[0] if w_main else None
        ab = _matmul_nt(h, _ab_weights_t(w_in_t, l), F32, 1024, 256)[0]

        gc, beta = _gdn_prep(ab, a_log[l], dt_bias[l], 1024, rows_p)
        grow_p = gc[:rows_p, :H_A].reshape(n_p, l_p // MIX_CHUNK, MIX_CHUNK, H_A).transpose(0, 1, 3, 2)
        grow_s = gc[rows_p:, :H_A].reshape(n_s, 1, l_s, H_A).transpose(0, 1, 3, 2)
        buf8_s = jnp.pad(state_conv[l], ((0, 0), (8 - (CONV_W - 1), 0), (0, 0)))
        o_a = jnp.zeros((m, BRANCH_W), BF16)
        o_b = jnp.zeros((m, BRANCH_W), BF16)
        o_c = jnp.zeros((m, BRANCH_W), BF16)
        o_a, gdn_p, tail_p = _gdn(proj, gc, beta, grow_p, zero_gdn, 0, gdn_p, depth, l, o_a, zero_buf8, conv_w[l],
                                  gdn_norm[l], 0, n_p, l_p, MIX_CHUNK, H_A, 1)
        o_a, gdn_s, tail_s = _gdn(proj, gc, beta, grow_s, state_gdn, l, gdn_s, depth, l, o_a, buf8_s, conv_w[l],
                                  gdn_norm[l], rows_p, n_s, l_s, l_s, GDN_GROUP, GDN_SAMPLE_SEQS)
        outs["conv_p"].append(tail_p[:, 8 - (CONV_W - 1):])
        outs["conv_s"].append(tail_s[:, 8 - (CONV_W - 1):])

        o_b, ret_p = _retention(proj, cos_p, sin_p, zero_ret, 0, ret_p, depth, l, o_b, 0, n_p, l_p, RET_CHUNK, 1)
        o_b, ret_s = _retention(proj, cos_s, sin_s, state_ret, l, ret_s, depth, l, o_b, rows_p, n_s, l_s, l_s,
                                RET_SAMPLE_SEQS)

        o_c, _ = _gmlp(proj, w_s[l], b_s[l].T, ln_c_g[l], ln_c_b[l], o_c, 0, rows_p, CHUNK_C, False)
        w_tile_s = jnp.tile(w_s[l][:, :l_s, :l_s], (1, reps, reps))
        b_col_s = jnp.tile(b_s[l][:, :l_s].T, (reps, 1))
        o_c, vrows = _gmlp(proj, w_tile_s, b_col_s, ln_c_g[l], ln_c_b[l], o_c, rows_p, rows_s, l_s, True)
        outs["vrows"].append(vrows.reshape(n_s, l_s, BRANCH_W))

        merged = _merge(o_a, o_b, o_c, w_br_l.reshape(3, BRANCH_W, D_MODEL), proj, 1024, 512)
        ffn_side = [_SideCast(w, l, FFN_BLK, D_MODEL // FFN_BLK, lambda b: b * FFN_BLK, D_MODEL // FFN_BLK)
                    for w in (w_ffn_gate, w_ffn_up)]
        mix, wg, wu = _matmul_ws(merged, w_o, l, BF16, 1024, 512, ffn_side)
        x, h = _post(xs, mix, g_post_mix[l], g_pre_ffn[l], tr)

        f1 = _ffn_up(h, wg, wu, 1024, 512, D_FF_PAD)
        f = _matmul_kgrid(f1, wd, BF16, 1024, 1024, D_FF_PAD // 2)
        if l + 1 < depth:
            x, h = _post((x,), f, g_post_ffn[l], g_pre_mix[l + 1], tr)
            xs = (x,)
        else:
            y_p, y_s = _post_split(x, f, g_post_ffn[l], rows_p, tr)

    y_p = y_p.reshape(n_p, l_p, D_MODEL)
    y_s = y_s.reshape(n_s, l_s, D_MODEL)
    st = lambda k: jnp.stack(outs[k])
    return (y_p, y_s, gdn_p, gdn_s, st("conv_p"), st("conv_s"), ret_p, ret_s, st("vrows"))
```

```python
import functools
import math
from typing import Callable, NamedTuple

import jax
import jax.numpy as jnp
from jax import lax
from jax.experimental import pallas as pl
from jax.experimental.pallas import tpu as pltpu

F32 = jnp.float32
BF16 = jnp.bfloat16

D_MODEL = 4096
BRANCH_W = 2048
H_A, DK_A, DV_A = 16, 128, 128
CONV_W = 4
CONV_DIM = 6144
H_B, DK_B, DV_B = 8, 128, 256
G_C, DG_C, CHUNK_C = 8, 256, 128
D_FF = 11008
D_FF_PAD = 11264
EPS = 1e-6
ROPE_BASE = 10000.0
PAST_LEN = 16384
MIX_CHUNK = 64
RET_CHUNK = 128
GDN_GROUP = 8
GDN_SAMPLE_SEQS = 2
RET_SAMPLE_SEQS = 4

COL_QKV, COL_Z, COL_QKB, COL_VB, COL_GB, COL_U, COL_V, COL_GATE = (
    0, 6144, 8192, 10240, 12288, 14336, 16384, 18432)
N_PROJ = 30720
VMEM_LIMIT = 56 * 1024 * 1024
VMEM_LIMIT_HOST = 60 * 1024 * 1024


def _cparams(sem, vmem=VMEM_LIMIT):
    return pltpu.CompilerParams(dimension_semantics=sem, vmem_limit_bytes=vmem)


def _sigmoid(x):
    return 0.5 * jnp.tanh(0.5 * x) + 0.5


def _bdot(a, b):
    return jnp.dot(a.astype(BF16), b.astype(BF16), preferred_element_type=F32)


def _bdot_nt(a, b):
    return lax.dot_general(a.astype(BF16), b.astype(BF16), (((1,), (1,)), ((), ())),
                           preferred_element_type=F32)


def _bdot_tn(a, b):
    return lax.dot_general(a.astype(BF16), b.astype(BF16), (((0,), (0,)), ((), ())),
                           preferred_element_type=F32)


class _SideCast(NamedTuple):
    w: jax.Array
    layer: int
    blk: int
    n_blocks: int
    src_row: Callable
    n_copy: int


def _side_specs(side, steps_inner):
    def blk_idx(job, g0, g1):
        return jnp.minimum(g0 * steps_inner + g1, job.n_blocks - 1)

    in_specs = [pl.BlockSpec((pl.Element(1), pl.Element(job.blk), pl.Element(job.w.shape[2])),
                             lambda g0, g1, job=job: (job.layer,
                                                      pl.multiple_of(job.src_row(blk_idx(job, g0, g1)), 8), 0))
                for job in side]
    out_specs = [pl.BlockSpec((job.blk, job.w.shape[2]), lambda g0, g1, job=job: (blk_idx(job, g0, g1), 0))
                 for job in side]
    out_shape = [jax.ShapeDtypeStruct((job.n_blocks * job.blk, job.w.shape[2]), BF16) for job in side]
    return in_specs, out_specs, out_shape


def _run_side_casts(side, steps_inner, w_refs, o_refs):
    t = pl.program_id(0) * steps_inner + pl.program_id(1)
    for job, w_ref, so_ref in zip(side, w_refs, o_refs):
        v = w_ref[0].astype(BF16)
        so_ref[...] = v if job.n_copy >= job.n_blocks else jnp.where(t < job.n_copy, v, jnp.zeros_like(v))


def _mm_nt_kernel(a_ref, bt_ref, *refs, side, steps_inner):
    n_side = len(side)
    o_ref = refs[n_side]
    _run_side_casts(side, steps_inner, refs[:n_side], refs[n_side + 1:])
    o_ref[...] = lax.dot_general(a_ref[...], bt_ref[...].astype(BF16), (((1,), (1,)), ((), ())),
                                 preferred_element_type=F32).astype(o_ref.dtype)


def _matmul_nt(a, bt, out_dtype, tm, tn, side=()):
    m, k = a.shape
    n = bt.shape[0]
    nj = n // tn
    assert all(job.n_blocks <= (m // tm) * nj for job in side)
    side_in, side_out, side_shape = _side_specs(side, nj)
    return pl.pallas_call(
        functools.partial(_mm_nt_kernel, side=tuple(side), steps_inner=nj), grid=(m // tm, nj),
        in_specs=[pl.BlockSpec((tm, k), lambda i, j: (i, 0)),
                  pl.BlockSpec((tn, k), lambda i, j: (j, 0))] + side_in,
        out_specs=[pl.BlockSpec((tm, tn), lambda i, j: (i, j))] + side_out,
        out_shape=[jax.ShapeDtypeStruct((m, n), out_dtype)] + side_shape,
        compiler_params=_cparams(("arbitrary", "arbitrary"), VMEM_LIMIT_HOST if side else VMEM_LIMIT))(
            a, bt, *[job.w for job in side])


def _mm_ws_kernel(a_ref, b_ref, *refs, side, steps_inner):
    n_side = len(side)
    o_ref, bw_ref = refs[n_side], refs[-1]

    @pl.when(pl.program_id(1) == 0)
    def _():
        bw_ref[...] = b_ref[...].astype(BF16)

    o_ref[...] = jnp.dot(a_ref[...], bw_ref[...], preferred_element_type=F32).astype(o_ref.dtype)
    _run_side_casts(side, steps_inner, refs[:n_side], refs[n_side + 1:-1])


def _matmul_ws(a, b, layer, out_dtype, tm, tn, side=()):
    m, k = a.shape
    n = b.shape[2]
    ni = m // tm
    assert all(job.n_blocks <= (n // tn) * ni for job in side)
    side_in, side_out, side_shape = _side_specs(side, ni)
    return pl.pallas_call(
        functools.partial(_mm_ws_kernel, side=tuple(side), steps_inner=ni), grid=(n // tn, ni),
        in_specs=[pl.BlockSpec((tm, k), lambda j, i: (i, 0)),
                  pl.BlockSpec((None, k, tn), lambda j, i: (layer, 0, j))] + side_in,
        out_specs=[pl.BlockSpec((tm, tn), lambda j, i: (i, j))] + side_out,
        out_shape=[jax.ShapeDtypeStruct((m, n), out_dtype)] + side_shape,
        scratch_shapes=[pltpu.VMEM((k, tn), BF16)],
        compiler_params=_cparams(("arbitrary", "arbitrary")))(a, b, *[job.w for job in side])


def _mm_acc_kernel(a_ref, b_ref, o_ref, acc_ref):
    k = pl.program_id(2)
    last = pl.num_programs(2) - 1

    @pl.when(k == 0)
    def _():
        acc_ref[...] = jnp.dot(a_ref[...], b_ref[...], preferred_element_type=F32)

    @pl.when((k > 0) & (k < last))
    def _():
        acc_ref[...] += jnp.dot(a_ref[...], b_ref[...], preferred_element_type=F32)

    @pl.when(k == last)
    def _():
        o_ref[...] = (acc_ref[...] + jnp.dot(a_ref[...], b_ref[...], preferred_element_type=F32)).astype(o_ref.dtype)


def _matmul_kgrid(a, b, out_dtype, tm, tn, tk):
    m, k = a.shape
    n = b.shape[1]
    assert k // tk >= 2
    return pl.pallas_call(
        _mm_acc_kernel, grid=(m // tm, n // tn, k // tk),
        in_specs=[pl.BlockSpec((tm, tk), lambda i, j, kk: (i, kk)),
                  pl.BlockSpec((tk, tn), lambda i, j, kk: (kk, j))],
        out_specs=pl.BlockSpec((tm, tn), lambda i, j, kk: (i, j)),
        out_shape=jax.ShapeDtypeStruct((m, n), out_dtype),
        scratch_shapes=[pltpu.VMEM((tm, tn), F32)],
        compiler_params=_cparams(("parallel", "parallel", "arbitrary")))(a, b)


def _ffn_up_kernel(h_ref, wg_ref, wu_ref, o_ref, *, tn, n_valid):
    h = h_ref[...]
    g = jnp.dot(h, wg_ref[...], preferred_element_type=F32)
    u = jnp.dot(h, wu_ref[...], preferred_element_type=F32)
    col = pl.program_id(1) * tn + lax.broadcasted_iota(jnp.int32, g.shape, 1)
    o_ref[...] = jnp.where(col < n_valid, g * _sigmoid(g) * u, 0.0).astype(o_ref.dtype)


def _ffn_up(h, wg, wu, tm, tn, n_out):
    m, k = h.shape
    n = wg.shape[1]
    wspec = pl.BlockSpec((k, tn), lambda i, j: (0, j))
    return pl.pallas_call(
        functools.partial(_ffn_up_kernel, tn=tn, n_valid=n), grid=(m // tm, n_out // tn),
        in_specs=[pl.BlockSpec((tm, k), lambda i, j: (i, 0)), wspec, wspec],
        out_specs=pl.BlockSpec((tm, tn), lambda i, j: (i, j)),
        out_shape=jax.ShapeDtypeStruct((m, n_out), BF16),
        compiler_params=_cparams(("parallel", "parallel")))(h, wg, wu)


def _merge_kernel(oa_ref, ob_ref, oc_ref, w_ref, ga_ref, gb_ref, gc_ref, o_ref):
    acc = _sigmoid(ga_ref[...]) * jnp.dot(oa_ref[...], w_ref[0], preferred_element_type=F32)
    acc = acc + _sigmoid(gb_ref[...]) * jnp.dot(ob_ref[...], w_ref[1], preferred_element_type=F32)
    acc = acc + _sigmoid(gc_ref[...]) * jnp.dot(oc_ref[...], w_ref[2], preferred_element_type=F32)
    o_ref[...] = acc.astype(o_ref.dtype)


def _merge(o_a, o_b, o_c, w_br, proj, tm, tn):
    m = o_a.shape[0]
    gate_blk0 = COL_GATE // tn
    per_branch = D_MODEL // tn
    o_spec = pl.BlockSpec((tm, BRANCH_W), lambda i, j: (i, 0))
    gate = lambda b: pl.BlockSpec((tm, tn), lambda i, j: (i, gate_blk0 + b * per_branch + j))
    return pl.pallas_call(
        _merge_kernel, grid=(m // tm, D_MODEL // tn),
        in_specs=[o_spec, o_spec, o_spec,
                  pl.BlockSpec((3, BRANCH_W, tn), lambda i, j: (0, 0, j)),
                  gate(0), gate(1), gate(2)],
        out_specs=pl.BlockSpec((tm, tn), lambda i, j: (i, j)),
        out_shape=jax.ShapeDtypeStruct((m, D_MODEL), BF16),
        compiler_params=_cparams(("parallel", "parallel")))(o_a, o_b, o_c, w_br, proj, proj, proj)


def _rms(x, g):
    return x * lax.rsqrt(jnp.mean(x * x, axis=-1, keepdims=True) + EPS) * g


def _two_source_specs(tr, nb_first):
    first = pl.BlockSpec((tr, D_MODEL), lambda i: (jnp.minimum(i, nb_first - 1), 0))
    second = pl.BlockSpec((tr, D_MODEL), lambda i: (jnp.maximum(i - nb_first, 0), 0))
    return first, second


def _prenorm_kernel(xa_ref, xb_ref, g_ref, h_ref, *, nb_first):
    i = pl.program_id(0)

    @pl.when(i < nb_first)
    def _():
        h_ref[...] = _rms(xa_ref[...], g_ref[...]).astype(h_ref.dtype)

    @pl.when(i >= nb_first)
    def _():
        h_ref[...] = _rms(xb_ref[...], g_ref[...]).astype(h_ref.dtype)


def _prenorm(xa, xb, g, tr):
    ma, mb = xa.shape[0], xb.shape[0]
    sa, sb = _two_source_specs(tr, ma // tr)
    return pl.pallas_call(
        functools.partial(_prenorm_kernel, nb_first=ma // tr), grid=((ma + mb) // tr,),
        in_specs=[sa, sb, pl.BlockSpec((1, D_MODEL), lambda i: (0, 0))],
        out_specs=pl.BlockSpec((tr, D_MODEL), lambda i: (i, 0)),
        out_shape=jax.ShapeDtypeStruct((ma + mb, D_MODEL), BF16),
        compiler_params=_cparams(("arbitrary",)))(xa, xb, g.reshape(1, D_MODEL))


def _post_kernel(*refs, nb_first, n_src, want_h):
    x_refs = refs[:n_src]
    y_ref, gp_ref, gn_ref, xo_ref = refs[n_src:n_src + 4]
    r = _rms(y_ref[...].astype(F32), gp_ref[...])

    def finish(x_ref):
        xn = x_ref[...] + r
        xo_ref[...] = xn
        if want_h:
            refs[n_src + 4][...] = _rms(xn, gn_ref[...]).astype(BF16)

    if n_src == 1:
        finish(x_refs[0])
    else:
        i = pl.program_id(0)
        pl.when(i < nb_first)(lambda: finish(x_refs[0]))
        pl.when(i >= nb_first)(lambda: finish(x_refs[1]))


def _post(xs, y, g_post, g_next, tr):
    m = y.shape[0]
    want_h = g_next is not None
    row = pl.BlockSpec((tr, D_MODEL), lambda i: (i, 0))
    vec = pl.BlockSpec((1, D_MODEL), lambda i: (0, 0))
    nb_first = xs[0].shape[0] // tr
    x_specs = list(_two_source_specs(tr, nb_first)) if len(xs) == 2 else [row]
    out_shape = [jax.ShapeDtypeStruct((m, D_MODEL), F32)]
    out_specs = [row]
    if want_h:
        out_shape.append(jax.ShapeDtypeStruct((m, D_MODEL), BF16))
        out_specs.append(row)
    gn = (g_next if want_h else g_post).reshape(1, D_MODEL)
    res = pl.pallas_call(
        functools.partial(_post_kernel, nb_first=nb_first, n_src=len(xs), want_h=want_h), grid=(m // tr,),
        in_specs=x_specs + [row, vec, vec], out_specs=out_specs, out_shape=out_shape,
        compiler_params=_cparams(("arbitrary",)))(*xs, y, g_post.reshape(1, D_MODEL), gn)
    return (res[0], res[1]) if want_h else (res[0], None)


def _post_split_kernel(x_ref, y_ref, gp_ref, oa_ref, ob_ref, *, nb_first):
    i = pl.program_id(0)
    xn = x_ref[...] + _rms(y_ref[...].astype(F32), gp_ref[...])

    @pl.when(i < nb_first)
    def _():
        oa_ref[...] = xn

    @pl.when(i >= nb_first)
    def _():
        ob_ref[...] = xn


def _post_split(x, y, g_post, rows_first, tr):
    m = x.shape[0]
    row = pl.BlockSpec((tr, D_MODEL), lambda i: (i, 0))
    vec = pl.BlockSpec((1, D_MODEL), lambda i: (0, 0))
    oa, ob = _two_source_specs(tr, rows_first // tr)
    return pl.pallas_call(
        functools.partial(_post_split_kernel, nb_first=rows_first // tr), grid=(m // tr,),
        in_specs=[row, row, vec], out_specs=[oa, ob],
        out_shape=[jax.ShapeDtypeStruct((rows_first, D_MODEL), F32),
                   jax.ShapeDtypeStruct((m - rows_first, D_MODEL), F32)],
        compiler_params=_cparams(("arbitrary",)))(x, y, g_post.reshape(1, D_MODEL))


def _state_view(s_all, fresh_layer):
    return s_all if fresh_layer is None else s_all.at[fresh_layer]


def _zero_other_slots(s_all, fresh_layer):
    if fresh_layer is not None:
        for d in range(s_all.shape[0]):
            if d != fresh_layer:
                s_all[d] = jnp.zeros(s_all.shape[1:], s_all.dtype)


def _gdn_prep_kernel(a_ref, b_ref, alog_ref, dt_ref, gc_ref, beta_ref, *, prompt_blocks):
    i = pl.program_id(0)
    x = a_ref[...] + dt_ref[...]
    softplus = jnp.maximum(x, 0.0) + jnp.log1p(jnp.exp(-jnp.abs(x)))
    g = -jnp.exp(alog_ref[...]) * softplus
    chunk = jnp.where(i < prompt_blocks, MIX_CHUNK, 8)
    rmod = lax.broadcasted_iota(jnp.int32, g.shape, 0) & (chunk - 1)
    s = 1
    while s < MIX_CHUNK:
        g = g + jnp.where(rmod >= s, pltpu.roll(g, s, axis=0), 0.0)
        s *= 2
    gc_ref[...] = g
    beta_ref[...] = jax.nn.sigmoid(b_ref[...])


def _gdn_prep(ab, a_log, dt_bias, tr, prompt_rows):
    m = ab.shape[0]
    pad = lambda v: jnp.pad(v.astype(F32), (0, 128 - H_A)).reshape(1, 128)
    blk = lambda c: pl.BlockSpec((tr, 128), lambda i: (i, c))
    vec = pl.BlockSpec((1, 128), lambda i: (0, 0))
    return pl.pallas_call(
        functools.partial(_gdn_prep_kernel, prompt_blocks=prompt_rows // tr), grid=(m // tr,),
        in_specs=[blk(0), blk(1), vec, vec], out_specs=[blk(0), blk(0)],
        out_shape=[jax.ShapeDtypeStruct((m, 128), F32)] * 2,
        compiler_params=_cparams(("parallel",)))(ab, ab, pad(a_log), pad(dt_bias))


def _gdn_kernel(x_ref, z_ref, gcol_ref, bcol_ref, grow_ref, s0_ref, buf_ref, cw_ref, ng_ref, *rest, c, group, nseq,
                fresh_layer, n_alias, side, steps_inner):
    n_side = len(side)
    o_ref, s_all, tail_ref = rest[n_alias + n_side:n_alias + n_side + 3]
    xc_ref = rest[-1]
    s_ref = _state_view(s_all, fresh_layer)

    @pl.when(pl.program_id(1) == 0)
    def _():
        _zero_other_slots(s_all, fresh_layer)
        s_ref[...] = s0_ref[...]
        xc_ref[:, :8, :] = buf_ref[...]

    for q in range(nseq):
        xc_ref[q, 8:, :] = x_ref[q * c:(q + 1) * c, :]
        tail_ref[q] = x_ref[(q + 1) * c - 8:(q + 1) * c, :]

    def conv(q, lo):
        xcol = xc_ref[q, :, lo:lo + 128]
        acc = xcol[8:] * cw_ref[CONV_W - 1:CONV_W, lo:lo + 128]
        for s in range(1, CONV_W):
            acc = acc + pltpu.roll(xcol, s, axis=0)[8:] * cw_ref[CONV_W - 1 - s:CONV_W - s, lo:lo + 128]
        return acc * _sigmoid(acc)

    def l2n(t, scale=1.0):
        return t * (lax.rsqrt(jnp.sum(t * t, axis=-1, keepdims=True) + EPS) * scale)

    ii = lax.broadcasted_iota(jnp.int32, (c, c), 0)
    jj = lax.broadcasted_iota(jnp.int32, (c, c), 1)
    tri = ii >= jj
    strict = ii > jj
    ng = ng_ref[...]
    hk = H_A * DK_A
    for h0 in range(0, H_A, group):
        st = []
        for q in range(nseq):
            rows = slice(q * c, (q + 1) * c)
            for h in range(h0, h0 + group):
                lo = h * DK_A
                qh = l2n(conv(q, lo), DK_A ** -0.5)
                k = l2n(conv(q, hk + lo))
                v = conv(q, 2 * hk + lo)
                gc = gcol_ref[rows, h:h + 1]
                beta = bcol_ref[rows, h:h + 1]
                gr = grow_ref[q, 0, h:h + 1, :]
                glast = gr[:, c - 1:c]
                decay = jnp.exp(jnp.where(tri, gc - gr, -jnp.inf))
                egc = jnp.exp(gc)
                kb = k * beta
                m = lax.dot_general(jnp.concatenate([kb, qh], axis=0).astype(BF16), k.astype(BF16),
                                    (((1,), (1,)), ((), ())), preferred_element_type=F32)
                s = s_ref[q, h]
                sq = _bdot(jnp.concatenate([kb * egc, qh * egc], axis=0), s)
                st.append(dict(q=q, h=h, rows=rows, m=m, sq=sq, s=s, decay=decay, vb=v * beta,
                               kd=(k * jnp.exp(glast - gc)).astype(BF16), cd=jnp.exp(glast)))
        for d in st:
            d["p"] = (-jnp.where(strict, d["m"][:c] * d["decay"], 0.0)).astype(BF16)
            d["attn"] = (d["m"][c:] * d["decay"]).astype(BF16)
            d["x"] = d["vb"] - d["sq"][:c]
        n_sq = 1
        while n_sq < c:
            last = 2 * n_sq >= c
            for d in st:
                d["x"] = d["x"] + jnp.dot(d["p"], d["x"].astype(BF16), preferred_element_type=F32)
            if not last:
                for d in st:
                    d["p"] = jnp.dot(d["p"], d["p"], preferred_element_type=F32).astype(BF16)
            n_sq *= 2
        outs = {}
        for d in st:
            xb = d["x"].astype(BF16)
            o = d["sq"][c:] + jnp.dot(d["attn"], xb, preferred_element_type=F32)
            s_ref[d["q"], d["h"]] = d["s"] * d["cd"] + lax.dot_general(
                d["kd"], xb, (((0,), (0,)), ((), ())), preferred_element_type=F32)
            lo = d["h"] * DV_A
            o = o * lax.rsqrt(jnp.mean(o * o, axis=-1, keepdims=True) + EPS) * ng
            z = z_ref[d["rows"], lo:lo + DV_A]
            outs[d["q"], d["h"]] = o * (z * _sigmoid(z))
        for h in range(h0, h0 + group):
            o = outs[0, h] if nseq == 1 else jnp.concatenate([outs[q, h] for q in range(nseq)], axis=0)
            o_ref[:, h * DV_A:(h + 1) * DV_A] = o.astype(o_ref.dtype)
    xc_ref[:, :8, :] = xc_ref[:, c:c + 8, :]
    _run_side_casts(side, steps_inner, rest[n_alias:n_alias + n_side], rest[n_alias + n_side + 3:-1])


def _state_out(stack, depth, layer_out, n, nseq, inner):
    if stack is None:
        return (pl.BlockSpec((depth, nseq) + inner, lambda i, j: (0, i, 0, 0, 0)),
                jax.ShapeDtypeStruct((depth, n) + inner, F32))
    return (pl.BlockSpec((None, nseq) + inner, lambda i, j: (layer_out, i, 0, 0, 0)),
            jax.ShapeDtypeStruct(stack.shape, F32))


def _gdn(proj, gc, beta, grow, s0, layer, stack, depth, layer_out, o_buf, buf8, conv_w, norm_g, row_off, n, l, c,
         group, nseq, side=()):
    nc = l // c
    assert nseq == 1 or nc == 1
    blk = nseq * c
    base = row_off // blk
    rows = lambda w, col: pl.BlockSpec((blk, w), lambda i, j: (base + i * nc + j, col))
    state_spec, state_shape = _state_out(stack, depth, layer_out, n, nseq, (H_A, DK_A, DV_A))
    aliased = ([stack] if stack is not None else []) + [o_buf]
    n_in = 9
    assert all(job.n_blocks <= (n // nseq) * nc for job in side)
    side_in, side_out, side_shape = _side_specs(side, nc)
    return pl.pallas_call(
        functools.partial(_gdn_kernel, c=c, group=group, nseq=nseq,
                          fresh_layer=layer_out if stack is None else None,
                          n_alias=len(aliased), side=tuple(side), steps_inner=nc),
        grid=(n // nseq, nc),
        in_specs=[rows(CONV_DIM, 0), rows(BRANCH_W, COL_Z // BRANCH_W), rows(128, 0), rows(128, 0),
                  pl.BlockSpec((nseq, 1, H_A, c), lambda i, j: (i, j, 0, 0)),
                  pl.BlockSpec((None, nseq, H_A, DK_A, DV_A), lambda i, j: (layer, i, 0, 0, 0)),
                  pl.BlockSpec((nseq, 8, CONV_DIM), lambda i, j: (i, 0, 0)),
                  pl.BlockSpec((CONV_W, CONV_DIM), lambda i, j: (0, 0)),
                  pl.BlockSpec((1, DV_A), lambda i, j: (0, 0))]
        + [pl.BlockSpec(memory_space=pl.ANY)] * len(aliased) + side_in,
        out_specs=[rows(BRANCH_W, 0), state_spec, pl.BlockSpec((nseq, 8, CONV_DIM), lambda i, j: (i, 0, 0))]
        + side_out,
        out_shape=[jax.ShapeDtypeStruct(o_buf.shape, o_buf.dtype), state_shape,
                   jax.ShapeDtypeStruct((n, 8, CONV_DIM), F32)] + side_shape,
        scratch_shapes=[pltpu.VMEM((nseq, c + 8, CONV_DIM), F32)],
        input_output_aliases={n_in + len(aliased) - 1: 0, **({n_in: 1} if stack is not None else {})},
        compiler_params=_cparams(("arbitrary", "arbitrary")))(
            proj, proj, gc, beta, grow, s0, buf8, conv_w, norm_g.reshape(1, DV_A), *aliased,
            *[job.w for job in side])


def _ret_kernel(qk_ref, v_ref, gate_ref, cos_ref, sin_ref, r0_ref, *rest, c, nseq, fresh_layer):
    o_ref = rest[-2]
    r_ref = _state_view(rest[-1], fresh_layer)

    @pl.when(pl.program_id(1) == 0)
    def _():
        _zero_other_slots(rest[-1], fresh_layer)
        r_ref[...] = r0_ref[...]

    ii = lax.broadcasted_iota(jnp.int32, (c, c), 0)
    jj = lax.broadcasted_iota(jnp.int32, (c, c), 1)
    rel = (ii - jj).astype(F32)
    idx = lax.broadcasted_iota(jnp.int32, (c, 1), 0).astype(F32)
    even = (lax.broadcasted_iota(jnp.int32, (c, DK_B), 1) & 1) == 0
    cos = cos_ref[...]
    sin = sin_ref[...]

    def rot(x):
        swapped = jnp.where(even, pltpu.roll(x, DK_B - 1, axis=1), pltpu.roll(x, 1, axis=1))
        return x * cos + swapped * sin

    units = [(q, h) for q in range(nseq) for h in range(H_B)]
    lg = {h: math.log1p(-2.0 ** (-5.0 - h)) for h in range(H_B)}
    rows = {q: slice(q * c, (q + 1) * c) for q in range(nseq)}
    qs = {u: rot(qk_ref[rows[u[0]], u[1] * DK_B:(u[1] + 1) * DK_B]) for u in units}
    ks = {u: rot(qk_ref[rows[u[0]], (H_B + u[1]) * DK_B:(H_B + u[1] + 1) * DK_B]) * (DK_B ** -0.5)
          for u in units}
    vs = {u: v_ref[rows[u[0]], u[1] * DV_B:(u[1] + 1) * DV_B].astype(BF16) for u in units}
    rs = {u: r_ref[u] for u in units}
    scores = {u: _bdot_nt(qs[u], ks[u]) for u in units}
    cross = {u: _bdot(qs[u] * jnp.exp(lg[u[1]] * (idx + 1.0)), rs[u]) for u in units}
    for u in units:
        r_ref[u] = rs[u] * math.exp(lg[u[1]] * c) + _bdot_tn(ks[u] * jnp.exp(lg[u[1]] * (c - 1.0 - idx)), vs[u])
    inners = {u: _bdot(scores[u] * jnp.where(rel >= 0, jnp.exp(lg[u[1]] * jnp.maximum(rel, 0.0)), 0.0), vs[u])
              for u in units}
    outs = {}
    for u in units:
        o = inners[u] + cross[u]
        mu = jnp.mean(o, axis=-1, keepdims=True)
        d = o - mu
        var = jnp.mean(d * d, axis=-1, keepdims=True)
        o = d * lax.rsqrt(var + EPS)
        gate = gate_ref[rows[u[0]], u[1] * DV_B:(u[1] + 1) * DV_B]
        outs[u] = gate * _sigmoid(gate) * o
    for h in range(H_B):
        o = outs[0, h] if nseq == 1 else jnp.concatenate([outs[q, h] for q in range(nseq)], axis=0)
        o_ref[:, h * DV_B:(h + 1) * DV_B] = o.astype(o_ref.dtype)


def _rope_tables(pos):
    half = DK_B // 2
    inv_freq = 1.0 / (ROPE_BASE ** jnp.linspace(0.0, 1.0, half, dtype=F32))
    ang = pos.astype(F32)[:, None] * inv_freq[None, :]
    sin = jnp.sin(ang)
    cos = jnp.cos(ang)
    cos2 = jnp.stack([cos, cos], axis=-1).reshape(-1, DK_B)
    sin2 = jnp.stack([-sin, sin], axis=-1).reshape(-1, DK_B)
    return cos2, sin2


def _retention(proj, cos2, sin2, r0, layer, stack, depth, layer_out, o_buf, row_off, n, l, c, nseq):
    nc = l // c
    assert nseq == 1 or nc == 1
    blk = nseq * c
    base = row_off // blk
    col = lambda off: pl.BlockSpec((blk, BRANCH_W), lambda i, j: (base + i * nc + j, off // BRANCH_W))
    tab = pl.BlockSpec((c, DK_B), lambda i, j: (j, 0))
    state_spec, state_shape = _state_out(stack, depth, layer_out, n, nseq, (H_B, DK_B, DV_B))
    aliased = ([stack] if stack is not None else []) + [o_buf]
    n_in = 6
    return pl.pallas_call(
        functools.partial(_ret_kernel, c=c, nseq=nseq, fresh_layer=layer_out if stack is None else None),
        grid=(n // nseq, nc),
        in_specs=[col(COL_QKB), col(COL_VB), col(COL_GB), tab, tab,
                  pl.BlockSpec((None, nseq, H_B, DK_B, DV_B), lambda i, j: (layer, i, 0, 0, 0))]
        + [pl.BlockSpec(memory_space=pl.ANY)] * len(aliased),
        out_specs=[col(0), state_spec],
        out_shape=[jax.ShapeDtypeStruct(o_buf.shape, o_buf.dtype), state_shape],
        input_output_aliases={n_in + len(aliased) - 1: 0, **({n_in: 1} if stack is not None else {})},
        compiler_params=_cparams(("parallel", "arbitrary")))(proj, proj, proj, cos2, sin2, r0, *aliased)


def _gelu(x):
    return 0.5 * x * (1.0 + lax.erf(x * (2.0 ** -0.5)))


def _gmlp_kernel(u_ref, v_ref, w_ref, bcol_ref, lng_ref, lnb_ref, obuf_ref, o_ref, *vrows_ref, seq):
    u32 = _gelu(u_ref[...])
    v32 = _gelu(v_ref[...])
    mu = jnp.mean(v32, axis=-1, keepdims=True)
    d = v32 - mu
    var = jnp.mean(d * d, axis=-1, keepdims=True)
    vn = d * lax.rsqrt(var + EPS) * lng_ref[...] + lnb_ref[...]
    if vrows_ref:
        vrows_ref[0][...] = vn
    ii = lax.broadcasted_iota(jnp.int32, (CHUNK_C, CHUNK_C), 0)
    jj = lax.broadcasted_iota(jnp.int32, (CHUNK_C, CHUNK_C), 1)
    mask = (ii >= jj) & ((ii & -seq) == (jj & -seq))
    for g in range(G_C):
        w = jnp.where(mask, w_ref[g], 0.0)
        mixed = _bdot(w, vn[:, g * DG_C:(g + 1) * DG_C]) + bcol_ref[:, g:g + 1]
        o_ref[:, g * DG_C:(g + 1) * DG_C] = (u32[:, g * DG_C:(g + 1) * DG_C] * mixed).astype(o_ref.dtype)


def _gmlp(proj, w_tile, b_col, ln_g, ln_b, o_buf, row_off, rows, seq, want_vrows):
    base = row_off // CHUNK_C
    col = lambda off: pl.BlockSpec((CHUNK_C, BRANCH_W), lambda i: (base + i, off // BRANCH_W))
    vec = pl.BlockSpec((1, BRANCH_W), lambda i: (0, 0))
    out_shape = [jax.ShapeDtypeStruct(o_buf.shape, o_buf.dtype)]
    out_specs = [col(0)]
    if want_vrows:
        out_shape.append(jax.ShapeDtypeStruct((rows, BRANCH_W), F32))
        out_specs.append(pl.BlockSpec((CHUNK_C, BRANCH_W), lambda i: (i, 0)))
    res = pl.pallas_call(
        functools.partial(_gmlp_kernel, seq=seq), grid=(rows // CHUNK_C,),
        in_specs=[col(COL_U), col(COL_V),
                  pl.BlockSpec((G_C, CHUNK_C, CHUNK_C), lambda i: (0, 0, 0)),
                  pl.BlockSpec((CHUNK_C, G_C), lambda i: (0, 0)), vec, vec,
                  pl.BlockSpec(memory_space=pl.ANY)],
        out_specs=out_specs, out_shape=out_shape, input_output_aliases={6: 0},
        compiler_params=_cparams(("parallel",)))(
            proj, proj, w_tile, b_col, ln_g.reshape(1, BRANCH_W), ln_b.reshape(1, BRANCH_W), o_buf)
    return res if want_vrows else (res[0], None)


AB_COL0 = 4 * BRANCH_W
AB_COLS = 2 * H_A
WD_BLK, WBR_BLK, PACK_BLK = 64, 32, 128
FFN_BLK = 64


def _pack_kernel(w_ref, o_ref):
    o_ref[...] = w_ref[0].astype(BF16)


def _pack_src_row(j, tn):
    r = j * tn
    return r + jnp.where(r >= AB_COL0, AB_COLS, 0)


def _pack_w_in_t(w_in_t, layer, tn):
    k = w_in_t.shape[2]
    return pl.pallas_call(
        _pack_kernel, grid=(N_PROJ // tn,),
        in_specs=[pl.BlockSpec((pl.Element(1), pl.Element(tn), pl.Element(k)),
                               lambda j: (layer, pl.multiple_of(_pack_src_row(j, tn), 8), 0))],
        out_specs=pl.BlockSpec((tn, k), lambda j: (j, 0)),
        out_shape=jax.ShapeDtypeStruct((N_PROJ, k), BF16),
        compiler_params=_cparams(("parallel",)))(w_in_t)


def _ab_weights_t(w_in_t, layer):
    ab = w_in_t[layer, AB_COL0:AB_COL0 + AB_COLS, :]
    zeros = jnp.zeros((128 - H_A, ab.shape[1]), F32)
    return jnp.concatenate([ab[:H_A], zeros, ab[H_A:], zeros], axis=0)


def kernel(x_prompt, x_sample, state_gdn, state_conv, state_ret, w_in, conv_w, a_log, dt_bias, gdn_norm, w_s, b_s, ln_c_g, ln_c_b, w_br, w_o, g_pre_mix, g_post_mix, g_pre_ffn, g_post_ffn, w_ffn_gate, w_ffn_up, w_ffn_down):
    n_p, l_p, _ = x_prompt.shape
    n_s, l_s, _ = x_sample.shape
    depth = w_in.shape[0]
    rows_p, rows_s = n_p * l_p, n_s * l_s
    m = rows_p + rows_s
    xs = (x_prompt.reshape(rows_p, D_MODEL), x_sample.reshape(rows_s, D_MODEL))
    tr = 256

    cos_p, sin_p = _rope_tables(jnp.arange(l_p))
    cos_s, sin_s = _rope_tables(PAST_LEN + jnp.arange(l_s))
    zero_gdn = jnp.zeros((1, n_p, H_A, DK_A, DV_A), F32)
    zero_ret = jnp.zeros((1, n_p, H_B, DK_B, DV_B), F32)
    zero_buf8 = jnp.zeros((n_p, 8, CONV_DIM), F32)
    reps = CHUNK_C // l_s

    w_in_t = jnp.swapaxes(w_in, 1, 2)

    gdn_p = gdn_s = ret_p = ret_s = None
    outs = {k: [] for k in ("conv_p", "conv_s", "vrows")}
    h = _prenorm(xs[0], xs[1], g_pre_mix[0], tr)
    w_br_rows = w_br.reshape(depth, 3 * BRANCH_W, D_MODEL)
    w_main = _pack_w_in_t(w_in_t, 0, 512)
    for l in range(depth):
        side = [_SideCast(w_ffn_down, l, WD_BLK, D_FF_PAD // WD_BLK,
                          lambda b: jnp.minimum(b, D_FF // WD_BLK - 1) * WD_BLK, D_FF // WD_BLK),
                _SideCast(w_br_rows, l, WBR_BLK, 3 * BRANCH_W // WBR_BLK, lambda b: b * WBR_BLK,
                          3 * BRANCH_W // WBR_BLK)]
        if l + 1 < depth:
            side.append(_SideCast(w_in_t, l + 1, PACK_BLK, N_PROJ // PACK_BLK,
                                  lambda b: _pack_src_row(b, PACK_BLK), N_PROJ // PACK_BLK))
        proj, wd, w_br_l, *w_main = _matmul_nt(h, w_main, F32, 1024, 1024, side)
        w_main = w_main[0] if w_main else None
        ab = _matmul_nt(h, _ab_weights_t(w_in_t, l), F32, 1024, 256)[0]

        gc, beta = _gdn_prep(ab, a_log[l], dt_bias[l], 1024, rows_p)
        grow_p = gc[:rows_p, :H_A].reshape(n_p, l_p // MIX_CHUNK, MIX_CHUNK, H_A).transpose(0, 1, 3, 2)
        grow_s = gc[rows_p:, :H_A].reshape(n_s, 1, l_s, H_A).transpose(0, 1, 3, 2)
        buf8_s = jnp.pad(state_conv[l], ((0, 0), (8 - (CONV_W - 1), 0), (0, 0)))
        o_a = jnp.zeros((m, BRANCH_W), BF16)
        o_b = jnp.zeros((m, BRANCH_W), BF16)
        o_c = jnp.zeros((m, BRANCH_W), BF16)
        ffn_side = [_SideCast(w, l, FFN_BLK, D_MODEL // FFN_BLK, lambda b: b * FFN_BLK, D_MODEL // FFN_BLK)
                    for w in (w_ffn_gate, w_ffn_up)]
        o_a, gdn_p, tail_p, wg, wu = _gdn(proj, gc, beta, grow_p, zero_gdn, 0, gdn_p, depth, l, o_a, zero_buf8,
                                          conv_w[l], gdn_norm[l], 0, n_p, l_p, MIX_CHUNK, H_A, 1, ffn_side)
        o_a, gdn_s, tail_s = _gdn(proj, gc, beta, grow_s, state_gdn, l, gdn_s, depth, l, o_a, buf8_s, conv_w[l],
                                  gdn_norm[l], rows_p, n_s, l_s, l_s, GDN_GROUP, GDN_SAMPLE_SEQS)
        outs["conv_p"].append(tail_p[:, 8 - (CONV_W - 1):])
        outs["conv_s"].append(tail_s[:, 8 - (CONV_W - 1):])

        o_b, ret_p = _retention(proj, cos_p, sin_p, zero_ret, 0, ret_p, depth, l, o_b, 0, n_p, l_p, RET_CHUNK, 1)
        o_b, ret_s = _retention(proj, cos_s, sin_s, state_ret, l, ret_s, depth, l, o_b, rows_p, n_s, l_s, l_s,
                                RET_SAMPLE_SEQS)

        o_c, _ = _gmlp(proj, w_s[l], b_s[l].T, ln_c_g[l], ln_c_b[l], o_c, 0, rows_p, CHUNK_C, False)
        w_tile_s = jnp.tile(w_s[l][:, :l_s, :l_s], (1, reps, reps))
        b_col_s = jnp.tile(b_s[l][:, :l_s].T, (reps, 1))
        o_c, vrows = _gmlp(proj, w_tile_s, b_col_s, ln_c_g[l], ln_c_b[l], o_c, rows_p, rows_s, l_s, True)
        outs["vrows"].append(vrows.reshape(n_s, l_s, BRANCH_W))

        merged = _merge(o_a, o_b, o_c, w_br_l.reshape(3, BRANCH_W, D_MODEL), proj, 1024, 512)
        mix = _matmul_ws(merged, w_o, l, BF16, 1024, 512)[0]
        x, h = _post(xs, mix, g_post_mix[l], g_pre_ffn[l], tr)

        f1 = _ffn_up(h, wg, wu, 1024, 512, D_FF_PAD)
        f = _matmul_kgrid(f1, wd, BF16, 1024, 1024, D_FF_PAD // 2)
        if l + 1 < depth:
            x, h = _post((x,), f, g_post_ffn[l], g_pre_mix[l + 1], tr)
            xs = (x,)
        else:
            y_p, y_s = _post_split(x, f, g_post_ffn[l], rows_p, tr)

    y_p = y_p.reshape(n_p, l_p, D_MODEL)
    y_s = y_s.reshape(n_s, l_s, D_MODEL)
    st = lambda k: jnp.stack(outs[k])
    return (y_p, y_s, gdn_p, gdn_s, st("conv_p"), st("conv_s"), ret_p, ret_s, st("vrows"))
```

```python
import functools
import math
from typing import Callable, NamedTuple

import jax
import jax.numpy as jnp
from jax import lax
from jax.experimental import pallas as pl
from jax.experimental.pallas import tpu as pltpu

F32 = jnp.float32
BF16 = jnp.bfloat16

D_MODEL = 4096
BRANCH_W = 2048
H_A, DK_A, DV_A = 16, 128, 128
CONV_W = 4
CONV_DIM = 6144
H_B, DK_B, DV_B = 8, 128, 256
G_C, DG_C, CHUNK_C = 8, 256, 128
D_FF = 11008
D_FF_PAD = 11264
EPS = 1e-6
ROPE_BASE = 10000.0
PAST_LEN = 16384
MIX_CHUNK = 64
RET_CHUNK = 256
GDN_GROUP = 8
GDN_SAMPLE_SEQS = 2
RET_SAMPLE_SEQS = 4

COL_QKV, COL_Z, COL_QKB, COL_VB, COL_GB, COL_U, COL_V, COL_GATE = (
    0, 6144, 8192, 10240, 12288, 14336, 16384, 18432)
N_PROJ = 30720
TM, TN, TN_HALF, ROW_TILE = 1024, 1024, 512, 256
LANES = 128
VMEM_LIMIT = 56 * 1024 * 1024
VMEM_LIMIT_HOST = 60 * 1024 * 1024


def _cparams(sem, vmem=VMEM_LIMIT):
    return pltpu.CompilerParams(dimension_semantics=sem, vmem_limit_bytes=vmem)


def _sigmoid(x):
    return 0.5 * jnp.tanh(0.5 * x) + 0.5


def _bdot(a, b):
    return jnp.dot(a.astype(BF16), b.astype(BF16), preferred_element_type=F32)


def _bdot_nt(a, b):
    return lax.dot_general(a.astype(BF16), b.astype(BF16), (((1,), (1,)), ((), ())),
                           preferred_element_type=F32)


def _bdot_tn(a, b):
    return lax.dot_general(a.astype(BF16), b.astype(BF16), (((0,), (0,)), ((), ())),
                           preferred_element_type=F32)


class _SideCast(NamedTuple):
    w: jax.Array
    layer: int
    blk: int
    n_blocks: int
    src_row: Callable
    n_copy: int


def _side_specs(side, steps_inner):
    def blk_idx(job, g0, g1):
        return jnp.minimum(g0 * steps_inner + g1, job.n_blocks - 1)

    in_specs = [pl.BlockSpec((pl.Element(1), pl.Element(job.blk), pl.Element(job.w.shape[2])),
                             lambda g0, g1, job=job: (job.layer,
                                                      pl.multiple_of(job.src_row(blk_idx(job, g0, g1)), 8), 0))
                for job in side]
    out_specs = [pl.BlockSpec((job.blk, job.w.shape[2]), lambda g0, g1, job=job: (blk_idx(job, g0, g1), 0))
                 for job in side]
    out_shape = [jax.ShapeDtypeStruct((job.n_blocks * job.blk, job.w.shape[2]), BF16) for job in side]
    return in_specs, out_specs, out_shape


def _run_side_casts(side, steps_inner, w_refs, o_refs):
    t = pl.program_id(0) * steps_inner + pl.program_id(1)
    for job, w_ref, so_ref in zip(side, w_refs, o_refs):
        v = w_ref[0].astype(BF16)
        so_ref[...] = v if job.n_copy >= job.n_blocks else jnp.where(t < job.n_copy, v, jnp.zeros_like(v))


def _mm_nt_kernel(a_ref, bt_ref, *refs, side, steps_inner):
    n_side = len(side)
    o_ref = refs[n_side]
    _run_side_casts(side, steps_inner, refs[:n_side], refs[n_side + 1:])
    o_ref[...] = lax.dot_general(a_ref[...], bt_ref[...].astype(BF16), (((1,), (1,)), ((), ())),
                                 preferred_element_type=F32).astype(o_ref.dtype)


def _matmul_nt(a, bt, out_dtype, tm, tn, side=()):
    m, k = a.shape
    n = bt.shape[0]
    nj = n // tn
    assert all(job.n_blocks <= (m // tm) * nj for job in side)
    side_in, side_out, side_shape = _side_specs(side, nj)
    return pl.pallas_call(
        functools.partial(_mm_nt_kernel, side=tuple(side), steps_inner=nj), grid=(m // tm, nj),
        in_specs=[pl.BlockSpec((tm, k), lambda i, j: (i, 0)),
                  pl.BlockSpec((tn, k), lambda i, j: (j, 0))] + side_in,
        out_specs=[pl.BlockSpec((tm, tn), lambda i, j: (i, j))] + side_out,
        out_shape=[jax.ShapeDtypeStruct((m, n), out_dtype)] + side_shape,
        compiler_params=_cparams(("arbitrary", "arbitrary"), VMEM_LIMIT_HOST if side else VMEM_LIMIT))(
            a, bt, *[job.w for job in side])


def _mm_ws_kernel(a_ref, b_ref, o_ref, bw_ref):
    @pl.when(pl.program_id(1) == 0)
    def _():
        bw_ref[...] = b_ref[...].astype(BF16)

    o_ref[...] = jnp.dot(a_ref[...], bw_ref[...], preferred_element_type=F32).astype(o_ref.dtype)


def _matmul_ws(a, b, layer, out_dtype, tm, tn):
    m, k = a.shape
    n = b.shape[2]
    return pl.pallas_call(
        _mm_ws_kernel, grid=(n // tn, m // tm),
        in_specs=[pl.BlockSpec((tm, k), lambda j, i: (i, 0)),
                  pl.BlockSpec((None, k, tn), lambda j, i: (layer, 0, j))],
        out_specs=pl.BlockSpec((tm, tn), lambda j, i: (i, j)),
        out_shape=jax.ShapeDtypeStruct((m, n), out_dtype),
        scratch_shapes=[pltpu.VMEM((k, tn), BF16)],
        compiler_params=_cparams(("arbitrary", "arbitrary")))(a, b)


def _mm_k2_kernel(a_ref, b_ref, o_ref, acc_ref):
    @pl.when(pl.program_id(2) == 0)
    def _():
        acc_ref[...] = jnp.dot(a_ref[...], b_ref[...], preferred_element_type=F32)

    @pl.when(pl.program_id(2) == 1)
    def _():
        o_ref[...] = (acc_ref[...] + jnp.dot(a_ref[...], b_ref[...], preferred_element_type=F32)).astype(o_ref.dtype)


def _matmul_k2(a, b, out_dtype, tm, tn):
    m, k = a.shape
    n = b.shape[1]
    tk = k // 2
    return pl.pallas_call(
        _mm_k2_kernel, grid=(m // tm, n // tn, 2),
        in_specs=[pl.BlockSpec((tm, tk), lambda i, j, kk: (i, kk)),
                  pl.BlockSpec((tk, tn), lambda i, j, kk: (kk, j))],
        out_specs=pl.BlockSpec((tm, tn), lambda i, j, kk: (i, j)),
        out_shape=jax.ShapeDtypeStruct((m, n), out_dtype),
        scratch_shapes=[pltpu.VMEM((tm, tn), F32)],
        compiler_params=_cparams(("parallel", "parallel", "arbitrary")))(a, b)


def _ffn_up_kernel(h_ref, wg_ref, wu_ref, o_ref, *, tn, n_valid):
    h = h_ref[...]
    g = jnp.dot(h, wg_ref[...], preferred_element_type=F32)
    u = jnp.dot(h, wu_ref[...], preferred_element_type=F32)
    col = pl.program_id(1) * tn + lax.broadcasted_iota(jnp.int32, g.shape, 1)
    o_ref[...] = jnp.where(col < n_valid, g * _sigmoid(g) * u, 0.0).astype(o_ref.dtype)


def _ffn_up(h, wg, wu, tm, tn, n_out):
    m, k = h.shape
    n = wg.shape[1]
    wspec = pl.BlockSpec((k, tn), lambda i, j: (0, j))
    return pl.pallas_call(
        functools.partial(_ffn_up_kernel, tn=tn, n_valid=n), grid=(m // tm, n_out // tn),
        in_specs=[pl.BlockSpec((tm, k), lambda i, j: (i, 0)), wspec, wspec],
        out_specs=pl.BlockSpec((tm, tn), lambda i, j: (i, j)),
        out_shape=jax.ShapeDtypeStruct((m, n_out), BF16),
        compiler_params=_cparams(("parallel", "parallel")))(h, wg, wu)


def _merge_kernel(oa_ref, ob_ref, oc_ref, w_ref, ga_ref, gb_ref, gc_ref, o_ref):
    acc = _sigmoid(ga_ref[...]) * jnp.dot(oa_ref[...], w_ref[0], preferred_element_type=F32)
    acc = acc + _sigmoid(gb_ref[...]) * jnp.dot(ob_ref[...], w_ref[1], preferred_element_type=F32)
    acc = acc + _sigmoid(gc_ref[...]) * jnp.dot(oc_ref[...], w_ref[2], preferred_element_type=F32)
    o_ref[...] = acc.astype(o_ref.dtype)


def _merge(o_a, o_b, o_c, w_br, proj, tm, tn):
    m = o_a.shape[0]
    gate_blk0 = COL_GATE // tn
    per_branch = D_MODEL // tn
    o_spec = pl.BlockSpec((tm, BRANCH_W), lambda i, j: (i, 0))
    gate = lambda b: pl.BlockSpec((tm, tn), lambda i, j: (i, gate_blk0 + b * per_branch + j))
    return pl.pallas_call(
        _merge_kernel, grid=(m // tm, D_MODEL // tn),
        in_specs=[o_spec, o_spec, o_spec,
                  pl.BlockSpec((3, BRANCH_W, tn), lambda i, j: (0, 0, j)),
                  gate(0), gate(1), gate(2)],
        out_specs=pl.BlockSpec((tm, tn), lambda i, j: (i, j)),
        out_shape=jax.ShapeDtypeStruct((m, D_MODEL), BF16),
        compiler_params=_cparams(("parallel", "parallel")))(o_a, o_b, o_c, w_br, proj, proj, proj)


def _rms(x, g):
    return x * lax.rsqrt(jnp.mean(x * x, axis=-1, keepdims=True) + EPS) * g


def _two_source_specs(tr, nb_first):
    first = pl.BlockSpec((tr, D_MODEL), lambda i: (jnp.minimum(i, nb_first - 1), 0))
    second = pl.BlockSpec((tr, D_MODEL), lambda i: (jnp.maximum(i - nb_first, 0), 0))
    return first, second


def _prenorm_kernel(xa_ref, xb_ref, g_ref, h_ref, *, nb_first):
    i = pl.program_id(0)

    @pl.when(i < nb_first)
    def _():
        h_ref[...] = _rms(xa_ref[...], g_ref[...]).astype(h_ref.dtype)

    @pl.when(i >= nb_first)
    def _():
        h_ref[...] = _rms(xb_ref[...], g_ref[...]).astype(h_ref.dtype)


def _prenorm(xa, xb, g, tr):
    ma, mb = xa.shape[0], xb.shape[0]
    sa, sb = _two_source_specs(tr, ma // tr)
    return pl.pallas_call(
        functools.partial(_prenorm_kernel, nb_first=ma // tr), grid=((ma + mb) // tr,),
        in_specs=[sa, sb, pl.BlockSpec((1, D_MODEL), lambda i: (0, 0))],
        out_specs=pl.BlockSpec((tr, D_MODEL), lambda i: (i, 0)),
        out_shape=jax.ShapeDtypeStruct((ma + mb, D_MODEL), BF16),
        compiler_params=_cparams(("arbitrary",)))(xa, xb, g.reshape(1, D_MODEL))


def _post_kernel(*refs, nb_first, n_src):
    x_refs = refs[:n_src]
    y_ref, gp_ref, gn_ref, xo_ref, h_ref = refs[n_src:]
    r = _rms(y_ref[...].astype(F32), gp_ref[...])

    def finish(x_ref):
        xn = x_ref[...] + r
        xo_ref[...] = xn
        h_ref[...] = _rms(xn, gn_ref[...]).astype(BF16)

    if n_src == 1:
        finish(x_refs[0])
    else:
        i = pl.program_id(0)
        pl.when(i < nb_first)(lambda: finish(x_refs[0]))
        pl.when(i >= nb_first)(lambda: finish(x_refs[1]))


def _post(xs, y, g_post, g_next, tr):
    m = y.shape[0]
    row = pl.BlockSpec((tr, D_MODEL), lambda i: (i, 0))
    vec = pl.BlockSpec((1, D_MODEL), lambda i: (0, 0))
    nb_first = xs[0].shape[0] // tr
    x_specs = list(_two_source_specs(tr, nb_first)) if len(xs) == 2 else [row]
    return pl.pallas_call(
        functools.partial(_post_kernel, nb_first=nb_first, n_src=len(xs)), grid=(m // tr,),
        in_specs=x_specs + [row, vec, vec], out_specs=[row, row],
        out_shape=[jax.ShapeDtypeStruct((m, D_MODEL), F32), jax.ShapeDtypeStruct((m, D_MODEL), BF16)],
        compiler_params=_cparams(("arbitrary",)))(
            *xs, y, g_post.reshape(1, D_MODEL), g_next.reshape(1, D_MODEL))


def _post_split_kernel(x_ref, y_ref, gp_ref, oa_ref, ob_ref, *, nb_first):
    i = pl.program_id(0)
    xn = x_ref[...] + _rms(y_ref[...].astype(F32), gp_ref[...])

    @pl.when(i < nb_first)
    def _():
        oa_ref[...] = xn

    @pl.when(i >= nb_first)
    def _():
        ob_ref[...] = xn


def _post_split(x, y, g_post, rows_first, tr):
    m = x.shape[0]
    row = pl.BlockSpec((tr, D_MODEL), lambda i: (i, 0))
    vec = pl.BlockSpec((1, D_MODEL), lambda i: (0, 0))
    oa, ob = _two_source_specs(tr, rows_first // tr)
    return pl.pallas_call(
        functools.partial(_post_split_kernel, nb_first=rows_first // tr), grid=(m // tr,),
        in_specs=[row, row, vec], out_specs=[oa, ob],
        out_shape=[jax.ShapeDtypeStruct((rows_first, D_MODEL), F32),
                   jax.ShapeDtypeStruct((m - rows_first, D_MODEL), F32)],
        compiler_params=_cparams(("arbitrary",)))(x, y, g_post.reshape(1, D_MODEL))


def _state_view(s_all, fresh_layer):
    return s_all if fresh_layer is None else s_all.at[fresh_layer]


def _zero_other_slots(s_all, fresh_layer):
    if fresh_layer is not None:
        for d in range(s_all.shape[0]):
            if d != fresh_layer:
                s_all[d] = jnp.zeros(s_all.shape[1:], s_all.dtype)


def _gdn_prep_kernel(a_ref, b_ref, alog_ref, dt_ref, gc_ref, beta_ref, *, prompt_blocks):
    i = pl.program_id(0)
    x = a_ref[...] + dt_ref[...]
    softplus = jnp.maximum(x, 0.0) + jnp.log1p(jnp.exp(-jnp.abs(x)))
    g = -jnp.exp(alog_ref[...]) * softplus
    chunk = jnp.where(i < prompt_blocks, MIX_CHUNK, 8)
    rmod = lax.broadcasted_iota(jnp.int32, g.shape, 0) & (chunk - 1)
    s = 1
    while s < MIX_CHUNK:
        g = g + jnp.where(rmod >= s, pltpu.roll(g, s, axis=0), 0.0)
        s *= 2
    gc_ref[...] = g
    beta_ref[...] = jax.nn.sigmoid(b_ref[...])


def _gdn_prep(ab, a_log, dt_bias, tr, prompt_rows):
    m = ab.shape[0]
    pad = lambda v: jnp.pad(v.astype(F32), (0, LANES - H_A)).reshape(1, LANES)
    blk = lambda c: pl.BlockSpec((tr, LANES), lambda i: (i, c))
    vec = pl.BlockSpec((1, LANES), lambda i: (0, 0))
    return pl.pallas_call(
        functools.partial(_gdn_prep_kernel, prompt_blocks=prompt_rows // tr), grid=(m // tr,),
        in_specs=[blk(0), blk(1), vec, vec], out_specs=[blk(0), blk(0)],
        out_shape=[jax.ShapeDtypeStruct((m, LANES), F32)] * 2,
        compiler_params=_cparams(("parallel",)))(ab, ab, pad(a_log), pad(dt_bias))


def _gdn_kernel(x_ref, z_ref, gcol_ref, bcol_ref, grow_ref, s0_ref, buf_ref, cw_ref, ng_ref, *rest, c, group, nseq,
                fresh_layer, n_alias, side, steps_inner):
    n_side = len(side)
    o_ref, s_all, tail_ref = rest[n_alias + n_side:n_alias + n_side + 3]
    xc_ref = rest[-1]
    s_ref = _state_view(s_all, fresh_layer)

    @pl.when(pl.program_id(1) == 0)
    def _():
        _zero_other_slots(s_all, fresh_layer)
        s_ref[...] = s0_ref[...]
        xc_ref[:, :8, :] = buf_ref[...]

    for q in range(nseq):
        xc_ref[q, 8:, :] = x_ref[q * c:(q + 1) * c, :]
        tail_ref[q] = x_ref[(q + 1) * c - 8:(q + 1) * c, :]

    def conv(q, lo):
        xcol = xc_ref[q, :, lo:lo + 128]
        acc = xcol[8:] * cw_ref[CONV_W - 1:CONV_W, lo:lo + 128]
        for s in range(1, CONV_W):
            acc = acc + pltpu.roll(xcol, s, axis=0)[8:] * cw_ref[CONV_W - 1 - s:CONV_W - s, lo:lo + 128]
        return acc * _sigmoid(acc)

    def l2n(t, scale=1.0):
        return t * (lax.rsqrt(jnp.sum(t * t, axis=-1, keepdims=True) + EPS) * scale)

    ii = lax.broadcasted_iota(jnp.int32, (c, c), 0)
    jj = lax.broadcasted_iota(jnp.int32, (c, c), 1)
    tri = ii >= jj
    strict = ii > jj
    ng = ng_ref[...]
    hk = H_A * DK_A
    for h0 in range(0, H_A, group):
        st = []
        for q in range(nseq):
            rows = slice(q * c, (q + 1) * c)
            for h in range(h0, h0 + group):
                lo = h * DK_A
                qh = l2n(conv(q, lo), DK_A ** -0.5)
                k = l2n(conv(q, hk + lo))
                v = conv(q, 2 * hk + lo)
                gc = gcol_ref[rows, h:h + 1]
                beta = bcol_ref[rows, h:h + 1]
                gr = grow_ref[q, 0, h:h + 1, :]
                glast = gr[:, c - 1:c]
                decay = jnp.exp(jnp.where(tri, gc - gr, -jnp.inf))
                egc = jnp.exp(gc)
                kb = k * beta
                m = lax.dot_general(jnp.concatenate([kb, qh], axis=0).astype(BF16), k.astype(BF16),
                                    (((1,), (1,)), ((), ())), preferred_element_type=F32)
                s = s_ref[q, h]
                sq = _bdot(jnp.concatenate([kb * egc, qh * egc], axis=0), s)
                st.append(dict(q=q, h=h, rows=rows, m=m, sq=sq, s=s, decay=decay, vb=v * beta,
                               kd=(k * jnp.exp(glast - gc)).astype(BF16), cd=jnp.exp(glast)))
        for d in st:
            d["p"] = (-jnp.where(strict, d["m"][:c] * d["decay"], 0.0)).astype(BF16)
            d["attn"] = (d["m"][c:] * d["decay"]).astype(BF16)
            d["x"] = d["vb"] - d["sq"][:c]
        n_sq = 1
        while n_sq < c:
            last = 2 * n_sq >= c
            for d in st:
                d["x"] = d["x"] + jnp.dot(d["p"], d["x"].astype(BF16), preferred_element_type=F32)
            if not last:
                for d in st:
                    d["p"] = jnp.dot(d["p"], d["p"], preferred_element_type=F32).astype(BF16)
            n_sq *= 2
        outs = {}
        for d in st:
            xb = d["x"].astype(BF16)
            o = d["sq"][c:] + jnp.dot(d["attn"], xb, preferred_element_type=F32)
            s_ref[d["q"], d["h"]] = d["s"] * d["cd"] + lax.dot_general(
                d["kd"], xb, (((0,), (0,)), ((), ())), preferred_element_type=F32)
            lo = d["h"] * DV_A
            o = o * lax.rsqrt(jnp.mean(o * o, axis=-1, keepdims=True) + EPS) * ng
            z = z_ref[d["rows"], lo:lo + DV_A]
            outs[d["q"], d["h"]] = o * (z * _sigmoid(z))
        for h in range(h0, h0 + group):
            o = outs[0, h] if nseq == 1 else jnp.concatenate([outs[q, h] for q in range(nseq)], axis=0)
            o_ref[:, h * DV_A:(h + 1) * DV_A] = o.astype(o_ref.dtype)
    xc_ref[:, :8, :] = xc_ref[:, c:c + 8, :]
    _run_side_casts(side, steps_inner, rest[n_alias:n_alias + n_side], rest[n_alias + n_side + 3:-1])


def _state_out(stack, depth, layer_out, n, nseq, inner):
    if stack is None:
        return (pl.BlockSpec((depth, nseq) + inner, lambda i, j: (0, i, 0, 0, 0)),
                jax.ShapeDtypeStruct((depth, n) + inner, F32))
    return (pl.BlockSpec((None, nseq) + inner, lambda i, j: (layer_out, i, 0, 0, 0)),
            jax.ShapeDtypeStruct(stack.shape, F32))


def _gdn(proj, gc, beta, grow, s0, layer, stack, depth, layer_out, o_buf, buf8, conv_w, norm_g, row_off, n, l, c,
         group, nseq, side=()):
    nc = l // c
    assert nseq == 1 or nc == 1
    blk = nseq * c
    base = row_off // blk
    rows = lambda w, col: pl.BlockSpec((blk, w), lambda i, j: (base + i * nc + j, col))
    state_spec, state_shape = _state_out(stack, depth, layer_out, n, nseq, (H_A, DK_A, DV_A))
    aliased = ([stack] if stack is not None else []) + [o_buf]
    n_in = 9
    assert all(job.n_blocks <= (n // nseq) * nc for job in side)
    side_in, side_out, side_shape = _side_specs(side, nc)
    return pl.pallas_call(
        functools.partial(_gdn_kernel, c=c, group=group, nseq=nseq,
                          fresh_layer=layer_out if stack is None else None,
                          n_alias=len(aliased), side=tuple(side), steps_inner=nc),
        grid=(n // nseq, nc),
        in_specs=[rows(CONV_DIM, 0), rows(BRANCH_W, COL_Z // BRANCH_W), rows(LANES, 0), rows(LANES, 0),
                  pl.BlockSpec((nseq, 1, H_A, c), lambda i, j: (i, j, 0, 0)),
                  pl.BlockSpec((None, nseq, H_A, DK_A, DV_A), lambda i, j: (layer, i, 0, 0, 0)),
                  pl.BlockSpec((nseq, 8, CONV_DIM), lambda i, j: (i, 0, 0)),
                  pl.BlockSpec((CONV_W, CONV_DIM), lambda i, j: (0, 0)),
                  pl.BlockSpec((1, DV_A), lambda i, j: (0, 0))]
        + [pl.BlockSpec(memory_space=pl.ANY)] * len(aliased) + side_in,
        out_specs=[rows(BRANCH_W, 0), state_spec, pl.BlockSpec((nseq, 8, CONV_DIM), lambda i, j: (i, 0, 0))]
        + side_out,
        out_shape=[jax.ShapeDtypeStruct(o_buf.shape, o_buf.dtype), state_shape,
                   jax.ShapeDtypeStruct((n, 8, CONV_DIM), F32)] + side_shape,
        scratch_shapes=[pltpu.VMEM((nseq, c + 8, CONV_DIM), F32)],
        input_output_aliases={n_in + len(aliased) - 1: 0, **({n_in: 1} if stack is not None else {})},
        compiler_params=_cparams(("arbitrary", "arbitrary")))(
            proj, proj, gc, beta, grow, s0, buf8, conv_w, norm_g.reshape(1, DV_A), *aliased,
            *[job.w for job in side])


def _ret_kernel(qk_ref, v_ref, gate_ref, cos_ref, sin_ref, r0_ref, *rest, c, nseq, fresh_layer):
    o_ref = rest[-2]
    r_ref = _state_view(rest[-1], fresh_layer)

    @pl.when(pl.program_id(1) == 0)
    def _():
        _zero_other_slots(rest[-1], fresh_layer)
        r_ref[...] = r0_ref[...]

    ii = lax.broadcasted_iota(jnp.int32, (c, c), 0)
    jj = lax.broadcasted_iota(jnp.int32, (c, c), 1)
    rel = (ii - jj).astype(F32)
    idx = lax.broadcasted_iota(jnp.int32, (c, 1), 0).astype(F32)
    even = (lax.broadcasted_iota(jnp.int32, (c, DK_B), 1) & 1) == 0
    cos = cos_ref[...]
    sin = sin_ref[...]

    def rot(x):
        swapped = jnp.where(even, pltpu.roll(x, DK_B - 1, axis=1), pltpu.roll(x, 1, axis=1))
        return x * cos + swapped * sin

    units = [(q, h) for q in range(nseq) for h in range(H_B)]
    lg = {h: math.log1p(-2.0 ** (-5.0 - h)) for h in range(H_B)}
    rows = {q: slice(q * c, (q + 1) * c) for q in range(nseq)}
    qs = {u: rot(qk_ref[rows[u[0]], u[1] * DK_B:(u[1] + 1) * DK_B]) for u in units}
    ks = {u: rot(qk_ref[rows[u[0]], (H_B + u[1]) * DK_B:(H_B + u[1] + 1) * DK_B]) * (DK_B ** -0.5)
          for u in units}
    vs = {u: v_ref[rows[u[0]], u[1] * DV_B:(u[1] + 1) * DV_B].astype(BF16) for u in units}
    rs = {u: r_ref[u] for u in units}
    scores = {u: _bdot_nt(qs[u], ks[u]) for u in units}
    cross = {u: _bdot(qs[u] * jnp.exp(lg[u[1]] * (idx + 1.0)), rs[u]) for u in units}
    for u in units:
        r_ref[u] = rs[u] * math.exp(lg[u[1]] * c) + _bdot_tn(ks[u] * jnp.exp(lg[u[1]] * (c - 1.0 - idx)), vs[u])
    inners = {u: _bdot(scores[u] * jnp.where(rel >= 0, jnp.exp(lg[u[1]] * jnp.maximum(rel, 0.0)), 0.0), vs[u])
              for u in units}
    outs = {}
    for u in units:
        o = inners[u] + cross[u]
        mu = jnp.mean(o, axis=-1, keepdims=True)
        d = o - mu
        var = jnp.mean(d * d, axis=-1, keepdims=True)
        o = d * lax.rsqrt(var + EPS)
        gate = gate_ref[rows[u[0]], u[1] * DV_B:(u[1] + 1) * DV_B]
        outs[u] = gate * _sigmoid(gate) * o
    for h in range(H_B):
        o = outs[0, h] if nseq == 1 else jnp.concatenate([outs[q, h] for q in range(nseq)], axis=0)
        o_ref[:, h * DV_B:(h + 1) * DV_B] = o.astype(o_ref.dtype)


def _rope_tables(pos):
    half = DK_B // 2
    inv_freq = 1.0 / (ROPE_BASE ** jnp.linspace(0.0, 1.0, half, dtype=F32))
    ang = pos.astype(F32)[:, None] * inv_freq[None, :]
    sin = jnp.sin(ang)
    cos = jnp.cos(ang)
    cos2 = jnp.stack([cos, cos], axis=-1).reshape(-1, DK_B)
    sin2 = jnp.stack([-sin, sin], axis=-1).reshape(-1, DK_B)
    return cos2, sin2


def _retention(proj, cos2, sin2, r0, layer, stack, depth, layer_out, o_buf, row_off, n, l, c, nseq):
    nc = l // c
    assert nseq == 1 or nc == 1
    blk = nseq * c
    base = row_off // blk
    col = lambda off: pl.BlockSpec((blk, BRANCH_W), lambda i, j: (base + i * nc + j, off // BRANCH_W))
    tab = pl.BlockSpec((c, DK_B), lambda i, j: (j, 0))
    state_spec, state_shape = _state_out(stack, depth, layer_out, n, nseq, (H_B, DK_B, DV_B))
    aliased = ([stack] if stack is not None else []) + [o_buf]
    n_in = 6
    return pl.pallas_call(
        functools.partial(_ret_kernel, c=c, nseq=nseq, fresh_layer=layer_out if stack is None else None),
        grid=(n // nseq, nc),
        in_specs=[col(COL_QKB), col(COL_VB), col(COL_GB), tab, tab,
                  pl.BlockSpec((None, nseq, H_B, DK_B, DV_B), lambda i, j: (layer, i, 0, 0, 0))]
        + [pl.BlockSpec(memory_space=pl.ANY)] * len(aliased),
        out_specs=[col(0), state_spec],
        out_shape=[jax.ShapeDtypeStruct(o_buf.shape, o_buf.dtype), state_shape],
        input_output_aliases={n_in + len(aliased) - 1: 0, **({n_in: 1} if stack is not None else {})},
        compiler_params=_cparams(("parallel", "arbitrary")))(proj, proj, proj, cos2, sin2, r0, *aliased)


def _gelu(x):
    return 0.5 * x * (1.0 + lax.erf(x * (2.0 ** -0.5)))


def _gmlp_kernel(u_ref, v_ref, w_ref, bcol_ref, lng_ref, lnb_ref, obuf_ref, o_ref, *vrows_ref, seq):
    u32 = _gelu(u_ref[...])
    v32 = _gelu(v_ref[...])
    mu = jnp.mean(v32, axis=-1, keepdims=True)
    d = v32 - mu
    var = jnp.mean(d * d, axis=-1, keepdims=True)
    vn = d * lax.rsqrt(var + EPS) * lng_ref[...] + lnb_ref[...]
    if vrows_ref:
        vrows_ref[0][...] = vn
    ii = lax.broadcasted_iota(jnp.int32, (CHUNK_C, CHUNK_C), 0)
    jj = lax.broadcasted_iota(jnp.int32, (CHUNK_C, CHUNK_C), 1)
    mask = (ii >= jj) & ((ii & -seq) == (jj & -seq))
    for g in range(G_C):
        w = jnp.where(mask, w_ref[g], 0.0)
        mixed = _bdot(w, vn[:, g * DG_C:(g + 1) * DG_C]) + bcol_ref[:, g:g + 1]
        o_ref[:, g * DG_C:(g + 1) * DG_C] = (u32[:, g * DG_C:(g + 1) * DG_C] * mixed).astype(o_ref.dtype)


def _gmlp(proj, w_tile, b_col, ln_g, ln_b, o_buf, row_off, rows, seq, want_vrows):
    base = row_off // CHUNK_C
    col = lambda off: pl.BlockSpec((CHUNK_C, BRANCH_W), lambda i: (base + i, off // BRANCH_W))
    vec = pl.BlockSpec((1, BRANCH_W), lambda i: (0, 0))
    out_shape = [jax.ShapeDtypeStruct(o_buf.shape, o_buf.dtype)]
    out_specs = [col(0)]
    if want_vrows:
        out_shape.append(jax.ShapeDtypeStruct((rows, BRANCH_W), F32))
        out_specs.append(pl.BlockSpec((CHUNK_C, BRANCH_W), lambda i: (i, 0)))
    res = pl.pallas_call(
        functools.partial(_gmlp_kernel, seq=seq), grid=(rows // CHUNK_C,),
        in_specs=[col(COL_U), col(COL_V),
                  pl.BlockSpec((G_C, CHUNK_C, CHUNK_C), lambda i: (0, 0, 0)),
                  pl.BlockSpec((CHUNK_C, G_C), lambda i: (0, 0)), vec, vec,
                  pl.BlockSpec(memory_space=pl.ANY)],
        out_specs=out_specs, out_shape=out_shape, input_output_aliases={6: 0},
        compiler_params=_cparams(("parallel",)))(
            proj, proj, w_tile, b_col, ln_g.reshape(1, BRANCH_W), ln_b.reshape(1, BRANCH_W), o_buf)
    return res if want_vrows else (res[0], None)


AB_COL0 = 4 * BRANCH_W
AB_COLS = 2 * H_A
WD_BLK, WBR_BLK, PACK_BLK = 64, 32, 128
FFN_BLK = 64


def _pack_kernel(w_ref, o_ref):
    o_ref[...] = w_ref[0].astype(BF16)


def _pack_src_row(j, tn):
    r = j * tn
    return r + jnp.where(r >= AB_COL0, AB_COLS, 0)


def _pack_w_in_t(w_in_t, layer, tn):
    k = w_in_t.shape[2]
    return pl.pallas_call(
        _pack_kernel, grid=(N_PROJ // tn,),
        in_specs=[pl.BlockSpec((pl.Element(1), pl.Element(tn), pl.Element(k)),
                               lambda j: (layer, pl.multiple_of(_pack_src_row(j, tn), 8), 0))],
        out_specs=pl.BlockSpec((tn, k), lambda j: (j, 0)),
        out_shape=jax.ShapeDtypeStruct((N_PROJ, k), BF16),
        compiler_params=_cparams(("parallel",)))(w_in_t)


def _ab_weights_t(w_in_t, layer):
    ab = w_in_t[layer, AB_COL0:AB_COL0 + AB_COLS, :]
    zeros = jnp.zeros((LANES - H_A, ab.shape[1]), F32)
    return jnp.concatenate([ab[:H_A], zeros, ab[H_A:], zeros], axis=0)


def kernel(x_prompt, x_sample, state_gdn, state_conv, state_ret, w_in, conv_w, a_log, dt_bias, gdn_norm, w_s, b_s, ln_c_g, ln_c_b, w_br, w_o, g_pre_mix, g_post_mix, g_pre_ffn, g_post_ffn, w_ffn_gate, w_ffn_up, w_ffn_down):
    n_p, l_p, _ = x_prompt.shape
    n_s, l_s, _ = x_sample.shape
    depth = w_in.shape[0]
    rows_p, rows_s = n_p * l_p, n_s * l_s
    m = rows_p + rows_s
    xs = (x_prompt.reshape(rows_p, D_MODEL), x_sample.reshape(rows_s, D_MODEL))
    tr = ROW_TILE

    cos_p, sin_p = _rope_tables(jnp.arange(l_p))
    cos_s, sin_s = _rope_tables(PAST_LEN + jnp.arange(l_s))
    zero_gdn = jnp.zeros((1, n_p, H_A, DK_A, DV_A), F32)
    zero_ret = jnp.zeros((1, n_p, H_B, DK_B, DV_B), F32)
    zero_buf8 = jnp.zeros((n_p, 8, CONV_DIM), F32)
    reps = CHUNK_C // l_s

    w_in_t = jnp.swapaxes(w_in, 1, 2)

    gdn_p = gdn_s = ret_p = ret_s = None
    outs = {k: [] for k in ("conv_p", "conv_s", "vrows")}
    h = _prenorm(xs[0], xs[1], g_pre_mix[0], tr)
    w_br_rows = w_br.reshape(depth, 3 * BRANCH_W, D_MODEL)
    w_main = _pack_w_in_t(w_in_t, 0, TN_HALF)
    for l in range(depth):
        side = [_SideCast(w_ffn_down, l, WD_BLK, D_FF_PAD // WD_BLK,
                          lambda b: jnp.minimum(b, D_FF // WD_BLK - 1) * WD_BLK, D_FF // WD_BLK),
                _SideCast(w_br_rows, l, WBR_BLK, 3 * BRANCH_W // WBR_BLK, lambda b: b * WBR_BLK,
                          3 * BRANCH_W // WBR_BLK)]
        if l + 1 < depth:
            side.append(_SideCast(w_in_t, l + 1, PACK_BLK, N_PROJ // PACK_BLK,
                                  lambda b: _pack_src_row(b, PACK_BLK), N_PROJ // PACK_BLK))
        proj, wd, w_br_l, *w_main = _matmul_nt(h, w_main, F32, TM, TN, side)
        w_main = w_main[0] if w_main else None
        ab = _matmul_nt(h, _ab_weights_t(w_in_t, l), F32, TM, 2 * LANES)[0]

        gc, beta = _gdn_prep(ab, a_log[l], dt_bias[l], TM, rows_p)
        grow_p = gc[:rows_p, :H_A].reshape(n_p, l_p // MIX_CHUNK, MIX_CHUNK, H_A).transpose(0, 1, 3, 2)
        grow_s = gc[rows_p:, :H_A].reshape(n_s, 1, l_s, H_A).transpose(0, 1, 3, 2)
        buf8_s = jnp.pad(state_conv[l], ((0, 0), (8 - (CONV_W - 1), 0), (0, 0)))
        o_a = jnp.zeros((m, BRANCH_W), BF16)
        o_b = jnp.zeros((m, BRANCH_W), BF16)
        o_c = jnp.zeros((m, BRANCH_W), BF16)
        ffn_side = [_SideCast(w, l, FFN_BLK, D_MODEL // FFN_BLK, lambda b: b * FFN_BLK, D_MODEL // FFN_BLK)
                    for w in (w_ffn_gate, w_ffn_up)]
        o_a, gdn_p, tail_p, wg, wu = _gdn(proj, gc, beta, grow_p, zero_gdn, 0, gdn_p, depth, l, o_a, zero_buf8,
                                          conv_w[l], gdn_norm[l], 0, n_p, l_p, MIX_CHUNK, H_A, 1, ffn_side)
        o_a, gdn_s, tail_s = _gdn(proj, gc, beta, grow_s, state_gdn, l, gdn_s, depth, l, o_a, buf8_s, conv_w[l],
                                  gdn_norm[l], rows_p, n_s, l_s, l_s, GDN_GROUP, GDN_SAMPLE_SEQS)
        outs["conv_p"].append(tail_p[:, 8 - (CONV_W - 1):])
        outs["conv_s"].append(tail_s[:, 8 - (CONV_W - 1):])

        o_b, ret_p = _retention(proj, cos_p, sin_p, zero_ret, 0, ret_p, depth, l, o_b, 0, n_p, l_p, RET_CHUNK, 1)
        o_b, ret_s = _retention(proj, cos_s, sin_s, state_ret, l, ret_s, depth, l, o_b, rows_p, n_s, l_s, l_s,
                                RET_SAMPLE_SEQS)

        o_c, _ = _gmlp(proj, w_s[l], b_s[l].T, ln_c_g[l], ln_c_b[l], o_c, 0, rows_p, CHUNK_C, False)
        w_tile_s = jnp.tile(w_s[l][:, :l_s, :l_s], (1, reps, reps))
        b_col_s = jnp.tile(b_s[l][:, :l_s].T, (reps, 1))
        o_c, vrows = _gmlp(proj, w_tile_s, b_col_s, ln_c_g[l], ln_c_b[l], o_c, rows_p, rows_s, l_s, True)
        outs["vrows"].append(vrows.reshape(n_s, l_s, BRANCH_W))

        merged = _merge(o_a, o_b, o_c, w_br_l.reshape(3, BRANCH_W, D_MODEL), proj, TM, TN_HALF)
        mix = _matmul_ws(merged, w_o, l, BF16, TM, TN_HALF)
        x, h = _post(xs, mix, g_post_mix[l], g_pre_ffn[l], tr)

        f1 = _ffn_up(h, wg, wu, TM, TN_HALF, D_FF_PAD)
        f = _matmul_k2(f1, wd, BF16, TM, TN)
        if l + 1 < depth:
            x, h = _post((x,), f, g_post_ffn[l], g_pre_mix[l + 1], tr)
            xs = (x,)
        else:
            y_p, y_s = _post_split(x, f, g_post_ffn[l], rows_p, tr)

    y_p = y_p.reshape(n_p, l_p, D_MODEL)
    y_s = y_s.reshape(n_s, l_s, D_MODEL)
    st = lambda k: jnp.stack(outs[k])
    return (y_p, y_s, gdn_p, gdn_s, st("conv_p"), st("conv_s"), ret_p, ret_s, st("vrows"))
```

```python
import functools
import math
from typing import Callable, NamedTuple

import jax
import jax.numpy as jnp
from jax import lax
from jax.experimental import pallas as pl
from jax.experimental.pallas import tpu as pltpu

F32 = jnp.float32
BF16 = jnp.bfloat16

D_MODEL = 4096
BRANCH_W = 2048
H_A, DK_A, DV_A = 16, 128, 128
CONV_W = 4
CONV_DIM = 6144
H_B, DK_B, DV_B = 8, 128, 256
G_C, DG_C, CHUNK_C = 8, 256, 128
D_FF = 11008
D_FF_PAD = 11264
EPS = 1e-6
ROPE_BASE = 10000.0
PAST_LEN = 16384
MIX_CHUNK = 64
RET_CHUNK = 256
SOLVE_BLOCK = 8
GDN_GROUP = 8
GDN_SAMPLE_SEQS = 2
RET_SAMPLE_SEQS = 4

COL_QKV, COL_Z, COL_QKB, COL_VB, COL_GB, COL_U, COL_V, COL_GATE = (
    0, 6144, 8192, 10240, 12288, 14336, 16384, 18432)
N_PROJ = 30720
TM, TN, TN_HALF, ROW_TILE = 1024, 1024, 512, 256
LANES = 128
VMEM_LIMIT = 56 * 1024 * 1024
VMEM_LIMIT_HOST = 60 * 1024 * 1024


def _cparams(sem, vmem=VMEM_LIMIT):
    return pltpu.CompilerParams(dimension_semantics=sem, vmem_limit_bytes=vmem)


def _sigmoid(x):
    return 0.5 * jnp.tanh(0.5 * x) + 0.5


def _bdot(a, b):
    return jnp.dot(a.astype(BF16), b.astype(BF16), preferred_element_type=F32)


def _bdot_nt(a, b):
    return lax.dot_general(a.astype(BF16), b.astype(BF16), (((1,), (1,)), ((), ())),
                           preferred_element_type=F32)


def _bdot_tn(a, b):
    return lax.dot_general(a.astype(BF16), b.astype(BF16), (((0,), (0,)), ((), ())),
                           preferred_element_type=F32)


class _SideCast(NamedTuple):
    w: jax.Array
    layer: int
    blk: int
    n_blocks: int
    src_row: Callable
    n_copy: int


def _side_specs(side, steps_inner):
    def blk_idx(job, g0, g1):
        return jnp.minimum(g0 * steps_inner + g1, job.n_blocks - 1)

    in_specs = [pl.BlockSpec((pl.Element(1), pl.Element(job.blk), pl.Element(job.w.shape[2])),
                             lambda g0, g1, job=job: (job.layer,
                                                      pl.multiple_of(job.src_row(blk_idx(job, g0, g1)), 8), 0))
                for job in side]
    out_specs = [pl.BlockSpec((job.blk, job.w.shape[2]), lambda g0, g1, job=job: (blk_idx(job, g0, g1), 0))
                 for job in side]
    out_shape = [jax.ShapeDtypeStruct((job.n_blocks * job.blk, job.w.shape[2]), BF16) for job in side]
    return in_specs, out_specs, out_shape


def _run_side_casts(side, steps_inner, w_refs, o_refs):
    t = pl.program_id(0) * steps_inner + pl.program_id(1)
    for job, w_ref, so_ref in zip(side, w_refs, o_refs):
        v = w_ref[0].astype(BF16)
        so_ref[...] = v if job.n_copy >= job.n_blocks else jnp.where(t < job.n_copy, v, jnp.zeros_like(v))


def _mm_nt_kernel(a_ref, bt_ref, *refs, side, steps_inner):
    n_side = len(side)
    o_ref = refs[n_side]
    _run_side_casts(side, steps_inner, refs[:n_side], refs[n_side + 1:])
    o_ref[...] = lax.dot_general(a_ref[...], bt_ref[...].astype(BF16), (((1,), (1,)), ((), ())),
                                 preferred_element_type=F32).astype(o_ref.dtype)


def _matmul_nt(a, bt, out_dtype, tm, tn, side=()):
    m, k = a.shape
    n = bt.shape[0]
    nj = n // tn
    assert all(job.n_blocks <= (m // tm) * nj for job in side)
    side_in, side_out, side_shape = _side_specs(side, nj)
    return pl.pallas_call(
        functools.partial(_mm_nt_kernel, side=tuple(side), steps_inner=nj), grid=(m // tm, nj),
        in_specs=[pl.BlockSpec((tm, k), lambda i, j: (i, 0)),
                  pl.BlockSpec((tn, k), lambda i, j: (j, 0))] + side_in,
        out_specs=[pl.BlockSpec((tm, tn), lambda i, j: (i, j))] + side_out,
        out_shape=[jax.ShapeDtypeStruct((m, n), out_dtype)] + side_shape,
        compiler_params=_cparams(("arbitrary", "arbitrary"), VMEM_LIMIT_HOST if side else VMEM_LIMIT))(
            a, bt, *[job.w for job in side])


def _mm_ws_kernel(a_ref, b_ref, o_ref, bw_ref):
    @pl.when(pl.program_id(1) == 0)
    def _():
        bw_ref[...] = b_ref[...].astype(BF16)

    o_ref[...] = jnp.dot(a_ref[...], bw_ref[...], preferred_element_type=F32).astype(o_ref.dtype)


def _matmul_ws(a, b, layer, out_dtype, tm, tn):
    m, k = a.shape
    n = b.shape[2]
    return pl.pallas_call(
        _mm_ws_kernel, grid=(n // tn, m // tm),
        in_specs=[pl.BlockSpec((tm, k), lambda j, i: (i, 0)),
                  pl.BlockSpec((None, k, tn), lambda j, i: (layer, 0, j))],
        out_specs=pl.BlockSpec((tm, tn), lambda j, i: (i, j)),
        out_shape=jax.ShapeDtypeStruct((m, n), out_dtype),
        scratch_shapes=[pltpu.VMEM((k, tn), BF16)],
        compiler_params=_cparams(("arbitrary", "arbitrary")))(a, b)


def _mm_k2_kernel(a_ref, b_ref, o_ref, acc_ref):
    @pl.when(pl.program_id(2) == 0)
    def _():
        acc_ref[...] = jnp.dot(a_ref[...], b_ref[...], preferred_element_type=F32)

    @pl.when(pl.program_id(2) == 1)
    def _():
        o_ref[...] = (acc_ref[...] + jnp.dot(a_ref[...], b_ref[...], preferred_element_type=F32)).astype(o_ref.dtype)


def _matmul_k2(a, b, out_dtype, tm, tn):
    m, k = a.shape
    n = b.shape[1]
    tk = k // 2
    return pl.pallas_call(
        _mm_k2_kernel, grid=(m // tm, n // tn, 2),
        in_specs=[pl.BlockSpec((tm, tk), lambda i, j, kk: (i, kk)),
                  pl.BlockSpec((tk, tn), lambda i, j, kk: (kk, j))],
        out_specs=pl.BlockSpec((tm, tn), lambda i, j, kk: (i, j)),
        out_shape=jax.ShapeDtypeStruct((m, n), out_dtype),
        scratch_shapes=[pltpu.VMEM((tm, tn), F32)],
        compiler_params=_cparams(("parallel", "parallel", "arbitrary")))(a, b)


def _ffn_up_kernel(h_ref, wg_ref, wu_ref, o_ref, *, tn, n_valid):
    h = h_ref[...]
    g = jnp.dot(h, wg_ref[...], preferred_element_type=F32)
    u = jnp.dot(h, wu_ref[...], preferred_element_type=F32)
    col = pl.program_id(1) * tn + lax.broadcasted_iota(jnp.int32, g.shape, 1)
    o_ref[...] = jnp.where(col < n_valid, g * _sigmoid(g) * u, 0.0).astype(o_ref.dtype)


def _ffn_up(h, wg, wu, tm, tn, n_out):
    m, k = h.shape
    n = wg.shape[1]
    wspec = pl.BlockSpec((k, tn), lambda i, j: (0, j))
    return pl.pallas_call(
        functools.partial(_ffn_up_kernel, tn=tn, n_valid=n), grid=(m // tm, n_out // tn),
        in_specs=[pl.BlockSpec((tm, k), lambda i, j: (i, 0)), wspec, wspec],
        out_specs=pl.BlockSpec((tm, tn), lambda i, j: (i, j)),
        out_shape=jax.ShapeDtypeStruct((m, n_out), BF16),
        compiler_params=_cparams(("parallel", "parallel")))(h, wg, wu)


def _merge_kernel(oa_ref, ob_ref, oc_ref, w_ref, ga_ref, gb_ref, gc_ref, o_ref):
    acc = _sigmoid(ga_ref[...]) * jnp.dot(oa_ref[...], w_ref[0], preferred_element_type=F32)
    acc = acc + _sigmoid(gb_ref[...]) * jnp.dot(ob_ref[...], w_ref[1], preferred_element_type=F32)
    acc = acc + _sigmoid(gc_ref[...]) * jnp.dot(oc_ref[...], w_ref[2], preferred_element_type=F32)
    o_ref[...] = acc.astype(o_ref.dtype)


def _merge(o_a, o_b, o_c, w_br, proj, tm, tn):
    m = o_a.shape[0]
    gate_blk0 = COL_GATE // tn
    per_branch = D_MODEL // tn
    o_spec = pl.BlockSpec((tm, BRANCH_W), lambda i, j: (i, 0))
    gate = lambda b: pl.BlockSpec((tm, tn), lambda i, j: (i, gate_blk0 + b * per_branch + j))
    return pl.pallas_call(
        _merge_kernel, grid=(m // tm, D_MODEL // tn),
        in_specs=[o_spec, o_spec, o_spec,
                  pl.BlockSpec((3, BRANCH_W, tn), lambda i, j: (0, 0, j)),
                  gate(0), gate(1), gate(2)],
        out_specs=pl.BlockSpec((tm, tn), lambda i, j: (i, j)),
        out_shape=jax.ShapeDtypeStruct((m, D_MODEL), BF16),
        compiler_params=_cparams(("parallel", "parallel")))(o_a, o_b, o_c, w_br, proj, proj, proj)


def _rms(x, g):
    return x * lax.rsqrt(jnp.mean(x * x, axis=-1, keepdims=True) + EPS) * g


def _two_source_specs(tr, nb_first):
    first = pl.BlockSpec((tr, D_MODEL), lambda i: (jnp.minimum(i, nb_first - 1), 0))
    second = pl.BlockSpec((tr, D_MODEL), lambda i: (jnp.maximum(i - nb_first, 0), 0))
    return first, second


def _prenorm_kernel(xa_ref, xb_ref, g_ref, h_ref, *, nb_first):
    i = pl.program_id(0)

    @pl.when(i < nb_first)
    def _():
        h_ref[...] = _rms(xa_ref[...], g_ref[...]).astype(h_ref.dtype)

    @pl.when(i >= nb_first)
    def _():
        h_ref[...] = _rms(xb_ref[...], g_ref[...]).astype(h_ref.dtype)


def _prenorm(xa, xb, g, tr):
    ma, mb = xa.shape[0], xb.shape[0]
    sa, sb = _two_source_specs(tr, ma // tr)
    return pl.pallas_call(
        functools.partial(_prenorm_kernel, nb_first=ma // tr), grid=((ma + mb) // tr,),
        in_specs=[sa, sb, pl.BlockSpec((1, D_MODEL), lambda i: (0, 0))],
        out_specs=pl.BlockSpec((tr, D_MODEL), lambda i: (i, 0)),
        out_shape=jax.ShapeDtypeStruct((ma + mb, D_MODEL), BF16),
        compiler_params=_cparams(("arbitrary",)))(xa, xb, g.reshape(1, D_MODEL))


def _post_kernel(*refs, nb_first, n_src):
    x_refs = refs[:n_src]
    y_ref, gp_ref, gn_ref, xo_ref, h_ref = refs[n_src:]
    r = _rms(y_ref[...].astype(F32), gp_ref[...])

    def finish(x_ref):
        xn = x_ref[...] + r
        xo_ref[...] = xn
        h_ref[...] = _rms(xn, gn_ref[...]).astype(BF16)

    if n_src == 1:
        finish(x_refs[0])
    else:
        i = pl.program_id(0)
        pl.when(i < nb_first)(lambda: finish(x_refs[0]))
        pl.when(i >= nb_first)(lambda: finish(x_refs[1]))


def _post(xs, y, g_post, g_next, tr):
    m = y.shape[0]
    row = pl.BlockSpec((tr, D_MODEL), lambda i: (i, 0))
    vec = pl.BlockSpec((1, D_MODEL), lambda i: (0, 0))
    nb_first = xs[0].shape[0] // tr
    x_specs = list(_two_source_specs(tr, nb_first)) if len(xs) == 2 else [row]
    return pl.pallas_call(
        functools.partial(_post_kernel, nb_first=nb_first, n_src=len(xs)), grid=(m // tr,),
        in_specs=x_specs + [row, vec, vec], out_specs=[row, row],
        out_shape=[jax.ShapeDtypeStruct((m, D_MODEL), F32), jax.ShapeDtypeStruct((m, D_MODEL), BF16)],
        compiler_params=_cparams(("arbitrary",)))(
            *xs, y, g_post.reshape(1, D_MODEL), g_next.reshape(1, D_MODEL))


def _post_split_kernel(x_ref, y_ref, gp_ref, oa_ref, ob_ref, *, nb_first):
    i = pl.program_id(0)
    xn = x_ref[...] + _rms(y_ref[...].astype(F32), gp_ref[...])

    @pl.when(i < nb_first)
    def _():
        oa_ref[...] = xn

    @pl.when(i >= nb_first)
    def _():
        ob_ref[...] = xn


def _post_split(x, y, g_post, rows_first, tr):
    m = x.shape[0]
    row = pl.BlockSpec((tr, D_MODEL), lambda i: (i, 0))
    vec = pl.BlockSpec((1, D_MODEL), lambda i: (0, 0))
    oa, ob = _two_source_specs(tr, rows_first // tr)
    return pl.pallas_call(
        functools.partial(_post_split_kernel, nb_first=rows_first // tr), grid=(m // tr,),
        in_specs=[row, row, vec], out_specs=[oa, ob],
        out_shape=[jax.ShapeDtypeStruct((rows_first, D_MODEL), F32),
                   jax.ShapeDtypeStruct((m - rows_first, D_MODEL), F32)],
        compiler_params=_cparams(("arbitrary",)))(x, y, g_post.reshape(1, D_MODEL))


def _state_view(s_all, fresh_layer):
    return s_all if fresh_layer is None else s_all.at[fresh_layer]


def _zero_other_slots(s_all, fresh_layer):
    if fresh_layer is not None:
        for d in range(s_all.shape[0]):
            if d != fresh_layer:
                s_all[d] = jnp.zeros(s_all.shape[1:], s_all.dtype)


def _gdn_prep_kernel(a_ref, b_ref, alog_ref, dt_ref, gc_ref, beta_ref, *, prompt_blocks):
    i = pl.program_id(0)
    x = a_ref[...] + dt_ref[...]
    softplus = jnp.maximum(x, 0.0) + jnp.log1p(jnp.exp(-jnp.abs(x)))
    g = -jnp.exp(alog_ref[...]) * softplus
    chunk = jnp.where(i < prompt_blocks, MIX_CHUNK, 8)
    rmod = lax.broadcasted_iota(jnp.int32, g.shape, 0) & (chunk - 1)
    s = 1
    while s < MIX_CHUNK:
        g = g + jnp.where(rmod >= s, pltpu.roll(g, s, axis=0), 0.0)
        s *= 2
    gc_ref[...] = g
    beta_ref[...] = jax.nn.sigmoid(b_ref[...])


def _gdn_prep(ab, a_log, dt_bias, tr, prompt_rows):
    m = ab.shape[0]
    pad = lambda v: jnp.pad(v.astype(F32), (0, LANES - H_A)).reshape(1, LANES)
    blk = lambda c: pl.BlockSpec((tr, LANES), lambda i: (i, c))
    vec = pl.BlockSpec((1, LANES), lambda i: (0, 0))
    return pl.pallas_call(
        functools.partial(_gdn_prep_kernel, prompt_blocks=prompt_rows // tr), grid=(m // tr,),
        in_specs=[blk(0), blk(1), vec, vec], out_specs=[blk(0), blk(0)],
        out_shape=[jax.ShapeDtypeStruct((m, LANES), F32)] * 2,
        compiler_params=_cparams(("parallel",)))(ab, ab, pad(a_log), pad(dt_bias))


def _gdn_kernel(x_ref, z_ref, gcol_ref, bcol_ref, grow_ref, s0_ref, buf_ref, cw_ref, ng_ref, *rest, c, group, nseq,
                fresh_layer, n_alias, side, steps_inner):
    n_side = len(side)
    o_ref, s_all, tail_ref = rest[n_alias + n_side:n_alias + n_side + 3]
    xc_ref = rest[-1]
    s_ref = _state_view(s_all, fresh_layer)

    @pl.when(pl.program_id(1) == 0)
    def _():
        _zero_other_slots(s_all, fresh_layer)
        s_ref[...] = s0_ref[...]
        xc_ref[:, :8, :] = buf_ref[...]

    for q in range(nseq):
        xc_ref[q, 8:, :] = x_ref[q * c:(q + 1) * c, :]
        tail_ref[q] = x_ref[(q + 1) * c - 8:(q + 1) * c, :]

    def conv(q, lo):
        xcol = xc_ref[q, :, lo:lo + 128]
        acc = xcol[8:] * cw_ref[CONV_W - 1:CONV_W, lo:lo + 128]
        for s in range(1, CONV_W):
            acc = acc + pltpu.roll(xcol, s, axis=0)[8:] * cw_ref[CONV_W - 1 - s:CONV_W - s, lo:lo + 128]
        return acc * _sigmoid(acc)

    def l2n(t, scale=1.0):
        return t * (lax.rsqrt(jnp.sum(t * t, axis=-1, keepdims=True) + EPS) * scale)

    ii = lax.broadcasted_iota(jnp.int32, (c, c), 0)
    jj = lax.broadcasted_iota(jnp.int32, (c, c), 1)
    tri = ii >= jj
    strict = ii > jj
    ng = ng_ref[...]
    hk = H_A * DK_A
    sb = min(c, SOLVE_BLOCK)
    for h0 in range(0, H_A, group):
        st = []
        for q in range(nseq):
            rows = slice(q * c, (q + 1) * c)
            for h in range(h0, h0 + group):
                lo = h * DK_A
                qh = l2n(conv(q, lo), DK_A ** -0.5)
                k = l2n(conv(q, hk + lo))
                v = conv(q, 2 * hk + lo)
                gc = gcol_ref[rows, h:h + 1]
                beta = bcol_ref[rows, h:h + 1]
                gr = grow_ref[q, 0, h:h + 1, :]
                glast = gr[:, c - 1:c]
                decay = jnp.exp(jnp.where(tri, gc - gr, -jnp.inf))
                egc = jnp.exp(gc)
                kb = k * beta
                m = lax.dot_general(jnp.concatenate([kb, qh], axis=0).astype(BF16), k.astype(BF16),
                                    (((1,), (1,)), ((), ())), preferred_element_type=F32)
                s = s_ref[q, h]
                sq = _bdot(jnp.concatenate([kb * egc, qh * egc], axis=0), s)
                st.append(dict(q=q, h=h, rows=rows, m=m, sq=sq, s=s, decay=decay, vb=v * beta,
                               kd=(k * jnp.exp(glast - gc)).astype(BF16), cd=jnp.exp(glast)))
        for d in st:
            d["p"] = -jnp.where(strict, d["m"][:c] * d["decay"], 0.0)
            d["pb"] = d["p"].astype(BF16)
            d["attn"] = (d["m"][c:] * d["decay"]).astype(BF16)
            d["r"] = d["vb"] - d["sq"][:c]
            d["xs"] = []
        for b0 in range(0, c, sb):
            for d in st:
                d["xb"] = d["r"][b0:b0 + sb]
                if b0:
                    solved = jnp.concatenate(d["xs"] + [jnp.zeros((c - b0, DV_A), BF16)], axis=0)
                    d["xb"] = d["xb"] + jnp.dot(d["pb"][b0:b0 + sb], solved, preferred_element_type=F32)
            for t in range(sb - 1):
                for d in st:
                    d["xb"] = d["xb"] + d["p"][b0:b0 + sb, b0 + t:b0 + t + 1] * d["xb"][t:t + 1]
            for d in st:
                d["xs"].append(d["xb"].astype(BF16))
        for d in st:
            d["x"] = d["xs"][0] if len(d["xs"]) == 1 else jnp.concatenate(d["xs"], axis=0)
        outs = {}
        for d in st:
            xb = d["x"]
            o = d["sq"][c:] + jnp.dot(d["attn"], xb, preferred_element_type=F32)
            s_ref[d["q"], d["h"]] = d["s"] * d["cd"] + lax.dot_general(
                d["kd"], xb, (((0,), (0,)), ((), ())), preferred_element_type=F32)
            lo = d["h"] * DV_A
            o = o * lax.rsqrt(jnp.mean(o * o, axis=-1, keepdims=True) + EPS) * ng
            z = z_ref[d["rows"], lo:lo + DV_A]
            outs[d["q"], d["h"]] = o * (z * _sigmoid(z))
        for h in range(h0, h0 + group):
            o = outs[0, h] if nseq == 1 else jnp.concatenate([outs[q, h] for q in range(nseq)], axis=0)
            o_ref[:, h * DV_A:(h + 1) * DV_A] = o.astype(o_ref.dtype)
    xc_ref[:, :8, :] = xc_ref[:, c:c + 8, :]
    _run_side_casts(side, steps_inner, rest[n_alias:n_alias + n_side], rest[n_alias + n_side + 3:-1])


def _state_out(stack, depth, layer_out, n, nseq, inner):
    if stack is None:
        return (pl.BlockSpec((depth, nseq) + inner, lambda i, j: (0, i, 0, 0, 0)),
                jax.ShapeDtypeStruct((depth, n) + inner, F32))
    return (pl.BlockSpec((None, nseq) + inner, lambda i, j: (layer_out, i, 0, 0, 0)),
            jax.ShapeDtypeStruct(stack.shape, F32))


def _gdn(proj, gc, beta, grow, s0, layer, stack, depth, layer_out, o_buf, buf8, conv_w, norm_g, row_off, n, l, c,
         group, nseq, side=()):
    nc = l // c
    assert nseq == 1 or nc == 1
    blk = nseq * c
    base = row_off // blk
    rows = lambda w, col: pl.BlockSpec((blk, w), lambda i, j: (base + i * nc + j, col))
    state_spec, state_shape = _state_out(stack, depth, layer_out, n, nseq, (H_A, DK_A, DV_A))
    aliased = ([stack] if stack is not None else []) + [o_buf]
    n_in = 9
    assert all(job.n_blocks <= (n // nseq) * nc for job in side)
    side_in, side_out, side_shape = _side_specs(side, nc)
    return pl.pallas_call(
        functools.partial(_gdn_kernel, c=c, group=group, nseq=nseq,
                          fresh_layer=layer_out if stack is None else None,
                          n_alias=len(aliased), side=tuple(side), steps_inner=nc),
        grid=(n // nseq, nc),
        in_specs=[rows(CONV_DIM, 0), rows(BRANCH_W, COL_Z // BRANCH_W), rows(LANES, 0), rows(LANES, 0),
                  pl.BlockSpec((nseq, 1, H_A, c), lambda i, j: (i, j, 0, 0)),
                  pl.BlockSpec((None, nseq, H_A, DK_A, DV_A), lambda i, j: (layer, i, 0, 0, 0)),
                  pl.BlockSpec((nseq, 8, CONV_DIM), lambda i, j: (i, 0, 0)),
                  pl.BlockSpec((CONV_W, CONV_DIM), lambda i, j: (0, 0)),
                  pl.BlockSpec((1, DV_A), lambda i, j: (0, 0))]
        + [pl.BlockSpec(memory_space=pl.ANY)] * len(aliased) + side_in,
        out_specs=[rows(BRANCH_W, 0), state_spec, pl.BlockSpec((nseq, 8, CONV_DIM), lambda i, j: (i, 0, 0))]
        + side_out,
        out_shape=[jax.ShapeDtypeStruct(o_buf.shape, o_buf.dtype), state_shape,
                   jax.ShapeDtypeStruct((n, 8, CONV_DIM), F32)] + side_shape,
        scratch_shapes=[pltpu.VMEM((nseq, c + 8, CONV_DIM), F32)],
        input_output_aliases={n_in + len(aliased) - 1: 0, **({n_in: 1} if stack is not None else {})},
        compiler_params=_cparams(("arbitrary", "arbitrary")))(
            proj, proj, gc, beta, grow, s0, buf8, conv_w, norm_g.reshape(1, DV_A), *aliased,
            *[job.w for job in side])


def _ret_kernel(qk_ref, v_ref, gate_ref, cos_ref, sin_ref, r0_ref, *rest, c, nseq, fresh_layer):
    o_ref = rest[-2]
    r_ref = _state_view(rest[-1], fresh_layer)

    @pl.when(pl.program_id(1) == 0)
    def _():
        _zero_other_slots(rest[-1], fresh_layer)
        r_ref[...] = r0_ref[...]

    ii = lax.broadcasted_iota(jnp.int32, (c, c), 0)
    jj = lax.broadcasted_iota(jnp.int32, (c, c), 1)
    rel = (ii - jj).astype(F32)
    idx = lax.broadcasted_iota(jnp.int32, (c, 1), 0).astype(F32)
    even = (lax.broadcasted_iota(jnp.int32, (c, DK_B), 1) & 1) == 0
    cos = cos_ref[...]
    sin = sin_ref[...]

    def rot(x):
        swapped = jnp.where(even, pltpu.roll(x, DK_B - 1, axis=1), pltpu.roll(x, 1, axis=1))
        return x * cos + swapped * sin

    units = [(q, h) for q in range(nseq) for h in range(H_B)]
    lg = {h: math.log1p(-2.0 ** (-5.0 - h)) for h in range(H_B)}
    rows = {q: slice(q * c, (q + 1) * c) for q in range(nseq)}
    qs = {u: rot(qk_ref[rows[u[0]], u[1] * DK_B:(u[1] + 1) * DK_B]) for u in units}
    ks = {u: rot(qk_ref[rows[u[0]], (H_B + u[1]) * DK_B:(H_B + u[1] + 1) * DK_B]) * (DK_B ** -0.5)
          for u in units}
    vs = {u: v_ref[rows[u[0]], u[1] * DV_B:(u[1] + 1) * DV_B].astype(BF16) for u in units}
    rs = {u: r_ref[u] for u in units}
    scores = {u: _bdot_nt(qs[u], ks[u]) for u in units}
    cross = {u: _bdot(qs[u] * jnp.exp(lg[u[1]] * (idx + 1.0)), rs[u]) for u in units}
    for u in units:
        r_ref[u] = rs[u] * math.exp(lg[u[1]] * c) + _bdot_tn(ks[u] * jnp.exp(lg[u[1]] * (c - 1.0 - idx)), vs[u])
    inners = {u: _bdot(scores[u] * jnp.where(rel >= 0, jnp.exp(lg[u[1]] * jnp.maximum(rel, 0.0)), 0.0), vs[u])
              for u in units}
    outs = {}
    for u in units:
        o = inners[u] + cross[u]
        mu = jnp.mean(o, axis=-1, keepdims=True)
        d = o - mu
        var = jnp.mean(d * d, axis=-1, keepdims=True)
        o = d * lax.rsqrt(var + EPS)
        gate = gate_ref[rows[u[0]], u[1] * DV_B:(u[1] + 1) * DV_B]
        outs[u] = gate * _sigmoid(gate) * o
    for h in range(H_B):
        o = outs[0, h] if nseq == 1 else jnp.concatenate([outs[q, h] for q in range(nseq)], axis=0)
        o_ref[:, h * DV_B:(h + 1) * DV_B] = o.astype(o_ref.dtype)


def _rope_tables(pos):
    half = DK_B // 2
    inv_freq = 1.0 / (ROPE_BASE ** jnp.linspace(0.0, 1.0, half, dtype=F32))
    ang = pos.astype(F32)[:, None] * inv_freq[None, :]
    sin = jnp.sin(ang)
    cos = jnp.cos(ang)
    cos2 = jnp.stack([cos, cos], axis=-1).reshape(-1, DK_B)
    sin2 = jnp.stack([-sin, sin], axis=-1).reshape(-1, DK_B)
    return cos2, sin2


def _retention(proj, cos2, sin2, r0, layer, stack, depth, layer_out, o_buf, row_off, n, l, c, nseq):
    nc = l // c
    assert nseq == 1 or nc == 1
    blk = nseq * c
    base = row_off // blk
    col = lambda off: pl.BlockSpec((blk, BRANCH_W), lambda i, j: (base + i * nc + j, off // BRANCH_W))
    tab = pl.BlockSpec((c, DK_B), lambda i, j: (j, 0))
    state_spec, state_shape = _state_out(stack, depth, layer_out, n, nseq, (H_B, DK_B, DV_B))
    aliased = ([stack] if stack is not None else []) + [o_buf]
    n_in = 6
    return pl.pallas_call(
        functools.partial(_ret_kernel, c=c, nseq=nseq, fresh_layer=layer_out if stack is None else None),
        grid=(n // nseq, nc),
        in_specs=[col(COL_QKB), col(COL_VB), col(COL_GB), tab, tab,
                  pl.BlockSpec((None, nseq, H_B, DK_B, DV_B), lambda i, j: (layer, i, 0, 0, 0))]
        + [pl.BlockSpec(memory_space=pl.ANY)] * len(aliased),
        out_specs=[col(0), state_spec],
        out_shape=[jax.ShapeDtypeStruct(o_buf.shape, o_buf.dtype), state_shape],
        input_output_aliases={n_in + len(aliased) - 1: 0, **({n_in: 1} if stack is not None else {})},
        compiler_params=_cparams(("parallel", "arbitrary")))(proj, proj, proj, cos2, sin2, r0, *aliased)


def _gelu(x):
    return 0.5 * x * (1.0 + lax.erf(x * (2.0 ** -0.5)))


def _gmlp_kernel(u_ref, v_ref, w_ref, bcol_ref, lng_ref, lnb_ref, obuf_ref, o_ref, *vrows_ref, seq):
    u32 = _gelu(u_ref[...])
    v32 = _gelu(v_ref[...])
    mu = jnp.mean(v32, axis=-1, keepdims=True)
    d = v32 - mu
    var = jnp.mean(d * d, axis=-1, keepdims=True)
    vn = d * lax.rsqrt(var + EPS) * lng_ref[...] + lnb_ref[...]
    if vrows_ref:
        vrows_ref[0][...] = vn
    ii = lax.broadcasted_iota(jnp.int32, (CHUNK_C, CHUNK_C), 0)
    jj = lax.broadcasted_iota(jnp.int32, (CHUNK_C, CHUNK_C), 1)
    mask = (ii >= jj) & ((ii & -seq) == (jj & -seq))
    for g in range(G_C):
        w = jnp.where(mask, w_ref[g], 0.0)
        mixed = _bdot(w, vn[:, g * DG_C:(g + 1) * DG_C]) + bcol_ref[:, g:g + 1]
        o_ref[:, g * DG_C:(g + 1) * DG_C] = (u32[:, g * DG_C:(g + 1) * DG_C] * mixed).astype(o_ref.dtype)


def _gmlp(proj, w_tile, b_col, ln_g, ln_b, o_buf, row_off, rows, seq, want_vrows):
    base = row_off // CHUNK_C
    col = lambda off: pl.BlockSpec((CHUNK_C, BRANCH_W), lambda i: (base + i, off // BRANCH_W))
    vec = pl.BlockSpec((1, BRANCH_W), lambda i: (0, 0))
    out_shape = [jax.ShapeDtypeStruct(o_buf.shape, o_buf.dtype)]
    out_specs = [col(0)]
    if want_vrows:
        out_shape.append(jax.ShapeDtypeStruct((rows, BRANCH_W), F32))
        out_specs.append(pl.BlockSpec((CHUNK_C, BRANCH_W), lambda i: (i, 0)))
    res = pl.pallas_call(
        functools.partial(_gmlp_kernel, seq=seq), grid=(rows // CHUNK_C,),
        in_specs=[col(COL_U), col(COL_V),
                  pl.BlockSpec((G_C, CHUNK_C, CHUNK_C), lambda i: (0, 0, 0)),
                  pl.BlockSpec((CHUNK_C, G_C), lambda i: (0, 0)), vec, vec,
                  pl.BlockSpec(memory_space=pl.ANY)],
        out_specs=out_specs, out_shape=out_shape, input_output_aliases={6: 0},
        compiler_params=_cparams(("parallel",)))(
            proj, proj, w_tile, b_col, ln_g.reshape(1, BRANCH_W), ln_b.reshape(1, BRANCH_W), o_buf)
    return res if want_vrows else (res[0], None)


AB_COL0 = 4 * BRANCH_W
AB_COLS = 2 * H_A
WD_BLK, WBR_BLK, PACK_BLK = 64, 32, 128
FFN_BLK = 64


def _pack_kernel(w_ref, o_ref):
    o_ref[...] = w_ref[0].astype(BF16)


def _pack_src_row(j, tn):
    r = j * tn
    return r + jnp.where(r >= AB_COL0, AB_COLS, 0)


def _pack_w_in_t(w_in_t, layer, tn):
    k = w_in_t.shape[2]
    return pl.pallas_call(
        _pack_kernel, grid=(N_PROJ // tn,),
        in_specs=[pl.BlockSpec((pl.Element(1), pl.Element(tn), pl.Element(k)),
                               lambda j: (layer, pl.multiple_of(_pack_src_row(j, tn), 8), 0))],
        out_specs=pl.BlockSpec((tn, k), lambda j: (j, 0)),
        out_shape=jax.ShapeDtypeStruct((N_PROJ, k), BF16),
        compiler_params=_cparams(("parallel",)))(w_in_t)


def _ab_weights_t(w_in_t, layer):
    ab = w_in_t[layer, AB_COL0:AB_COL0 + AB_COLS, :]
    zeros = jnp.zeros((LANES - H_A, ab.shape[1]), F32)
    return jnp.concatenate([ab[:H_A], zeros, ab[H_A:], zeros], axis=0)


def kernel(x_prompt, x_sample, state_gdn, state_conv, state_ret, w_in, conv_w, a_log, dt_bias, gdn_norm, w_s, b_s, ln_c_g, ln_c_b, w_br, w_o, g_pre_mix, g_post_mix, g_pre_ffn, g_post_ffn, w_ffn_gate, w_ffn_up, w_ffn_down):
    n_p, l_p, _ = x_prompt.shape
    n_s, l_s, _ = x_sample.shape
    depth = w_in.shape[0]
    rows_p, rows_s = n_p * l_p, n_s * l_s
    m = rows_p + rows_s
    xs = (x_prompt.reshape(rows_p, D_MODEL), x_sample.reshape(rows_s, D_MODEL))
    tr = ROW_TILE

    cos_p, sin_p = _rope_tables(jnp.arange(l_p))
    cos_s, sin_s = _rope_tables(PAST_LEN + jnp.arange(l_s))
    zero_gdn = jnp.zeros((1, n_p, H_A, DK_A, DV_A), F32)
    zero_ret = jnp.zeros((1, n_p, H_B, DK_B, DV_B), F32)
    zero_buf8 = jnp.zeros((n_p, 8, CONV_DIM), F32)
    reps = CHUNK_C // l_s

    w_in_t = jnp.swapaxes(w_in, 1, 2)

    gdn_p = gdn_s = ret_p = ret_s = None
    outs = {k: [] for k in ("conv_p", "conv_s", "vrows")}
    h = _prenorm(xs[0], xs[1], g_pre_mix[0], tr)
    w_br_rows = w_br.reshape(depth, 3 * BRANCH_W, D_MODEL)
    w_main = _pack_w_in_t(w_in_t, 0, TN_HALF)
    for l in range(depth):
        side = [_SideCast(w_ffn_down, l, WD_BLK, D_FF_PAD // WD_BLK,
                          lambda b: jnp.minimum(b, D_FF // WD_BLK - 1) * WD_BLK, D_FF // WD_BLK),
                _SideCast(w_br_rows, l, WBR_BLK, 3 * BRANCH_W // WBR_BLK, lambda b: b * WBR_BLK,
                          3 * BRANCH_W // WBR_BLK)]
        if l + 1 < depth:
            side.append(_SideCast(w_in_t, l + 1, PACK_BLK, N_PROJ // PACK_BLK,
                                  lambda b: _pack_src_row(b, PACK_BLK), N_PROJ // PACK_BLK))
        proj, wd, w_br_l, *w_main = _matmul_nt(h, w_main, F32, TM, TN, side)
        w_main = w_main[0] if w_main else None
        ab = _matmul_nt(h, _ab_weights_t(w_in_t, l), F32, TM, 2 * LANES)[0]

        gc, beta = _gdn_prep(ab, a_log[l], dt_bias[l], TM, rows_p)
        grow_p = gc[:rows_p, :H_A].reshape(n_p, l_p // MIX_CHUNK, MIX_CHUNK, H_A).transpose(0, 1, 3, 2)
        grow_s = gc[rows_p:, :H_A].reshape(n_s, 1, l_s, H_A).transpose(0, 1, 3, 2)
        buf8_s = jnp.pad(state_conv[l], ((0, 0), (8 - (CONV_W - 1), 0), (0, 0)))
        o_a = jnp.zeros((m, BRANCH_W), BF16)
        o_b = jnp.zeros((m, BRANCH_W), BF16)
        o_c = jnp.zeros((m, BRANCH_W), BF16)
        ffn_side = [_SideCast(w, l, FFN_BLK, D_MODEL // FFN_BLK, lambda b: b * FFN_BLK, D_MODEL // FFN_BLK)
                    for w in (w_ffn_gate, w_ffn_up)]
        o_a, gdn_p, tail_p, wg, wu = _gdn(proj, gc, beta, grow_p, zero_gdn, 0, gdn_p, depth, l, o_a, zero_buf8,
                                          conv_w[l], gdn_norm[l], 0, n_p, l_p, MIX_CHUNK, H_A, 1, ffn_side)
        o_a, gdn_s, tail_s = _gdn(proj, gc, beta, grow_s, state_gdn, l, gdn_s, depth, l, o_a, buf8_s, conv_w[l],
                                  gdn_norm[l], rows_p, n_s, l_s, l_s, GDN_GROUP, GDN_SAMPLE_SEQS)
        outs["conv_p"].append(tail_p[:, 8 - (CONV_W - 1):])
        outs["conv_s"].append(tail_s[:, 8 - (CONV_W - 1):])

        o_b, ret_p = _retention(proj, cos_p, sin_p, zero_ret, 0, ret_p, depth, l, o_b, 0, n_p, l_p, RET_CHUNK, 1)
        o_b, ret_s = _retention(proj, cos_s, sin_s, state_ret, l, ret_s, depth, l, o_b, rows_p, n_s, l_s, l_s,
                                RET_SAMPLE_SEQS)

        o_c, _ = _gmlp(proj, w_s[l], b_s[l].T, ln_c_g[l], ln_c_b[l], o_c, 0, rows_p, CHUNK_C, False)
        w_tile_s = jnp.tile(w_s[l][:, :l_s, :l_s], (1, reps, reps))
        b_col_s = jnp.tile(b_s[l][:, :l_s].T, (reps, 1))
        o_c, vrows = _gmlp(proj, w_tile_s, b_col_s, ln_c_g[l], ln_c_b[l], o_c, rows_p, rows_s, l_s, True)
        outs["vrows"].append(vrows.reshape(n_s, l_s, BRANCH_W))

        merged = _merge(o_a, o_b, o_c, w_br_l.reshape(3, BRANCH_W, D_MODEL), proj, TM, TN_HALF)
        mix = _matmul_ws(merged, w_o, l, BF16, TM, TN_HALF)
        x, h = _post(xs, mix, g_post_mix[l], g_pre_ffn[l], tr)

        f1 = _ffn_up(h, wg, wu, TM, TN_HALF, D_FF_PAD)
        f = _matmul_k2(f1, wd, BF16, TM, TN)
        if l + 1 < depth:
            x, h = _post((x,), f, g_post_ffn[l], g_pre_mix[l + 1], tr)
            xs = (x,)
        else:
            y_p, y_s = _post_split(x, f, g_post_ffn[l], rows_p, tr)

    y_p = y_p.reshape(n_p, l_p, D_MODEL)
    y_s = y_s.reshape(n_s, l_s, D_MODEL)
    st = lambda k: jnp.stack(outs[k])
    return (y_p, y_s, gdn_p, gdn_s, st("conv_p"), st("conv_s"), ret_p, ret_s, st("vrows"))
```

```python
import functools
import math
from typing import Callable, NamedTuple

import jax
import jax.numpy as jnp
from jax import lax
from jax.experimental import pallas as pl
from jax.experimental.pallas import tpu as pltpu

F32 = jnp.float32
BF16 = jnp.bfloat16

D_MODEL = 4096
BRANCH_W = 2048
H_A, DK_A, DV_A = 16, 128, 128
CONV_W = 4
CONV_DIM = 6144
H_B, DK_B, DV_B = 8, 128, 256
G_C, DG_C, CHUNK_C = 8, 256, 128
D_FF = 11008
D_FF_PAD = 11264
EPS = 1e-6
ROPE_BASE = 10000.0
PAST_LEN = 16384
MIX_CHUNK = 64
RET_CHUNK = 256
SOLVE_BLOCK = 8
GDN_GROUP = 8
GDN_SAMPLE_SEQS = 4
RET_SAMPLE_SEQS = 4

COL_QKV, COL_Z, COL_QKB, COL_VB, COL_GB, COL_U, COL_V, COL_GATE = (
    0, 6144, 8192, 10240, 12288, 14336, 16384, 18432)
N_PROJ = 30720
TM, TN, TN_HALF, ROW_TILE = 1024, 1024, 512, 256
LANES = 128
VMEM_LIMIT = 56 * 1024 * 1024
VMEM_LIMIT_HOST = 60 * 1024 * 1024


def _cparams(sem, vmem=VMEM_LIMIT):
    return pltpu.CompilerParams(dimension_semantics=sem, vmem_limit_bytes=vmem)


def _sigmoid(x):
    return 0.5 * jnp.tanh(0.5 * x) + 0.5


def _bdot(a, b):
    return jnp.dot(a.astype(BF16), b.astype(BF16), preferred_element_type=F32)


def _bdot_nt(a, b):
    return lax.dot_general(a.astype(BF16), b.astype(BF16), (((1,), (1,)), ((), ())),
                           preferred_element_type=F32)


def _bdot_tn(a, b):
    return lax.dot_general(a.astype(BF16), b.astype(BF16), (((0,), (0,)), ((), ())),
                           preferred_element_type=F32)


class _SideCast(NamedTuple):
    w: jax.Array
    layer: int
    blk: int
    n_blocks: int
    src_row: Callable
    n_copy: int


def _side_specs(side, steps_inner):
    def blk_idx(job, g0, g1):
        return jnp.minimum(g0 * steps_inner + g1, job.n_blocks - 1)

    in_specs = [pl.BlockSpec((pl.Element(1), pl.Element(job.blk), pl.Element(job.w.shape[2])),
                             lambda g0, g1, job=job: (job.layer,
                                                      pl.multiple_of(job.src_row(blk_idx(job, g0, g1)), 8), 0))
                for job in side]
    out_specs = [pl.BlockSpec((job.blk, job.w.shape[2]), lambda g0, g1, job=job: (blk_idx(job, g0, g1), 0))
                 for job in side]
    out_shape = [jax.ShapeDtypeStruct((job.n_blocks * job.blk, job.w.shape[2]), BF16) for job in side]
    return in_specs, out_specs, out_shape


def _run_side_casts(side, steps_inner, w_refs, o_refs):
    t = pl.program_id(0) * steps_inner + pl.program_id(1)
    for job, w_ref, so_ref in zip(side, w_refs, o_refs):
        v = w_ref[0].astype(BF16)
        so_ref[...] = v if job.n_copy >= job.n_blocks else jnp.where(t < job.n_copy, v, jnp.zeros_like(v))


def _mm_nt_kernel(a_ref, bt_ref, *refs, side, steps_inner):
    n_side = len(side)
    o_ref = refs[n_side]
    _run_side_casts(side, steps_inner, refs[:n_side], refs[n_side + 1:])
    o_ref[...] = lax.dot_general(a_ref[...], bt_ref[...].astype(BF16), (((1,), (1,)), ((), ())),
                                 preferred_element_type=F32).astype(o_ref.dtype)


def _matmul_nt(a, bt, out_dtype, tm, tn, side=()):
    m, k = a.shape
    n = bt.shape[0]
    nj = n // tn
    assert all(job.n_blocks <= (m // tm) * nj for job in side)
    side_in, side_out, side_shape = _side_specs(side, nj)
    return pl.pallas_call(
        functools.partial(_mm_nt_kernel, side=tuple(side), steps_inner=nj), grid=(m // tm, nj),
        in_specs=[pl.BlockSpec((tm, k), lambda i, j: (i, 0)),
                  pl.BlockSpec((tn, k), lambda i, j: (j, 0))] + side_in,
        out_specs=[pl.BlockSpec((tm, tn), lambda i, j: (i, j))] + side_out,
        out_shape=[jax.ShapeDtypeStruct((m, n), out_dtype)] + side_shape,
        compiler_params=_cparams(("arbitrary", "arbitrary"), VMEM_LIMIT_HOST if side else VMEM_LIMIT))(
            a, bt, *[job.w for job in side])


def _mm_ws_kernel(a_ref, b_ref, o_ref, bw_ref):
    @pl.when(pl.program_id(1) == 0)
    def _():
        bw_ref[...] = b_ref[...].astype(BF16)

    o_ref[...] = jnp.dot(a_ref[...], bw_ref[...], preferred_element_type=F32).astype(o_ref.dtype)


def _matmul_ws(a, b, layer, out_dtype, tm, tn):
    m, k = a.shape
    n = b.shape[2]
    return pl.pallas_call(
        _mm_ws_kernel, grid=(n // tn, m // tm),
        in_specs=[pl.BlockSpec((tm, k), lambda j, i: (i, 0)),
                  pl.BlockSpec((None, k, tn), lambda j, i: (layer, 0, j))],
        out_specs=pl.BlockSpec((tm, tn), lambda j, i: (i, j)),
        out_shape=jax.ShapeDtypeStruct((m, n), out_dtype),
        scratch_shapes=[pltpu.VMEM((k, tn), BF16)],
        compiler_params=_cparams(("arbitrary", "arbitrary")))(a, b)


def _mm_k2_kernel(a_ref, b_ref, o_ref, acc_ref):
    @pl.when(pl.program_id(2) == 0)
    def _():
        acc_ref[...] = jnp.dot(a_ref[...], b_ref[...], preferred_element_type=F32)

    @pl.when(pl.program_id(2) == 1)
    def _():
        o_ref[...] = (acc_ref[...] + jnp.dot(a_ref[...], b_ref[...], preferred_element_type=F32)).astype(o_ref.dtype)


def _matmul_k2(a, b, out_dtype, tm, tn):
    m, k = a.shape
    n = b.shape[1]
    tk = k // 2
    return pl.pallas_call(
        _mm_k2_kernel, grid=(m // tm, n // tn, 2),
        in_specs=[pl.BlockSpec((tm, tk), lambda i, j, kk: (i, kk)),
                  pl.BlockSpec((tk, tn), lambda i, j, kk: (kk, j))],
        out_specs=pl.BlockSpec((tm, tn), lambda i, j, kk: (i, j)),
        out_shape=jax.ShapeDtypeStruct((m, n), out_dtype),
        scratch_shapes=[pltpu.VMEM((tm, tn), F32)],
        compiler_params=_cparams(("parallel", "parallel", "arbitrary")))(a, b)


def _ffn_up_kernel(h_ref, wg_ref, wu_ref, o_ref, *, tn, n_valid):
    h = h_ref[...]
    g = jnp.dot(h, wg_ref[...], preferred_element_type=F32)
    u = jnp.dot(h, wu_ref[...], preferred_element_type=F32)
    col = pl.program_id(1) * tn + lax.broadcasted_iota(jnp.int32, g.shape, 1)
    o_ref[...] = jnp.where(col < n_valid, g * _sigmoid(g) * u, 0.0).astype(o_ref.dtype)


def _ffn_up(h, wg, wu, tm, tn, n_out):
    m, k = h.shape
    n = wg.shape[1]
    wspec = pl.BlockSpec((k, tn), lambda i, j: (0, j))
    return pl.pallas_call(
        functools.partial(_ffn_up_kernel, tn=tn, n_valid=n), grid=(m // tm, n_out // tn),
        in_specs=[pl.BlockSpec((tm, k), lambda i, j: (i, 0)), wspec, wspec],
        out_specs=pl.BlockSpec((tm, tn), lambda i, j: (i, j)),
        out_shape=jax.ShapeDtypeStruct((m, n_out), BF16),
        compiler_params=_cparams(("parallel", "parallel")))(h, wg, wu)


def _merge_kernel(oa_ref, ob_ref, oc_ref, w_ref, ga_ref, gb_ref, gc_ref, o_ref):
    acc = _sigmoid(ga_ref[...]) * jnp.dot(oa_ref[...], w_ref[0], preferred_element_type=F32)
    acc = acc + _sigmoid(gb_ref[...]) * jnp.dot(ob_ref[...], w_ref[1], preferred_element_type=F32)
    acc = acc + _sigmoid(gc_ref[...]) * jnp.dot(oc_ref[...], w_ref[2], preferred_element_type=F32)
    o_ref[...] = acc.astype(o_ref.dtype)


def _merge(o_a, o_b, o_c, w_br, proj, tm, tn):
    m = o_a.shape[0]
    gate_blk0 = COL_GATE // tn
    per_branch = D_MODEL // tn
    o_spec = pl.BlockSpec((tm, BRANCH_W), lambda i, j: (i, 0))
    gate = lambda b: pl.BlockSpec((tm, tn), lambda i, j: (i, gate_blk0 + b * per_branch + j))
    return pl.pallas_call(
        _merge_kernel, grid=(m // tm, D_MODEL // tn),
        in_specs=[o_spec, o_spec, o_spec,
                  pl.BlockSpec((3, BRANCH_W, tn), lambda i, j: (0, 0, j)),
                  gate(0), gate(1), gate(2)],
        out_specs=pl.BlockSpec((tm, tn), lambda i, j: (i, j)),
        out_shape=jax.ShapeDtypeStruct((m, D_MODEL), BF16),
        compiler_params=_cparams(("parallel", "parallel")))(o_a, o_b, o_c, w_br, proj, proj, proj)


def _rms(x, g):
    return x * lax.rsqrt(jnp.mean(x * x, axis=-1, keepdims=True) + EPS) * g


def _two_source_specs(tr, nb_first):
    first = pl.BlockSpec((tr, D_MODEL), lambda i: (jnp.minimum(i, nb_first - 1), 0))
    second = pl.BlockSpec((tr, D_MODEL), lambda i: (jnp.maximum(i - nb_first, 0), 0))
    return first, second


def _prenorm_kernel(xa_ref, xb_ref, g_ref, h_ref, *, nb_first):
    i = pl.program_id(0)

    @pl.when(i < nb_first)
    def _():
        h_ref[...] = _rms(xa_ref[...], g_ref[...]).astype(h_ref.dtype)

    @pl.when(i >= nb_first)
    def _():
        h_ref[...] = _rms(xb_ref[...], g_ref[...]).astype(h_ref.dtype)


def _prenorm(xa, xb, g, tr):
    ma, mb = xa.shape[0], xb.shape[0]
    sa, sb = _two_source_specs(tr, ma // tr)
    return pl.pallas_call(
        functools.partial(_prenorm_kernel, nb_first=ma // tr), grid=((ma + mb) // tr,),
        in_specs=[sa, sb, pl.BlockSpec((1, D_MODEL), lambda i: (0, 0))],
        out_specs=pl.BlockSpec((tr, D_MODEL), lambda i: (i, 0)),
        out_shape=jax.ShapeDtypeStruct((ma + mb, D_MODEL), BF16),
        compiler_params=_cparams(("arbitrary",)))(xa, xb, g.reshape(1, D_MODEL))


def _post_kernel(*refs, nb_first, n_src):
    x_refs = refs[:n_src]
    y_ref, gp_ref, gn_ref, xo_ref, h_ref = refs[n_src:]
    r = _rms(y_ref[...].astype(F32), gp_ref[...])

    def finish(x_ref):
        xn = x_ref[...] + r
        xo_ref[...] = xn
        h_ref[...] = _rms(xn, gn_ref[...]).astype(BF16)

    if n_src == 1:
        finish(x_refs[0])
    else:
        i = pl.program_id(0)
        pl.when(i < nb_first)(lambda: finish(x_refs[0]))
        pl.when(i >= nb_first)(lambda: finish(x_refs[1]))


def _post(xs, y, g_post, g_next, tr):
    m = y.shape[0]
    row = pl.BlockSpec((tr, D_MODEL), lambda i: (i, 0))
    vec = pl.BlockSpec((1, D_MODEL), lambda i: (0, 0))
    nb_first = xs[0].shape[0] // tr
    x_specs = list(_two_source_specs(tr, nb_first)) if len(xs) == 2 else [row]
    return pl.pallas_call(
        functools.partial(_post_kernel, nb_first=nb_first, n_src=len(xs)), grid=(m // tr,),
        in_specs=x_specs + [row, vec, vec], out_specs=[row, row],
        out_shape=[jax.ShapeDtypeStruct((m, D_MODEL), F32), jax.ShapeDtypeStruct((m, D_MODEL), BF16)],
        compiler_params=_cparams(("arbitrary",)))(
            *xs, y, g_post.reshape(1, D_MODEL), g_next.reshape(1, D_MODEL))


def _post_split_kernel(x_ref, y_ref, gp_ref, oa_ref, ob_ref, *, nb_first):
    i = pl.program_id(0)
    xn = x_ref[...] + _rms(y_ref[...].astype(F32), gp_ref[...])

    @pl.when(i < nb_first)
    def _():
        oa_ref[...] = xn

    @pl.when(i >= nb_first)
    def _():
        ob_ref[...] = xn


def _post_split(x, y, g_post, rows_first, tr):
    m = x.shape[0]
    row = pl.BlockSpec((tr, D_MODEL), lambda i: (i, 0))
    vec = pl.BlockSpec((1, D_MODEL), lambda i: (0, 0))
    oa, ob = _two_source_specs(tr, rows_first // tr)
    return pl.pallas_call(
        functools.partial(_post_split_kernel, nb_first=rows_first // tr), grid=(m // tr,),
        in_specs=[row, row, vec], out_specs=[oa, ob],
        out_shape=[jax.ShapeDtypeStruct((rows_first, D_MODEL), F32),
                   jax.ShapeDtypeStruct((m - rows_first, D_MODEL), F32)],
        compiler_params=_cparams(("arbitrary",)))(x, y, g_post.reshape(1, D_MODEL))


def _state_view(s_all, fresh_layer):
    return s_all if fresh_layer is None else s_all.at[fresh_layer]


def _zero_other_slots(s_all, fresh_layer):
    if fresh_layer is not None:
        for d in range(s_all.shape[0]):
            if d != fresh_layer:
                s_all[d] = jnp.zeros(s_all.shape[1:], s_all.dtype)


def _gdn_prep_kernel(a_ref, b_ref, alog_ref, dt_ref, gc_ref, beta_ref, *, prompt_blocks):
    i = pl.program_id(0)
    x = a_ref[...] + dt_ref[...]
    softplus = jnp.maximum(x, 0.0) + jnp.log1p(jnp.exp(-jnp.abs(x)))
    g = -jnp.exp(alog_ref[...]) * softplus
    chunk = jnp.where(i < prompt_blocks, MIX_CHUNK, 8)
    rmod = lax.broadcasted_iota(jnp.int32, g.shape, 0) & (chunk - 1)
    s = 1
    while s < MIX_CHUNK:
        g = g + jnp.where(rmod >= s, pltpu.roll(g, s, axis=0), 0.0)
        s *= 2
    gc_ref[...] = g
    beta_ref[...] = jax.nn.sigmoid(b_ref[...])


def _gdn_prep(ab, a_log, dt_bias, tr, prompt_rows):
    m = ab.shape[0]
    pad = lambda v: jnp.pad(v.astype(F32), (0, LANES - H_A)).reshape(1, LANES)
    blk = lambda c: pl.BlockSpec((tr, LANES), lambda i: (i, c))
    vec = pl.BlockSpec((1, LANES), lambda i: (0, 0))
    return pl.pallas_call(
        functools.partial(_gdn_prep_kernel, prompt_blocks=prompt_rows // tr), grid=(m // tr,),
        in_specs=[blk(0), blk(1), vec, vec], out_specs=[blk(0), blk(0)],
        out_shape=[jax.ShapeDtypeStruct((m, LANES), F32)] * 2,
        compiler_params=_cparams(("parallel",)))(ab, ab, pad(a_log), pad(dt_bias))


def _gdn_kernel(x_ref, z_ref, gcol_ref, bcol_ref, grow_ref, s0_ref, buf_ref, cw_ref, ng_ref, *rest, c, group, nseq,
                fresh_layer, n_alias, side, steps_inner):
    n_side = len(side)
    o_ref, s_all, tail_ref = rest[n_alias + n_side:n_alias + n_side + 3]
    xc_ref = rest[-1]
    s_ref = _state_view(s_all, fresh_layer)

    @pl.when(pl.program_id(1) == 0)
    def _():
        _zero_other_slots(s_all, fresh_layer)
        s_ref[...] = s0_ref[...]
        xc_ref[:, :8, :] = buf_ref[...]

    for q in range(nseq):
        xc_ref[q, 8:, :] = x_ref[q * c:(q + 1) * c, :]
        tail_ref[q] = x_ref[(q + 1) * c - 8:(q + 1) * c, :]

    def conv(q, lo):
        xcol = xc_ref[q, :, lo:lo + 128]
        acc = xcol[8:] * cw_ref[CONV_W - 1:CONV_W, lo:lo + 128]
        for s in range(1, CONV_W):
            acc = acc + pltpu.roll(xcol, s, axis=0)[8:] * cw_ref[CONV_W - 1 - s:CONV_W - s, lo:lo + 128]
        return acc * _sigmoid(acc)

    def l2n(t, scale=1.0):
        return t * (lax.rsqrt(jnp.sum(t * t, axis=-1, keepdims=True) + EPS) * scale)

    ii = lax.broadcasted_iota(jnp.int32, (c, c), 0)
    jj = lax.broadcasted_iota(jnp.int32, (c, c), 1)
    tri = ii >= jj
    strict = ii > jj
    ng = ng_ref[...]
    hk = H_A * DK_A
    sb = min(c, SOLVE_BLOCK)
    for h0 in range(0, H_A, group):
        st = []
        for q in range(nseq):
            rows = slice(q * c, (q + 1) * c)
            for h in range(h0, h0 + group):
                lo = h * DK_A
                qh = l2n(conv(q, lo), DK_A ** -0.5)
                k = l2n(conv(q, hk + lo))
                v = conv(q, 2 * hk + lo)
                gc = gcol_ref[rows, h:h + 1]
                beta = bcol_ref[rows, h:h + 1]
                gr = grow_ref[q, 0, h:h + 1, :]
                glast = gr[:, c - 1:c]
                decay = jnp.exp(jnp.where(tri, gc - gr, -jnp.inf))
                egc = jnp.exp(gc)
                kb = k * beta
                m = lax.dot_general(jnp.concatenate([kb, qh], axis=0).astype(BF16), k.astype(BF16),
                                    (((1,), (1,)), ((), ())), preferred_element_type=F32)
                s = s_ref[q, h]
                sq = _bdot(jnp.concatenate([kb * egc, qh * egc], axis=0), s)
                st.append(dict(q=q, h=h, rows=rows, m=m, sq=sq, s=s, decay=decay, vb=v * beta,
                               kd=(k * jnp.exp(glast - gc)).astype(BF16), cd=jnp.exp(glast)))
        for d in st:
            d["p"] = -jnp.where(strict, d["m"][:c] * d["decay"], 0.0)
            d["pb"] = d["p"].astype(BF16)
            d["attn"] = (d["m"][c:] * d["decay"]).astype(BF16)
            d["r"] = d["vb"] - d["sq"][:c]
            d["xs"] = []
        for b0 in range(0, c, sb):
            for d in st:
                d["xb"] = d["r"][b0:b0 + sb]
                if b0:
                    solved = jnp.concatenate(d["xs"] + [jnp.zeros((c - b0, DV_A), BF16)], axis=0)
                    d["xb"] = d["xb"] + jnp.dot(d["pb"][b0:b0 + sb], solved, preferred_element_type=F32)
            for t in range(sb - 1):
                for d in st:
                    d["xb"] = d["xb"] + d["p"][b0:b0 + sb, b0 + t:b0 + t + 1] * d["xb"][t:t + 1]
            for d in st:
                d["xs"].append(d["xb"].astype(BF16))
        for d in st:
            d["x"] = d["xs"][0] if len(d["xs"]) == 1 else jnp.concatenate(d["xs"], axis=0)
        outs = {}
        for d in st:
            xb = d["x"]
            o = d["sq"][c:] + jnp.dot(d["attn"], xb, preferred_element_type=F32)
            s_ref[d["q"], d["h"]] = d["s"] * d["cd"] + lax.dot_general(
                d["kd"], xb, (((0,), (0,)), ((), ())), preferred_element_type=F32)
            lo = d["h"] * DV_A
            o = o * lax.rsqrt(jnp.mean(o * o, axis=-1, keepdims=True) + EPS) * ng
            z = z_ref[d["rows"], lo:lo + DV_A]
            outs[d["q"], d["h"]] = o * (z * _sigmoid(z))
        for h in range(h0, h0 + group):
            o = outs[0, h] if nseq == 1 else jnp.concatenate([outs[q, h] for q in range(nseq)], axis=0)
            o_ref[:, h * DV_A:(h + 1) * DV_A] = o.astype(o_ref.dtype)
    xc_ref[:, :8, :] = xc_ref[:, c:c + 8, :]
    _run_side_casts(side, steps_inner, rest[n_alias:n_alias + n_side], rest[n_alias + n_side + 3:-1])


def _state_out(stack, depth, layer_out, n, nseq, inner):
    if stack is None:
        return (pl.BlockSpec((depth, nseq) + inner, lambda i, j: (0, i, 0, 0, 0)),
                jax.ShapeDtypeStruct((depth, n) + inner, F32))
    return (pl.BlockSpec((None, nseq) + inner, lambda i, j: (layer_out, i, 0, 0, 0)),
            jax.ShapeDtypeStruct(stack.shape, F32))


def _gdn(proj, gc, beta, grow, s0, layer, stack, depth, layer_out, o_buf, buf8, conv_w, norm_g, row_off, n, l, c,
         group, nseq, side=()):
    nc = l // c
    assert nseq == 1 or nc == 1
    blk = nseq * c
    base = row_off // blk
    rows = lambda w, col: pl.BlockSpec((blk, w), lambda i, j: (base + i * nc + j, col))
    state_spec, state_shape = _state_out(stack, depth, layer_out, n, nseq, (H_A, DK_A, DV_A))
    aliased = ([stack] if stack is not None else []) + [o_buf]
    n_in = 9
    assert all(job.n_blocks <= (n // nseq) * nc for job in side)
    side_in, side_out, side_shape = _side_specs(side, nc)
    return pl.pallas_call(
        functools.partial(_gdn_kernel, c=c, group=group, nseq=nseq,
                          fresh_layer=layer_out if stack is None else None,
                          n_alias=len(aliased), side=tuple(side), steps_inner=nc),
        grid=(n // nseq, nc),
        in_specs=[rows(CONV_DIM, 0), rows(BRANCH_W, COL_Z // BRANCH_W), rows(LANES, 0), rows(LANES, 0),
                  pl.BlockSpec((nseq, 1, H_A, c), lambda i, j: (i, j, 0, 0)),
                  pl.BlockSpec((None, nseq, H_A, DK_A, DV_A), lambda i, j: (layer, i, 0, 0, 0)),
                  pl.BlockSpec((nseq, 8, CONV_DIM), lambda i, j: (i, 0, 0)),
                  pl.BlockSpec((CONV_W, CONV_DIM), lambda i, j: (0, 0)),
                  pl.BlockSpec((1, DV_A), lambda i, j: (0, 0))]
        + [pl.BlockSpec(memory_space=pl.ANY)] * len(aliased) + side_in,
        out_specs=[rows(BRANCH_W, 0), state_spec, pl.BlockSpec((nseq, 8, CONV_DIM), lambda i, j: (i, 0, 0))]
        + side_out,
        out_shape=[jax.ShapeDtypeStruct(o_buf.shape, o_buf.dtype), state_shape,
                   jax.ShapeDtypeStruct((n, 8, CONV_DIM), F32)] + side_shape,
        scratch_shapes=[pltpu.VMEM((nseq, c + 8, CONV_DIM), F32)],
        input_output_aliases={n_in + len(aliased) - 1: 0, **({n_in: 1} if stack is not None else {})},
        compiler_params=_cparams(("arbitrary", "arbitrary")))(
            proj, proj, gc, beta, grow, s0, buf8, conv_w, norm_g.reshape(1, DV_A), *aliased,
            *[job.w for job in side])


def _ret_kernel(qk_ref, v_ref, gate_ref, cos_ref, sin_ref, r0_ref, *rest, c, nseq, fresh_layer):
    o_ref = rest[-2]
    r_ref = _state_view(rest[-1], fresh_layer)

    @pl.when(pl.program_id(1) == 0)
    def _():
        _zero_other_slots(rest[-1], fresh_layer)
        r_ref[...] = r0_ref[...]

    ii = lax.broadcasted_iota(jnp.int32, (c, c), 0)
    jj = lax.broadcasted_iota(jnp.int32, (c, c), 1)
    rel = (ii - jj).astype(F32)
    idx = lax.broadcasted_iota(jnp.int32, (c, 1), 0).astype(F32)
    even = (lax.broadcasted_iota(jnp.int32, (c, DK_B), 1) & 1) == 0
    cos = cos_ref[...]
    sin = sin_ref[...]

    def rot(x):
        swapped = jnp.where(even, pltpu.roll(x, DK_B - 1, axis=1), pltpu.roll(x, 1, axis=1))
        return x * cos + swapped * sin

    units = [(q, h) for q in range(nseq) for h in range(H_B)]
    lg = {h: math.log1p(-2.0 ** (-5.0 - h)) for h in range(H_B)}
    rows = {q: slice(q * c, (q + 1) * c) for q in range(nseq)}
    qs = {u: rot(qk_ref[rows[u[0]], u[1] * DK_B:(u[1] + 1) * DK_B]) for u in units}
    ks = {u: rot(qk_ref[rows[u[0]], (H_B + u[1]) * DK_B:(H_B + u[1] + 1) * DK_B]) * (DK_B ** -0.5)
          for u in units}
    vs = {u: v_ref[rows[u[0]], u[1] * DV_B:(u[1] + 1) * DV_B].astype(BF16) for u in units}
    rs = {u: r_ref[u] for u in units}
    scores = {u: _bdot_nt(qs[u], ks[u]) for u in units}
    cross = {u: _bdot(qs[u] * jnp.exp(lg[u[1]] * (idx + 1.0)), rs[u]) for u in units}
    for u in units:
        r_ref[u] = rs[u] * math.exp(lg[u[1]] * c) + _bdot_tn(ks[u] * jnp.exp(lg[u[1]] * (c - 1.0 - idx)), vs[u])
    inners = {u: _bdot(scores[u] * jnp.where(rel >= 0, jnp.exp(lg[u[1]] * jnp.maximum(rel, 0.0)), 0.0), vs[u])
              for u in units}
    outs = {}
    for u in units:
        o = inners[u] + cross[u]
        mu = jnp.mean(o, axis=-1, keepdims=True)
        d = o - mu
        var = jnp.mean(d * d, axis=-1, keepdims=True)
        o = d * lax.rsqrt(var + EPS)
        gate = gate_ref[rows[u[0]], u[1] * DV_B:(u[1] + 1) * DV_B]
        outs[u] = gate * _sigmoid(gate) * o
    for h in range(H_B):
        o = outs[0, h] if nseq == 1 else jnp.concatenate([outs[q, h] for q in range(nseq)], axis=0)
        o_ref[:, h * DV_B:(h + 1) * DV_B] = o.astype(o_ref.dtype)


def _rope_tables(pos):
    half = DK_B // 2
    inv_freq = 1.0 / (ROPE_BASE ** jnp.linspace(0.0, 1.0, half, dtype=F32))
    ang = pos.astype(F32)[:, None] * inv_freq[None, :]
    sin = jnp.sin(ang)
    cos = jnp.cos(ang)
    cos2 = jnp.stack([cos, cos], axis=-1).reshape(-1, DK_B)
    sin2 = jnp.stack([-sin, sin], axis=-1).reshape(-1, DK_B)
    return cos2, sin2


def _retention(proj, cos2, sin2, r0, layer, stack, depth, layer_out, o_buf, row_off, n, l, c, nseq):
    nc = l // c
    assert nseq == 1 or nc == 1
    blk = nseq * c
    base = row_off // blk
    col = lambda off: pl.BlockSpec((blk, BRANCH_W), lambda i, j: (base + i * nc + j, off // BRANCH_W))
    tab = pl.BlockSpec((c, DK_B), lambda i, j: (j, 0))
    state_spec, state_shape = _state_out(stack, depth, layer_out, n, nseq, (H_B, DK_B, DV_B))
    aliased = ([stack] if stack is not None else []) + [o_buf]
    n_in = 6
    return pl.pallas_call(
        functools.partial(_ret_kernel, c=c, nseq=nseq, fresh_layer=layer_out if stack is None else None),
        grid=(n // nseq, nc),
        in_specs=[col(COL_QKB), col(COL_VB), col(COL_GB), tab, tab,
                  pl.BlockSpec((None, nseq, H_B, DK_B, DV_B), lambda i, j: (layer, i, 0, 0, 0))]
        + [pl.BlockSpec(memory_space=pl.ANY)] * len(aliased),
        out_specs=[col(0), state_spec],
        out_shape=[jax.ShapeDtypeStruct(o_buf.shape, o_buf.dtype), state_shape],
        input_output_aliases={n_in + len(aliased) - 1: 0, **({n_in: 1} if stack is not None else {})},
        compiler_params=_cparams(("parallel", "arbitrary")))(proj, proj, proj, cos2, sin2, r0, *aliased)


def _gelu(x):
    return 0.5 * x * (1.0 + lax.erf(x * (2.0 ** -0.5)))


def _gmlp_kernel(u_ref, v_ref, w_ref, bcol_ref, lng_ref, lnb_ref, obuf_ref, o_ref, *vrows_ref, seq):
    u32 = _gelu(u_ref[...])
    v32 = _gelu(v_ref[...])
    mu = jnp.mean(v32, axis=-1, keepdims=True)
    d = v32 - mu
    var = jnp.mean(d * d, axis=-1, keepdims=True)
    vn = d * lax.rsqrt(var + EPS) * lng_ref[...] + lnb_ref[...]
    if vrows_ref:
        vrows_ref[0][...] = vn
    ii = lax.broadcasted_iota(jnp.int32, (CHUNK_C, CHUNK_C), 0)
    jj = lax.broadcasted_iota(jnp.int32, (CHUNK_C, CHUNK_C), 1)
    mask = (ii >= jj) & ((ii & -seq) == (jj & -seq))
    for g in range(G_C):
        w = jnp.where(mask, w_ref[g], 0.0)
        mixed = _bdot(w, vn[:, g * DG_C:(g + 1) * DG_C]) + bcol_ref[:, g:g + 1]
        o_ref[:, g * DG_C:(g + 1) * DG_C] = (u32[:, g * DG_C:(g + 1) * DG_C] * mixed).astype(o_ref.dtype)


def _gmlp(proj, w_tile, b_col, ln_g, ln_b, o_buf, row_off, rows, seq, want_vrows):
    base = row_off // CHUNK_C
    col = lambda off: pl.BlockSpec((CHUNK_C, BRANCH_W), lambda i: (base + i, off // BRANCH_W))
    vec = pl.BlockSpec((1, BRANCH_W), lambda i: (0, 0))
    out_shape = [jax.ShapeDtypeStruct(o_buf.shape, o_buf.dtype)]
    out_specs = [col(0)]
    if want_vrows:
        out_shape.append(jax.ShapeDtypeStruct((rows, BRANCH_W), F32))
        out_specs.append(pl.BlockSpec((CHUNK_C, BRANCH_W), lambda i: (i, 0)))
    res = pl.pallas_call(
        functools.partial(_gmlp_kernel, seq=seq), grid=(rows // CHUNK_C,),
        in_specs=[col(COL_U), col(COL_V),
                  pl.BlockSpec((G_C, CHUNK_C, CHUNK_C), lambda i: (0, 0, 0)),
                  pl.BlockSpec((CHUNK_C, G_C), lambda i: (0, 0)), vec, vec,
                  pl.BlockSpec(memory_space=pl.ANY)],
        out_specs=out_specs, out_shape=out_shape, input_output_aliases={6: 0},
        compiler_params=_cparams(("parallel",)))(
            proj, proj, w_tile, b_col, ln_g.reshape(1, BRANCH_W), ln_b.reshape(1, BRANCH_W), o_buf)
    return res if want_vrows else (res[0], None)


AB_COL0 = 4 * BRANCH_W
AB_COLS = 2 * H_A
WD_BLK, WBR_BLK, PACK_BLK = 64, 32, 128
FFN_BLK = 64


def _pack_kernel(w_ref, o_ref):
    o_ref[...] = w_ref[0].astype(BF16)


def _pack_src_row(j, tn):
    r = j * tn
    return r + jnp.where(r >= AB_COL0, AB_COLS, 0)


def _pack_w_in_t(w_in_t, layer, tn):
    k = w_in_t.shape[2]
    return pl.pallas_call(
        _pack_kernel, grid=(N_PROJ // tn,),
        in_specs=[pl.BlockSpec((pl.Element(1), pl.Element(tn), pl.Element(k)),
                               lambda j: (layer, pl.multiple_of(_pack_src_row(j, tn), 8), 0))],
        out_specs=pl.BlockSpec((tn, k), lambda j: (j, 0)),
        out_shape=jax.ShapeDtypeStruct((N_PROJ, k), BF16),
        compiler_params=_cparams(("parallel",)))(w_in_t)


def _ab_weights_t(w_in_t, layer):
    ab = w_in_t[layer, AB_COL0:AB_COL0 + AB_COLS, :]
    zeros = jnp.zeros((LANES - H_A, ab.shape[1]), F32)
    return jnp.concatenate([ab[:H_A], zeros, ab[H_A:], zeros], axis=0)


def kernel(x_prompt, x_sample, state_gdn, state_conv, state_ret, w_in, conv_w, a_log, dt_bias, gdn_norm, w_s, b_s, ln_c_g, ln_c_b, w_br, w_o, g_pre_mix, g_post_mix, g_pre_ffn, g_post_ffn, w_ffn_gate, w_ffn_up, w_ffn_down):
    n_p, l_p, _ = x_prompt.shape
    n_s, l_s, _ = x_sample.shape
    depth = w_in.shape[0]
    rows_p, rows_s = n_p * l_p, n_s * l_s
    m = rows_p + rows_s
    xs = (x_prompt.reshape(rows_p, D_MODEL), x_sample.reshape(rows_s, D_MODEL))
    tr = ROW_TILE

    cos_p, sin_p = _rope_tables(jnp.arange(l_p))
    cos_s, sin_s = _rope_tables(PAST_LEN + jnp.arange(l_s))
    zero_gdn = jnp.zeros((1, n_p, H_A, DK_A, DV_A), F32)
    zero_ret = jnp.zeros((1, n_p, H_B, DK_B, DV_B), F32)
    zero_buf8 = jnp.zeros((n_p, 8, CONV_DIM), F32)
    reps = CHUNK_C // l_s

    w_in_t = jnp.swapaxes(w_in, 1, 2)

    gdn_p = gdn_s = ret_p = ret_s = None
    outs = {k: [] for k in ("conv_p", "conv_s", "vrows")}
    h = _prenorm(xs[0], xs[1], g_pre_mix[0], tr)
    w_br_rows = w_br.reshape(depth, 3 * BRANCH_W, D_MODEL)
    w_main = _pack_w_in_t(w_in_t, 0, TN_HALF)
    for l in range(depth):
        side = [_SideCast(w_ffn_down, l, WD_BLK, D_FF_PAD // WD_BLK,
                          lambda b: jnp.minimum(b, D_FF // WD_BLK - 1) * WD_BLK, D_FF // WD_BLK),
                _SideCast(w_br_rows, l, WBR_BLK, 3 * BRANCH_W // WBR_BLK, lambda b: b * WBR_BLK,
                          3 * BRANCH_W // WBR_BLK)]
        if l + 1 < depth:
            side.append(_SideCast(w_in_t, l + 1, PACK_BLK, N_PROJ // PACK_BLK,
                                  lambda b: _pack_src_row(b, PACK_BLK), N_PROJ // PACK_BLK))
        proj, wd, w_br_l, *w_main = _matmul_nt(h, w_main, F32, TM, TN, side)
        w_main = w_main[0] if w_main else None
        ab = _matmul_nt(h, _ab_weights_t(w_in_t, l), F32, TM, 2 * LANES)[0]

        gc, beta = _gdn_prep(ab, a_log[l], dt_bias[l], TM, rows_p)
        grow_p = gc[:rows_p, :H_A].reshape(n_p, l_p // MIX_CHUNK, MIX_CHUNK, H_A).transpose(0, 1, 3, 2)
        grow_s = gc[rows_p:, :H_A].reshape(n_s, 1, l_s, H_A).transpose(0, 1, 3, 2)
        buf8_s = jnp.pad(state_conv[l], ((0, 0), (8 - (CONV_W - 1), 0), (0, 0)))
        o_a = jnp.zeros((m, BRANCH_W), BF16)
        o_b = jnp.zeros((m, BRANCH_W), BF16)
        o_c = jnp.zeros((m, BRANCH_W), BF16)
        ffn_side = [_SideCast(w, l, FFN_BLK, D_MODEL // FFN_BLK, lambda b: b * FFN_BLK, D_MODEL // FFN_BLK)
                    for w in (w_ffn_gate, w_ffn_up)]
        o_a, gdn_p, tail_p, wg, wu = _gdn(proj, gc, beta, grow_p, zero_gdn, 0, gdn_p, depth, l, o_a, zero_buf8,
                                          conv_w[l], gdn_norm[l], 0, n_p, l_p, MIX_CHUNK, H_A, 1, ffn_side)
        o_a, gdn_s, tail_s = _gdn(proj, gc, beta, grow_s, state_gdn, l, gdn_s, depth, l, o_a, buf8_s, conv_w[l],
                                  gdn_norm[l], rows_p, n_s, l_s, l_s, GDN_GROUP, GDN_SAMPLE_SEQS)
        outs["conv_p"].append(tail_p[:, 8 - (CONV_W - 1):])
        outs["conv_s"].append(tail_s[:, 8 - (CONV_W - 1):])

        o_b, ret_p = _retention(proj, cos_p, sin_p, zero_ret, 0, ret_p, depth, l, o_b, 0, n_p, l_p, RET_CHUNK, 1)
        o_b, ret_s = _retention(proj, cos_s, sin_s, state_ret, l, ret_s, depth, l, o_b, rows_p, n_s, l_s, l_s,
                                RET_SAMPLE_SEQS)

        o_c, _ = _gmlp(proj, w_s[l], b_s[l].T, ln_c_g[l], ln_c_b[l], o_c, 0, rows_p, CHUNK_C, False)
        w_tile_s = jnp.tile(w_s[l][:, :l_s, :l_s], (1, reps, reps))
        b_col_s = jnp.tile(b_s[l][:, :l_s].T, (reps, 1))
        o_c, vrows = _gmlp(proj, w_tile_s, b_col_s, ln_c_g[l], ln_c_b[l], o_c, rows_p, rows_s, l_s, True)
        outs["vrows"].append(vrows.reshape(n_s, l_s, BRANCH_W))

        merged = _merge(o_a, o_b, o_c, w_br_l.reshape(3, BRANCH_W, D_MODEL), proj, TM, TN_HALF)
        mix = _matmul_ws(merged, w_o, l, BF16, TM, TN_HALF)
        x, h = _post(xs, mix, g_post_mix[l], g_pre_ffn[l], tr)

        f1 = _ffn_up(h, wg, wu, TM, TN_HALF, D_FF_PAD)
        f = _matmul_k2(f1, wd, BF16, TM, TN)
        if l + 1 < depth:
            x, h = _post((x,), f, g_post_ffn[l], g_pre_mix[l + 1], tr)
            xs = (x,)
        else:
            y_p, y_s = _post_split(x, f, g_post_ffn[l], rows_p, tr)

    y_p = y_p.reshape(n_p, l_p, D_MODEL)
    y_s = y_s.reshape(n_s, l_s, D_MODEL)
    st = lambda k: jnp.stack(outs[k])
    return (y_p, y_s, gdn_p, gdn_s, st("conv_p"), st("conv_s"), ret_p, ret_s, st("vrows"))
```
